```python
import math
import jax, jax.numpy as jnp
from jax import lax
import numpy as np

D_MODEL = 1024
BATCH = 16
SEQ = 256
DEPTH = 4
DEC_BATCH = 2
DEC_SEQ = 4096
PAST_LEN = 256

GRID_W = 64
H_R = 8
HEAD_R = 64
D_R = H_R * HEAD_R
LORA_W = 64
LORA_A = 64
LORA_G = 128
SHIFT_W = 3
H_N = 8
HEAD_N = 64
D_N = H_N * HEAD_N
KH_MAX = 8
KW = 16
D_FF = 2816
N_EXPERTS = 8
TOP_K = 2
D_FF_EXPERT = 3584
MOE_BLOCK = 128
N_DENSE = (DEPTH + 1) // 2
N_MOE = DEPTH // 2
ALPHA = (2 * DEPTH) ** 0.25
BETA = (8 * DEPTH) ** -0.25
LN_EPS = 1e-5
GN_EPS = 64e-5
RWKV_COLS = 3 * D_R + 2 * LORA_W + 2 * LORA_A + LORA_G
NA_COLS = 3 * D_N
GATE_COLS = 2 * D_MODEL
IN_COLS = RWKV_COLS + NA_COLS + GATE_COLS

kernel_name = 'hybrid_rwkv7_natten_deepnorm_dit_step'


def layer_norm(x, g, b):
    xf = x.astype(jnp.float32)
    mu = jnp.mean(xf, -1, keepdims=True)
    var = jnp.mean(jnp.square(xf - mu), -1, keepdims=True)
    return ((xf - mu) * lax.rsqrt(var + LN_EPS)).astype(x.dtype) * g + b


def centred_shift(x, w):
    xp = jnp.pad(x, ((0, 0), (1, 1), (0, 0)))
    return w[0] * xp[:, :-2] + w[1] * xp[:, 1:-1] + w[2] * xp[:, 2:]


def rwkv7_scan(s0, r, w, kk, a, kt, v, reverse):
    def step(s, inp):
        r_t, w_t, kk_t, a_t, kt_t, v_t = inp
        sa = jnp.einsum('bhvk,bhk->bhv', s, kk_t)
        s = (s * w_t[:, :, None, :] - sa[..., None] * (kk_t * a_t)[:, :, None, :]
             + v_t[..., None] * kt_t[:, :, None, :])
        return s, jnp.einsum('bhvk,bhk->bhv', s, r_t)
    xs = tuple(jnp.moveaxis(t.astype(jnp.float32), 1, 0) for t in (r, w, kk, a, kt, v))
    s_fin, ys = lax.scan(step, s0.astype(jnp.float32), xs, reverse=reverse)
    return s_fin, jnp.moveaxis(ys, 0, 1)


def rwkv_branch(slab, p, l, s0_f, s0_b):
    B, T, _ = slab.shape
    r, k, v, wd, ad, gd = jnp.split(
        slab, [D_R, 2 * D_R, 3 * D_R, 3 * D_R + 2 * LORA_W, 3 * D_R + 2 * LORA_W + 2 * LORA_A], axis=-1)
    wd = wd.reshape(B, T, 2, LORA_W)
    ad = ad.reshape(B, T, 2, LORA_A)
    w_logit = p['rw_w0'][l] + jnp.einsum('btdr,drc->btdc', jnp.tanh(wd), p['rw_w2'][l])
    decay = jnp.exp(-jnp.exp(-jax.nn.softplus(-w_logit) - 0.5))
    a = jax.nn.sigmoid(p['rw_a0'][l] + jnp.einsum('btdr,drc->btdc', ad, p['rw_a2'][l]))
    g = jnp.einsum('btr,rc->btc', jax.nn.sigmoid(gd), p['rw_g2'][l])
    hd = lambda t: t.reshape(B, T, H_R, HEAD_R)
    kk = hd(k * p['rw_kk'][l]).astype(jnp.float32)
    kk = kk * lax.rsqrt(jnp.sum(jnp.square(kk), -1, keepdims=True) + 1e-12)
    kt = k[:, :, None, :] * (1 + (a - 1) * p['rw_ka'][l])
    s_f, y_f = rwkv7_scan(s0_f, hd(r), hd(decay[:, :, 0]), kk, hd(a[:, :, 0]), hd(kt[:, :, 0]), hd(v), False)
    s_b, y_b = rwkv7_scan(s0_b, hd(r), hd(decay[:, :, 1]), kk, hd(a[:, :, 1]), hd(kt[:, :, 1]), hd(v), True)
    y = y_f + y_b
    mu = jnp.mean(y, -1, keepdims=True)
    var = jnp.mean(jnp.square(y - mu), -1, keepdims=True)
    y = ((y - mu) * lax.rsqrt(var + GN_EPS)).reshape(B, T, D_R).astype(slab.dtype)
    y = y * p['rw_gn_g'][l] + p['rw_gn_b'][l]
    kt_mean = 0.5 * (kt[:, :, 0] + kt[:, :, 1])
    bonus = jnp.sum(hd(r) * hd(kt_mean) * p['rw_u'][l], -1, keepdims=True) * hd(v)
    y = (y + bonus.reshape(B, T, D_R)) * g
    return y, s_f, s_b


def ctx_attention(q, k, v):
    s = jnp.einsum('bqhd,bkhd->bhqk', q, k).astype(jnp.float32) * (HEAD_N ** -0.5)
    pr = jax.nn.softmax(s, -1).astype(v.dtype)
    return jnp.einsum('bhqk,bkhd->bqhd', pr, v)


def neighbourhood_attention(q, k, v, k_ctx, v_ctx, rpb):
    B, T, H, d = q.shape
    rows = T // GRID_W
    kh = min(KH_MAX, rows)
    qg = q.reshape(B, rows, GRID_W, H, d)
    kg = k.reshape(B, rows, GRID_W, H, d)
    vg = v.reshape(B, rows, GRID_W, H, d)
    col_start = np.clip(np.arange(GRID_W) - KW // 2, 0, GRID_W - KW)
    col_idx = col_start[:, None] + np.arange(KW)[None, :]
    col_off = col_idx - np.arange(GRID_W)[:, None] + (KW - 1)
    rpb_cols = jnp.transpose(rpb[:, :, col_off], (0, 2, 1, 3))
    scale = HEAD_N ** -0.5
    n_loc = kh * KW

    def row_block(i):
        rs = jnp.clip(i - kh // 2, 0, rows - kh)
        q_row = lax.dynamic_index_in_dim(qg, i, axis=1, keepdims=False)
        k_rows = lax.dynamic_slice_in_dim(kg, rs, kh, axis=1)
        v_rows = lax.dynamic_slice_in_dim(vg, rs, kh, axis=1)
        k_win = k_rows[:, :, col_idx]
        v_win = v_rows[:, :, col_idx]
        row_off = rs + jnp.arange(kh) - i + (KH_MAX - 1)
        bias = rpb_cols[:, :, row_off]
        s_loc = jnp.einsum('bwhd,bawkhd->bhwak', q_row, k_win).astype(jnp.float32) * scale + bias
        s_ctx = jnp.einsum('bwhd,bchd->bhwc', q_row, k_ctx).astype(jnp.float32) * scale
        s = jnp.concatenate([s_loc.reshape(B, H, GRID_W, n_loc), s_ctx], -1)
        pr = jax.nn.softmax(s, -1).astype(v.dtype)
        p_loc = pr[..., :n_loc].reshape(B, H, GRID_W, kh, KW)
        p_ctx = pr[..., n_loc:]
        return (jnp.einsum('bhwak,bawkhd->bwhd', p_loc, v_win)
                + jnp.einsum('bhwc,bchd->bwhd', p_ctx, v_ctx))

    out = lax.map(row_block, jnp.arange(rows, dtype=jnp.int32))
    return jnp.moveaxis(out, 0, 1).reshape(B, T, H * d)


def mixer_sublayer(h, p, l, ctx):
    B, T, _ = h.shape
    proj = jnp.einsum('btd,dc->btc', h, p['w_in'][l])
    slab = centred_shift(proj[..., :RWKV_COLS], p['ts_w'][l])
    qkv = proj[..., RWKV_COLS:RWKV_COLS + NA_COLS].reshape(B, T, 3, H_N, HEAD_N)
    qn, kn, vn = qkv[:, :, 0], qkv[:, :, 1], qkv[:, :, 2]
    gates = jax.nn.sigmoid(proj[..., RWKV_COLS + NA_COLS:])
    g_r, g_n = gates[..., :D_MODEL], gates[..., D_MODEL:]
    if ctx is None:
        zeros = jnp.zeros((B, H_R, HEAD_R, HEAD_R), jnp.float32)
        y_r, s_f, s_b = rwkv_branch(slab, p, l, zeros, zeros)
        y_n = ctx_attention(qn, kn, vn).reshape(B, T, D_N)
        new = (s_f, s_b, kn, vn)
    else:
        k_ctx, v_ctx, s0_f, s0_b = ctx
        y_r, _, _ = rwkv_branch(slab, p, l, s0_f, s0_b)
        y_n = neighbourhood_attention(qn, kn, vn, k_ctx, v_ctx, p['na_rpb'][l])
        new = None
    m = g_r * (y_r @ p['w_up_r'][l]) + g_n * (y_n @ p['w_up_n'][l])
    return m @ p['w_o'][l], new


def swiglu(x, w1, w3, w2):
    return (jax.nn.silu(x @ w1) * (x @ w3)) @ w2


def moe(x, router, w1, w3, w2):
    B, T, D = x.shape
    N = B * T
    xf = x.reshape(N, D)
    logits = (xf @ router).astype(jnp.float32)
    top_v, top_i = lax.top_k(logits, TOP_K)
    gates = jax.nn.softmax(top_v, -1).astype(x.dtype)
    flat_e = top_i.reshape(-1)
    flat_tok = jnp.repeat(jnp.arange(N, dtype=jnp.int32), TOP_K)
    flat_g = gates.reshape(-1)
    order = jnp.argsort(flat_e)
    e_s, tok_s, g_s = flat_e[order], flat_tok[order], flat_g[order]
    counts = jnp.bincount(flat_e, length=N_EXPERTS)
    padded = (counts + MOE_BLOCK - 1) // MOE_BLOCK * MOE_BLOCK
    start = jnp.cumsum(counts) - counts
    pad_end = jnp.cumsum(padded)
    pad_start = pad_end - padded
    dest = pad_start[e_s] + (jnp.arange(N * TOP_K, dtype=jnp.int32) - start[e_s])
    n_blocks = -(-(N * TOP_K + N_EXPERTS * (MOE_BLOCK - 1)) // MOE_BLOCK)
    P = n_blocks * MOE_BLOCK
    buf_tok = jnp.full((P,), N, jnp.int32).at[dest].set(tok_s)
    buf_g = jnp.zeros((P,), x.dtype).at[dest].set(g_s)
    block_e = jnp.minimum(jnp.searchsorted(pad_end, jnp.arange(n_blocks) * MOE_BLOCK, side='right'),
                          N_EXPERTS - 1)
    xb = jnp.concatenate([xf, jnp.zeros((1, D), x.dtype)], 0)[buf_tok].reshape(n_blocks, MOE_BLOCK, D)

    def expert_block(args):
        xblk, e = args
        return swiglu(xblk, w1[e], w3[e], w2[e])

    yb = lax.map(expert_block, (xb, block_e)).reshape(P, D)
    y = jax.ops.segment_sum(yb * buf_g[:, None], buf_tok, num_segments=N + 1)[:N]
    return y.reshape(B, T, D)


def layer(x, cond_mod, p, l, ctx):
    sh1, sc1, g1, sh2, sc2, g2 = jnp.split(cond_mod[:, None, :], 6, axis=-1)
    h = x * (1 + sc1) + sh1
    o, new = mixer_sublayer(h, p, l, ctx)
    x = layer_norm(ALPHA * x + g1 * o, p['ln1_g'][l], p['ln1_b'][l])
    h = x * (1 + sc2) + sh2
    if l % 2 == 0:
        f = swiglu(h, p['ffn_w1'][l // 2], p['ffn_w3'][l // 2], p['ffn_w2'][l // 2])
    else:
        f = moe(h, p['moe_router'][l // 2], p['moe_w1'][l // 2], p['moe_w3'][l // 2], p['moe_w2'][l // 2])
    x = layer_norm(ALPHA * x + g2 * f, p['ln2_g'][l], p['ln2_b'][l])
    return x, new


def setup_inputs(seed: int = 0) -> dict:
    key = jax.random.key(seed)
    ks = iter(jax.random.split(key, 48))
    nrm = lambda shape, s=1.0: s * jax.random.normal(next(ks), shape, jnp.float32)
    D = D_MODEL
    return {
        'x_prompt': nrm((BATCH, SEQ, D)),
        'x_sample': nrm((DEC_BATCH, DEC_SEQ, D)),
        'c': nrm((DEC_BATCH, D)),
        'state_rwkv': nrm((DEC_BATCH, DEPTH, 2, H_R, HEAD_R, HEAD_R), 0.5),
        'cache_k': nrm((DEC_BATCH, DEPTH, PAST_LEN, H_N, HEAD_N)),
        'cache_v': nrm((DEC_BATCH, DEPTH, PAST_LEN, H_N, HEAD_N)),
        'c_ctx': nrm((D,)),
        'w_ada': nrm((DEPTH, D, 6 * D), 0.5 * D ** -0.5),
        'b_ada': nrm((DEPTH, 6 * D), 0.01),
        'w_in': nrm((DEPTH, D, IN_COLS), D ** -0.5),
        'ts_w': jnp.array([0.25, 0.5, 0.25], jnp.float32)[None, :, None] + nrm((DEPTH, SHIFT_W, RWKV_COLS), 0.05),
        'rw_w0': nrm((DEPTH, 2, D_R), 0.5),
        'rw_w2': nrm((DEPTH, 2, LORA_W, D_R), 0.1 * LORA_W ** -0.5),
        'rw_a0': nrm((DEPTH, 2, D_R), 0.5),
        'rw_a2': nrm((DEPTH, 2, LORA_A, D_R), 0.1 * LORA_A ** -0.5),
        'rw_g2': nrm((DEPTH, LORA_G, D_R), LORA_G ** -0.5),
        'rw_kk': 1.0 + nrm((DEPTH, D_R), 0.1),
        'rw_ka': 1.0 + nrm((DEPTH, D_R), 0.1),
        'rw_u': nrm((DEPTH, H_R, HEAD_R), 0.5),
        'rw_gn_g': 1.0 + nrm((DEPTH, D_R), 0.1),
        'rw_gn_b': nrm((DEPTH, D_R), 0.01),
        'na_rpb': nrm((DEPTH, H_N, 2 * KH_MAX - 1, 2 * KW - 1), 0.1),
        'w_up_r': nrm((DEPTH, D_R, D), D_R ** -0.5),
        'w_up_n': nrm((DEPTH, D_N, D), D_N ** -0.5),
        'w_o': nrm((DEPTH, D, D), BETA * D ** -0.5),
        'ln1_g': 1.0 + nrm((DEPTH, D), 0.1),
        'ln1_b': nrm((DEPTH, D), 0.01),
        'ln2_g': 1.0 + nrm((DEPTH, D), 0.1),
        'ln2_b': nrm((DEPTH, D), 0.01),
        'ffn_w1': nrm((N_DENSE, D, D_FF), D ** -0.5),
        'ffn_w3': nrm((N_DENSE, D, D_FF), D ** -0.5),
        'ffn_w2': nrm((N_DENSE, D_FF, D), BETA * D_FF ** -0.5),
        'moe_router': nrm((N_MOE, D, N_EXPERTS), D ** -0.5),
        'moe_w1': nrm((N_MOE, N_EXPERTS, D, D_FF_EXPERT), D ** -0.5),
        'moe_w3': nrm((N_MOE, N_EXPERTS, D, D_FF_EXPERT), D ** -0.5),
        'moe_w2': nrm((N_MOE, N_EXPERTS, D_FF_EXPERT, D), BETA * D_FF_EXPERT ** -0.5),
    }


def reference(x_prompt, x_sample, c, state_rwkv, cache_k, cache_v, c_ctx, w_ada, b_ada, w_in, ts_w,
              rw_w0, rw_w2, rw_a0, rw_a2, rw_g2, rw_kk, rw_ka, rw_u, rw_gn_g, rw_gn_b, na_rpb,
              w_up_r, w_up_n, w_o, ln1_g, ln1_b, ln2_g, ln2_b, ffn_w1, ffn_w3, ffn_w2,
              moe_router, moe_w1, moe_w3, moe_w2):
    p = dict(w_in=w_in, ts_w=ts_w, rw_w0=rw_w0, rw_w2=rw_w2, rw_a0=rw_a0, rw_a2=rw_a2, rw_g2=rw_g2,
             rw_kk=rw_kk, rw_ka=rw_ka, rw_u=rw_u, rw_gn_g=rw_gn_g, rw_gn_b=rw_gn_b, na_rpb=na_rpb,
             w_up_r=w_up_r, w_up_n=w_up_n, w_o=w_o, ln1_g=ln1_g, ln1_b=ln1_b, ln2_g=ln2_g, ln2_b=ln2_b,
             ffn_w1=ffn_w1, ffn_w3=ffn_w3, ffn_w2=ffn_w2, moe_router=moe_router,
             moe_w1=moe_w1, moe_w3=moe_w3, moe_w2=moe_w2)
    xp = x_prompt
    new_s, new_k, new_v = [], [], []
    for l in range(DEPTH):
        mod = jax.nn.silu(c_ctx)[None, :] @ w_ada[l] + b_ada[l]
        xp, (s_f, s_b, k_l, v_l) = layer(xp, mod, p, l, None)
        new_s.append(jnp.stack([s_f, s_b], axis=1))
        new_k.append(k_l)
        new_v.append(v_l)
    xs = x_sample
    for l in range(DEPTH):
        mod = jax.nn.silu(c) @ w_ada[l] + b_ada[l]
        ctx = (cache_k[:, l], cache_v[:, l], state_rwkv[:, l, 0], state_rwkv[:, l, 1])
        xs, _ = layer(xs, mod, p, l, ctx)
    new_state_rwkv = jnp.stack(new_s, axis=1).astype(x_prompt.dtype)
    new_cache_k = jnp.stack(new_k, axis=1)
    new_cache_v = jnp.stack(new_v, axis=1)
    return (xp, xs, new_state_rwkv, new_cache_k, new_cache_v)
```

```python
import functools

import numpy as np
import jax
import jax.numpy as jnp
from jax import lax
from jax.experimental import pallas as pl
from jax.experimental.pallas import tpu as pltpu

F32 = jnp.float32
BF16 = jnp.bfloat16
HIGHEST = lax.Precision.HIGHEST

N_HEADS = 8
HEAD_DIM = 64
D_BRANCH = N_HEADS * HEAD_DIM
V7X_LANES = 128
ROW_BLOCK = 256
SCAN_CHUNK = 128
MOE_ROWS = 256
KH_MAX = 8
KW = 16
GRID_W = 64
LN_EPS = 1e-5
GN_EPS = 64e-5
NEG_BIG = -1e30
VMEM_LIMIT = 56 * 1024 * 1024


def _params(*sem):
    return pltpu.CompilerParams(dimension_semantics=sem, vmem_limit_bytes=VMEM_LIMIT)


def _const_spec(shape):
    nd = len(shape)
    return pl.BlockSpec(shape, lambda *_: (0,) * nd, pipeline_mode=pl.Buffered(1))


def _dot(a, b):
    return jnp.dot(a, b, preferred_element_type=F32)


def _dot_nt(a, b):
    return lax.dot_general(a, b, (((1,), (1,)), ((), ())), preferred_element_type=F32)


def _sigmoid(x):
    return 1.0 / (1.0 + jnp.exp(-x))


def _head_sum(x, bd):
    outs = []
    for j in range(x.shape[1] // V7X_LANES):
        xs = x[:, j * V7X_LANES:(j + 1) * V7X_LANES]
        hi = xs.astype(BF16)
        lo = (xs - hi.astype(F32)).astype(BF16)
        outs.append(_dot(hi, bd) + _dot(lo, bd))
    return jnp.concatenate(outs, axis=1)


def _layer_norm(z, g, b):
    mu = jnp.mean(z, axis=-1, keepdims=True)
    zc = z - mu
    var = jnp.mean(zc * zc, axis=-1, keepdims=True)
    return zc * lax.rsqrt(var + LN_EPS) * g + b


def _ada_kernel(c_ref, w_ref, b_ref, o_ref):
    cs = c_ref[...]
    s = cs * _sigmoid(cs)
    o_ref[0] = jnp.dot(s, w_ref[0], preferred_element_type=F32, precision=HIGHEST) + b_ref[0]


def _ada_mod(cvec, w_ada, b_ada):
    depth, d, n6 = w_ada.shape
    tn = n6 // 4
    return pl.pallas_call(
        _ada_kernel,
        grid=(depth, n6 // tn),
        in_specs=[pl.BlockSpec((8, d), lambda l, j: (0, 0)),
                  pl.BlockSpec((1, d, tn), lambda l, j: (l, 0, j)),
                  pl.BlockSpec((1, 1, tn), lambda l, j: (l, 0, j))],
        out_specs=pl.BlockSpec((1, 8, tn), lambda l, j: (l, 0, j)),
        out_shape=jax.ShapeDtypeStruct((depth, 8, n6), F32),
        name="ada_mod",
        compiler_params=_params("parallel", "parallel"),
    )(cvec, w_ada, b_ada.reshape(depth, 1, n6))


def _inproj_kernel(x_ref, mod_ref, w_ref, slab_ref, qkv_ref, gate_ref, *, n_slab, n_qkv):
    sh = mod_ref[0, 0:1, :]
    sc = mod_ref[0, 1:2, :]
    h = (x_ref[...] * (1.0 + sc) + sh).astype(BF16)
    slab_ref[...] = _dot(h, w_ref[:, 0:n_slab])
    qkv_ref[...] = _dot(h, w_ref[:, n_slab:n_slab + n_qkv])
    gate_ref[...] = _sigmoid(_dot(h, w_ref[:, n_slab + n_qkv:]))


def _in_proj(x, modp, w_in, t_len, n_slab, n_qkv):
    n, d = x.shape
    n_cols = w_in.shape[1]
    n_gate = n_cols - n_slab - n_qkv
    per_seq = t_len // ROW_BLOCK
    return pl.pallas_call(
        functools.partial(_inproj_kernel, n_slab=n_slab, n_qkv=n_qkv),
        grid=(n // ROW_BLOCK,),
        in_specs=[pl.BlockSpec((ROW_BLOCK, d), lambda i: (i, 0)),
                  pl.BlockSpec((1, 6, d), lambda i: (i // per_seq, 0, 0)),
                  _const_spec((d, n_cols))],
        out_specs=[pl.BlockSpec((ROW_BLOCK, n_slab), lambda i: (i, 0)),
                   pl.BlockSpec((ROW_BLOCK, n_qkv), lambda i: (i, 0)),
                   pl.BlockSpec((ROW_BLOCK, n_gate), lambda i: (i, 0))],
        out_shape=[jax.ShapeDtypeStruct((n, n_slab), F32),
                   jax.ShapeDtypeStruct((n, n_qkv), F32),
                   jax.ShapeDtypeStruct((n, n_gate), F32)],
        name="in_proj",
        compiler_params=_params("parallel"),
    )(x, modp, w_in)


def _store_head_padded(o_ref, x):
    zeros = jnp.zeros((x.shape[0], V7X_LANES - HEAD_DIM), F32)
    for h in range(N_HEADS):
        o_ref[:, h * V7X_LANES:h * V7X_LANES + HEAD_DIM] = x[:, h * HEAD_DIM:(h + 1) * HEAD_DIM]
        o_ref[:, h * V7X_LANES + HEAD_DIM:(h + 1) * V7X_LANES] = zeros


def _prep_kernel(cur_ref, prev_ref, next_ref, ts_ref, w0_ref, w2_ref, a0_ref, a2_ref, g2_ref,
                 kkw_ref, ka_ref, u_ref, bd_ref,
                 r_o, kk_o, wf_o, wb_o, bf_o, bb_o, ktf_o, ktb_o, vt_o, bonus_o, g_o,
                 *, per_seq, lora_w, lora_a):
    i = pl.program_id(0)
    tb, n_slab = cur_ref.shape
    dr = D_BRANCH
    cur = cur_ref[...]
    first = (i % per_seq) == 0
    last = (i % per_seq) == per_seq - 1
    prow = jnp.where(first, 0.0, prev_ref[7:8, :])
    nrow = jnp.where(last, 0.0, next_ref[0:1, :])
    rid = lax.broadcasted_iota(jnp.int32, (tb, n_slab), 0)
    xp = jnp.where(rid == 0, prow, pltpu.roll(cur, 1, 0))
    xn = jnp.where(rid == tb - 1, nrow, pltpu.roll(cur, tb - 1, 0))
    s = ts_ref[0:1, :] * xp + ts_ref[1:2, :] * cur + ts_ref[2:3, :] * xn

    r = s[:, 0:dr]
    k = s[:, dr:2 * dr]
    v = s[:, 2 * dr:3 * dr]
    o = 3 * dr
    wd = s[:, o:o + 2 * lora_w]
    ad = s[:, o + 2 * lora_w:o + 2 * lora_w + 2 * lora_a]
    gd = s[:, o + 2 * lora_w + 2 * lora_a:]

    wl = w0_ref[...] + _dot(jnp.tanh(wd).astype(BF16), w2_ref[...])
    decay = jnp.exp(-float(np.exp(-0.5)) * _sigmoid(wl))
    a = _sigmoid(a0_ref[...] + _dot(ad.astype(BF16), a2_ref[...]))
    g = _dot(_sigmoid(gd).astype(BF16), g2_ref[...])

    bd = bd_ref[...]
    kk = k * kkw_ref[...]
    kk = kk * lax.rsqrt(_head_sum(kk * kk, bd) + 1e-12)
    ka = ka_ref[...]
    a_f = a[:, :dr]
    a_b = a[:, dr:]
    kt_f = k * (1.0 + (a_f - 1.0) * ka)
    kt_b = k * (1.0 + (a_b - 1.0) * ka)
    bonus = _head_sum(r * (0.5 * (kt_f + kt_b)) * u_ref[...], bd) * v

    _store_head_padded(r_o, r)
    _store_head_padded(kk_o, kk)
    _store_head_padded(wf_o, decay[:, :dr])
    _store_head_padded(wb_o, decay[:, dr:])
    _store_head_padded(bf_o, kk * a_f)
    _store_head_padded(bb_o, kk * a_b)
    _store_head_padded(ktf_o, kt_f)
    _store_head_padded(ktb_o, kt_b)
    vt_o[0] = v.T
    bonus_o[...] = bonus
    g_o[...] = g


def _rwkv_prep(slab, lw, b_len, t_len):
    n, n_slab = slab.shape
    dr = D_BRANCH
    per_seq = t_len // ROW_BLOCK
    sub = ROW_BLOCK // 8
    lora_w = lw["w2"].shape[0] // 2
    lora_a = lw["a2"].shape[0] // 2
    pad = N_HEADS * V7X_LANES
    row = lambda i: (i, 0)
    consts = [lw["ts"], lw["w0"], lw["w2"], lw["a0"], lw["a2"], lw["g2"], lw["kkw"], lw["ka"], lw["u"], lw["bd"]]
    outs = pl.pallas_call(
        functools.partial(_prep_kernel, per_seq=per_seq, lora_w=lora_w, lora_a=lora_a),
        grid=(n // ROW_BLOCK,),
        in_specs=[pl.BlockSpec((ROW_BLOCK, n_slab), row),
                  pl.BlockSpec((8, n_slab), lambda i: (jnp.maximum(i * sub - 1, 0), 0)),
                  pl.BlockSpec((8, n_slab), lambda i: (jnp.minimum((i + 1) * sub, n // 8 - 1), 0))]
                 + [_const_spec(c.shape) for c in consts],
        out_specs=[pl.BlockSpec((ROW_BLOCK, pad), row)] * 8
                  + [pl.BlockSpec((1, dr, ROW_BLOCK), lambda i: (i // per_seq, 0, i % per_seq)),
                     pl.BlockSpec((ROW_BLOCK, dr), row),
                     pl.BlockSpec((ROW_BLOCK, dr), row)],
        out_shape=[jax.ShapeDtypeStruct((n, pad), F32)] * 8
                  + [jax.ShapeDtypeStruct((b_len, dr, t_len), F32),
                     jax.ShapeDtypeStruct((n, dr), F32),
                     jax.ShapeDtypeStruct((n, dr), F32)],
        name="rwkv_prep",
        compiler_params=_params("parallel"),
    )(slab, slab, slab, *consts)
    return outs


def _scan_kernel(s0_ref, rf, kkf, wf, bf, ktf, vtf, rb, kkb, wb, bb, ktb, vtb,
                 ytf_o, ytb_o, sfin_o, s_scr):
    j = pl.program_id(1)
    nj = pl.num_programs(1)
    tc = vtf.shape[2]

    @pl.when(j == 0)
    def _():
        s_scr[...] = jnp.zeros(s_scr.shape, F32)
        s_scr[:, :, :, 0:HEAD_DIM] = s0_ref[0]

    ytf_o[...] = jnp.zeros(ytf_o.shape, F32)
    ytb_o[...] = jnp.zeros(ytb_o.shape, F32)
    lane = lax.broadcasted_iota(jnp.int32, (HEAD_DIM, tc), 1)
    dirs = ((rf, kkf, wf, bf, ktf, vtf, ytf_o), (rb, kkb, wb, bb, ktb, vtb, ytb_o))

    def group(gi, carry):
        for d, (r_ref, kk_ref, w_ref, b_ref, kt_ref, vt_ref, yt_ref) in enumerate(dirs):
            base = pl.multiple_of(gi * 8 if d == 0 else tc - 8 - gi * 8, 8)
            kk8 = kk_ref[pl.ds(base, 8), :]
            w8 = w_ref[pl.ds(base, 8), :]
            b8 = b_ref[pl.ds(base, 8), :]
            kt8 = kt_ref[pl.ds(base, 8), :]
            r8 = r_ref[pl.ds(base, 8), :]
            for si in range(8):
                u = si if d == 0 else 7 - si
                sel = lane == base + u
                for h in range(N_HEADS):
                    cols = slice(h * V7X_LANES, (h + 1) * V7X_LANES)
                    rows = slice(h * HEAD_DIM, (h + 1) * HEAD_DIM)
                    s = s_scr[d, h]
                    sa = jnp.sum(s * kk8[u:u + 1, cols], axis=1, keepdims=True)
                    vcol = jnp.sum(jnp.where(sel, vt_ref[0, rows, :], 0.0), axis=1, keepdims=True)
                    s = s * w8[u:u + 1, cols] - sa * b8[u:u + 1, cols] + vcol * kt8[u:u + 1, cols]
                    s_scr[d, h] = s
                    y = jnp.sum(s * r8[u:u + 1, cols], axis=1, keepdims=True)
                    yt_ref[0, rows, :] = jnp.where(sel, y, yt_ref[0, rows, :])
        return carry

    lax.fori_loop(0, tc // 8, group, 0)

    @pl.when(j == nj - 1)
    def _():
        sfin_o[0] = s_scr[:, :, :, 0:HEAD_DIM]


def _rwkv_scan(s0, r, kk, wf, wb, bf, bb, ktf, ktb, vt, b_len, t_len):
    tc = SCAN_CHUNK
    nj = t_len // tc
    pad = N_HEADS * V7X_LANES
    dr = D_BRANCH
    fwd = pl.BlockSpec((tc, pad), lambda b, j: (b * nj + j, 0))
    bwd = pl.BlockSpec((tc, pad), lambda b, j: (b * nj + nj - 1 - j, 0))
    vfwd = pl.BlockSpec((1, dr, tc), lambda b, j: (b, 0, j))
    vbwd = pl.BlockSpec((1, dr, tc), lambda b, j: (b, 0, nj - 1 - j))
    st = pl.BlockSpec((1, 2, N_HEADS, HEAD_DIM, HEAD_DIM), lambda b, j: (b, 0, 0, 0, 0))
    return pl.pallas_call(
        _scan_kernel,
        grid=(b_len, nj),
        in_specs=[st, fwd, fwd, fwd, fwd, fwd, vfwd, bwd, bwd, bwd, bwd, bwd, vbwd],
        out_specs=[vfwd, vbwd, st],
        out_shape=[jax.ShapeDtypeStruct((b_len, dr, t_len), F32),
                   jax.ShapeDtypeStruct((b_len, dr, t_len), F32),
                   jax.ShapeDtypeStruct((b_len, 2, N_HEADS, HEAD_DIM, HEAD_DIM), F32)],
        scratch_shapes=[pltpu.VMEM((2, N_HEADS, HEAD_DIM, V7X_LANES), F32)],
        name="rwkv_scan",
        compiler_params=_params("parallel", "arbitrary"),
    )(s0, r, kk, wf, bf, ktf, vt, r, kk, wb, bb, ktb, vt)


def _softmax_rows(parts):
    m = parts[0].max(axis=-1, keepdims=True)
    for p in parts[1:]:
        m = jnp.maximum(m, p.max(axis=-1, keepdims=True))
    es = [jnp.exp(p - m) for p in parts]
    den = es[0].sum(axis=-1, keepdims=True)
    for e in es[1:]:
        den = den + e.sum(axis=-1, keepdims=True)
    inv = 1.0 / den
    return [e * inv for e in es]


def _ctx_attn_kernel(q_ref, k_ref, v_ref, o_ref):
    scale = HEAD_DIM ** -0.5
    outs = []
    for hh in range(2):
        cols = slice(hh * HEAD_DIM, (hh + 1) * HEAD_DIM)
        q = q_ref[:, cols].astype(BF16)
        k = k_ref[:, cols].astype(BF16)
        v = v_ref[:, cols].astype(BF16)
        (p,) = _softmax_rows([_dot_nt(q, k) * scale])
        outs.append(_dot(p.astype(BF16), v))
    o_ref[...] = jnp.concatenate(outs, axis=1).astype(o_ref.dtype)


def _ctx_attention(qkv, b_len, t_len):
    n = qkv.shape[0]
    npair = N_HEADS // 2
    return pl.pallas_call(
        _ctx_attn_kernel,
        grid=(b_len, npair),
        in_specs=[pl.BlockSpec((t_len, V7X_LANES), lambda b, p: (b, p)),
                  pl.BlockSpec((t_len, V7X_LANES), lambda b, p: (b, npair + p)),
                  pl.BlockSpec((t_len, V7X_LANES), lambda b, p: (b, 2 * npair + p))],
        out_specs=pl.BlockSpec((t_len, V7X_LANES), lambda b, p: (b, p)),
        out_shape=jax.ShapeDtypeStruct((n, D_BRANCH), BF16),
        name="ctx_attn",
        compiler_params=_params("parallel", "parallel"),
    )(qkv, qkv, qkv)


def _nbr_attn_kernel(q_ref, k_ref, v_ref, kc_ref, vc_ref, bias_ref, o_ref, *, rows, kh):
    scale = HEAD_DIM ** -0.5
    win = kh * GRID_W

    def body(i, carry):
        rs = jnp.clip(i - kh // 2, 0, rows - kh)
        dd = i - rs
        q0 = pl.multiple_of(i * GRID_W, GRID_W)
        k0 = pl.multiple_of(rs * GRID_W, GRID_W)
        qb = q_ref[pl.ds(q0, GRID_W), :]
        kw = k_ref[pl.ds(k0, win), :]
        vw = v_ref[pl.ds(k0, win), :]
        outs = []
        for hh in range(2):
            cols = slice(hh * HEAD_DIM, (hh + 1) * HEAD_DIM)
            q = qb[:, cols].astype(BF16)
            s_loc = _dot_nt(q, kw[:, cols].astype(BF16)) * scale + bias_ref[hh, dd]
            s_ctx = _dot_nt(q, kc_ref[0, :, cols].astype(BF16)) * scale
            p_loc, p_ctx = _softmax_rows([s_loc, s_ctx])
            outs.append(_dot(p_loc.astype(BF16), vw[:, cols].astype(BF16))
                        + _dot(p_ctx.astype(BF16), vc_ref[0, :, cols].astype(BF16)))
        o_ref[pl.ds(q0, GRID_W), :] = jnp.concatenate(outs, axis=1).astype(o_ref.dtype)
        return carry

    lax.fori_loop(0, rows, body, 0)


def _nbr_attention(qkv, k_ctx, v_ctx, bias, b_len, t_len):
    n = qkv.shape[0]
    npair = N_HEADS // 2
    rows = t_len // GRID_W
    kh = min(KH_MAX, rows)
    c_len = k_ctx.shape[1]
    return pl.pallas_call(
        functools.partial(_nbr_attn_kernel, rows=rows, kh=kh),
        grid=(b_len, npair),
        in_specs=[pl.BlockSpec((t_len, V7X_LANES), lambda b, p: (b, p)),
                  pl.BlockSpec((t_len, V7X_LANES), lambda b, p: (b, npair + p)),
                  pl.BlockSpec((t_len, V7X_LANES), lambda b, p: (b, 2 * npair + p)),
                  pl.BlockSpec((1, c_len, V7X_LANES), lambda b, p: (b, 0, p)),
                  pl.BlockSpec((1, c_len, V7X_LANES), lambda b, p: (b, 0, p)),
                  pl.BlockSpec((2, kh, GRID_W, kh * GRID_W), lambda b, p: (p, 0, 0, 0))],
        out_specs=pl.BlockSpec((t_len, V7X_LANES), lambda b, p: (b, p)),
        out_shape=jax.ShapeDtypeStruct((n, D_BRANCH), BF16),
        name="nbr_attn",
        compiler_params=_params("parallel", "parallel"),
    )(qkv, qkv, qkv, k_ctx, v_ctx, bias)


def _nbr_bias_table(rpb, rows):
    kh = min(KH_MAX, rows)
    w = np.arange(GRID_W)
    col_start = np.clip(w - KW // 2, 0, GRID_W - KW)
    c = np.arange(GRID_W)
    inside = (c[None, :] >= col_start[:, None]) & (c[None, :] < col_start[:, None] + KW)
    col_off = np.clip(c[None, :] - w[:, None] + (KW - 1), 0, 2 * KW - 2)
    row_off = (KH_MAX - 1) - np.arange(kh)[:, None] + np.arange(kh)[None, :]
    tab = rpb[:, row_off[:, None, :, None], col_off[None, :, None, :]]
    tab = jnp.where(inside[None, None, :, None, :], tab, NEG_BIG)
    return tab.reshape(rpb.shape[0], kh, GRID_W, kh * GRID_W)


def _top2_route(logits, n_exp):
    lane = lax.broadcasted_iota(jnp.int32, logits.shape, 1)
    lane_f = lane.astype(F32)
    lg = jnp.where(lane < n_exp, logits, -jnp.inf)
    m1 = lg.max(axis=-1, keepdims=True)
    i1 = jnp.where(lg == m1, lane_f, float(V7X_LANES)).min(axis=-1, keepdims=True)
    lg2 = jnp.where(lane_f == i1, -jnp.inf, lg)
    m2 = lg2.max(axis=-1, keepdims=True)
    i2 = jnp.where(lg2 == m2, lane_f, float(V7X_LANES)).min(axis=-1, keepdims=True)
    e = jnp.exp(m2 - m1)
    g1 = 1.0 / (1.0 + e)
    g2 = e * g1
    out = jnp.where(lane == 0, i1, 0.0)
    out = jnp.where(lane == 1, i2, out)
    out = jnp.where(lane == 2, g1, out)
    return jnp.where(lane == 3, g2, out)


def _mix_kernel(ytf_ref, ytb_ref, bonus_ref, g_ref, yn_ref, gate_ref, x_ref, mod_ref,
                gng_ref, gnb_ref, bd_ref, wur_ref, wun_ref, wo_ref, lng_ref, lnb_ref, *rest,
                alpha, n_exp):
    if n_exp:
        router_ref, x1_o, h2_o, route_o = rest
    else:
        x1_o, h2_o = rest
    d = x_ref.shape[1]
    bd = bd_ref[...]
    y = (ytf_ref[0] + ytb_ref[0]).T
    inv_n = 1.0 / HEAD_DIM
    mu = _head_sum(y, bd) * inv_n
    yc = y - mu
    var = _head_sum(yc * yc, bd) * inv_n
    yr = yc * lax.rsqrt(var + GN_EPS) * gng_ref[...] + gnb_ref[...]
    yr = (yr + bonus_ref[...]) * g_ref[...]
    m = (gate_ref[:, 0:d] * _dot(yr.astype(BF16), wur_ref[...])
         + gate_ref[:, d:] * _dot(yn_ref[...], wun_ref[...]))
    o = _dot(m.astype(BF16), wo_ref[...])
    g1 = mod_ref[0, 2:3, :]
    sh2 = mod_ref[0, 3:4, :]
    sc2 = mod_ref[0, 4:5, :]
    x1 = _layer_norm(alpha * x_ref[...] + g1 * o, lng_ref[...], lnb_ref[...])
    x1_o[...] = x1
    h2 = x1 * (1.0 + sc2) + sh2
    h2_o[...] = h2.astype(BF16)
    if n_exp:
        logits = jnp.dot(h2, router_ref[...], preferred_element_type=F32, precision=HIGHEST)
        route_o[...] = _top2_route(logits, n_exp)


def _mix_out(ytf, ytb, bonus, g, yn, gates, x, modp, lw, t_len, alpha, router):
    n, d = x.shape
    dr = D_BRANCH
    per_seq = t_len // ROW_BLOCK
    row = lambda i: (i, 0)
    yt_spec = pl.BlockSpec((1, dr, ROW_BLOCK), lambda i: (i // per_seq, 0, i % per_seq))
    consts = [lw["gn_g"], lw["gn_b"], lw["bd"], lw["w_up_r"], lw["w_up_n"], lw["w_o"], lw["ln1_g"], lw["ln1_b"]]
    n_exp = 0
    out_specs = [pl.BlockSpec((ROW_BLOCK, d), row), pl.BlockSpec((ROW_BLOCK, d), row)]
    out_shape = [jax.ShapeDtypeStruct((n, d), F32), jax.ShapeDtypeStruct((n, d), BF16)]
    if router is not None:
        n_exp = router.shape[1]
        consts.append(jnp.pad(router, ((0, 0), (0, V7X_LANES - n_exp))))
        out_specs.append(pl.BlockSpec((ROW_BLOCK, V7X_LANES), row))
        out_shape.append(jax.ShapeDtypeStruct((n, V7X_LANES), F32))
    return pl.pallas_call(
        functools.partial(_mix_kernel, alpha=alpha, n_exp=n_exp),
        grid=(n // ROW_BLOCK,),
        in_specs=[yt_spec, yt_spec,
                  pl.BlockSpec((ROW_BLOCK, dr), row), pl.BlockSpec((ROW_BLOCK, dr), row),
                  pl.BlockSpec((ROW_BLOCK, dr), row), pl.BlockSpec((ROW_BLOCK, 2 * d), row),
                  pl.BlockSpec((ROW_BLOCK, d), row),
                  pl.BlockSpec((1, 6, d), lambda i: (i // per_seq, 0, 0))]
                 + [_const_spec(c.shape) for c in consts],
        out_specs=out_specs,
        out_shape=out_shape,
        name="mix_out",
        compiler_params=_params("parallel"),
    )(ytf, ytb, bonus, g, yn, gates, x, modp, *consts)


def _swiglu(h, w1, w3, w2):
    u = _dot(h, w1)
    act = u * _sigmoid(u) * _dot(h, w3)
    return _dot(act.astype(BF16), w2)


def _ffn_kernel(h_ref, x1_ref, mod_ref, w1_ref, w3_ref, w2_ref, lng_ref, lnb_ref, o_ref, *, alpha):
    f = _swiglu(h_ref[...], w1_ref[...], w3_ref[...], w2_ref[...])
    g2 = mod_ref[0, 5:6, :]
    o_ref[...] = _layer_norm(alpha * x1_ref[...] + g2 * f, lng_ref[...], lnb_ref[...])


def _ffn_dense(h2, x1, modp, w1, w3, w2, ln_g, ln_b, t_len, alpha):
    n, d = x1.shape
    per_seq = t_len // ROW_BLOCK
    row = lambda i: (i, 0)
    consts = [w1, w3, w2, ln_g, ln_b]
    return pl.pallas_call(
        functools.partial(_ffn_kernel, alpha=alpha),
        grid=(n // ROW_BLOCK,),
        in_specs=[pl.BlockSpec((ROW_BLOCK, d), row), pl.BlockSpec((ROW_BLOCK, d), row),
                  pl.BlockSpec((1, 6, d), lambda i: (i // per_seq, 0, 0))]
                 + [_const_spec(c.shape) for c in consts],
        out_specs=pl.BlockSpec((ROW_BLOCK, d), row),
        out_shape=jax.ShapeDtypeStruct((n, d), F32),
        name="ffn_dense",
        compiler_params=_params("parallel"),
    )(h2, x1, modp, *consts)


def _moe_kernel(be_ref, nb_ref, x_ref, w1_ref, w3_ref, w2_ref, o_ref):
    i = pl.program_id(0)

    @pl.when(i < nb_ref[0])
    def _():
        o_ref[...] = _swiglu(x_ref[...], w1_ref[0], w3_ref[0], w2_ref[0])

    @pl.when(i >= nb_ref[0])
    def _():
        o_ref[...] = jnp.zeros(o_ref.shape, F32)


def _moe_experts(block_e, n_used, xb, w1, w3, w2):
    p, d = xb.shape
    _, _, dff = w1.shape
    nblk = p // MOE_ROWS
    one = pl.Buffered(1)
    grid_spec = pltpu.PrefetchScalarGridSpec(
        num_scalar_prefetch=2,
        grid=(nblk,),
        in_specs=[pl.BlockSpec((MOE_ROWS, d), lambda i, be, nb: (i, 0)),
                  pl.BlockSpec((1, d, dff), lambda i, be, nb: (be[i], 0, 0), pipeline_mode=one),
                  pl.BlockSpec((1, d, dff), lambda i, be, nb: (be[i], 0, 0), pipeline_mode=one),
                  pl.BlockSpec((1, dff, d), lambda i, be, nb: (be[i], 0, 0), pipeline_mode=one)],
        out_specs=pl.BlockSpec((MOE_ROWS, d), lambda i, be, nb: (i, 0)),
    )
    return pl.pallas_call(
        _moe_kernel,
        grid_spec=grid_spec,
        out_shape=jax.ShapeDtypeStruct((p, d), F32),
        name="moe_experts",
        compiler_params=_params("arbitrary"),
    )(block_e, n_used, xb, w1, w3, w2)


def _combine_kernel(ya_ref, yb_ref, route_ref, x1_ref, mod_ref, lng_ref, lnb_ref, o_ref, *, alpha):
    f = route_ref[:, 2:3] * ya_ref[...] + route_ref[:, 3:4] * yb_ref[...]
    g2 = mod_ref[0, 5:6, :]
    o_ref[...] = _layer_norm(alpha * x1_ref[...] + g2 * f, lng_ref[...], lnb_ref[...])


def _moe_combine(ya, yb, route, x1, modp, ln_g, ln_b, t_len, alpha):
    n, d = x1.shape
    per_seq = t_len // ROW_BLOCK
    row = lambda i: (i, 0)
    return pl.pallas_call(
        functools.partial(_combine_kernel, alpha=alpha),
        grid=(n // ROW_BLOCK,),
        in_specs=[pl.BlockSpec((ROW_BLOCK, d), row), pl.BlockSpec((ROW_BLOCK, d), row),
                  pl.BlockSpec((ROW_BLOCK, V7X_LANES), row), pl.BlockSpec((ROW_BLOCK, d), row),
                  pl.BlockSpec((1, 6, d), lambda i: (i // per_seq, 0, 0)),
                  _const_spec(ln_g.shape), _const_spec(ln_b.shape)],
        out_specs=pl.BlockSpec((ROW_BLOCK, d), row),
        out_shape=jax.ShapeDtypeStruct((n, d), F32),
        name="moe_combine",
        compiler_params=_params("parallel"),
    )(ya, yb, route, x1, modp, ln_g, ln_b)


def _moe(h2, route, x1, modp, w1, w3, w2, ln_g, ln_b, t_len, alpha):
    n, d = x1.shape
    n_exp = w1.shape[0]
    top_e = route[:, 0:2].astype(jnp.int32)
    flat_e = top_e.reshape(-1)
    onehot = (flat_e[:, None] == jnp.arange(n_exp, dtype=jnp.int32)[None, :]).astype(jnp.int32)
    rank = jnp.sum((jnp.cumsum(onehot, axis=0) - onehot) * onehot, axis=1)
    counts = jnp.sum(onehot, axis=0)
    padded = (counts + MOE_ROWS - 1) // MOE_ROWS * MOE_ROWS
    pad_end = jnp.cumsum(padded)
    pad_start = pad_end - padded
    dest = pad_start[flat_e] + rank
    nblk = -(-(2 * n + n_exp * (MOE_ROWS - 1)) // MOE_ROWS)
    p = nblk * MOE_ROWS
    flat_tok = jnp.repeat(jnp.arange(n, dtype=jnp.int32), 2)
    buf_tok = jnp.full((p,), n, jnp.int32).at[dest].set(flat_tok)
    block_e = jnp.minimum(jnp.searchsorted(pad_end, jnp.arange(nblk, dtype=jnp.int32) * MOE_ROWS, side="right"),
                          n_exp - 1).astype(jnp.int32)
    n_used = (pad_end[-1] // MOE_ROWS).astype(jnp.int32).reshape(1)
    xb = jnp.concatenate([h2, jnp.zeros((1, d), h2.dtype)], axis=0)[buf_tok]
    yb = _moe_experts(block_e, n_used, xb, w1, w3, w2)
    dest2 = dest.reshape(n, 2)
    return _moe_combine(yb[dest2[:, 0]], yb[dest2[:, 1]], route, x1, modp, ln_g, ln_b, t_len, alpha)


def _layer(x, modp, lw, b_len, t_len, alpha, s0, ctx_kv):
    slab, qkv, gates = _in_proj(x, modp, lw["w_in"], t_len, lw["n_slab"], 3 * D_BRANCH)
    r, kk, wf, wb, bf, bb, ktf, ktb, vt, bonus, g = _rwkv_prep(slab, lw, b_len, t_len)
    ytf, ytb, s_fin = _rwkv_scan(s0, r, kk, wf, wb, bf, bb, ktf, ktb, vt, b_len, t_len)
    if ctx_kv is None:
        yn = _ctx_attention(qkv, b_len, t_len)
    else:
        yn = _nbr_attention(qkv, ctx_kv[0], ctx_kv[1], lw["bias"], b_len, t_len)
    outs = _mix_out(ytf, ytb, bonus, g, yn, gates, x, modp, lw, t_len, alpha, lw.get("router"))
    if "router" in lw:
        x1, h2, route = outs
        x2 = _moe(h2, route, x1, modp, lw["moe_w1"], lw["moe_w3"], lw["moe_w2"], lw["ln2_g"], lw["ln2_b"],
                  t_len, alpha)
    else:
        x1, h2 = outs
        x2 = _ffn_dense(h2, x1, modp, lw["ffn_w1"], lw["ffn_w3"], lw["ffn_w2"], lw["ln2_g"], lw["ln2_b"],
                        t_len, alpha)
    return x2, s_fin, qkv


def kernel(x_prompt, x_sample, c, state_rwkv, cache_k, cache_v, c_ctx, w_ada, b_ada, w_in, ts_w, rw_w0, rw_w2, rw_a0, rw_a2, rw_g2, rw_kk, rw_ka, rw_u, rw_gn_g, rw_gn_b, na_rpb, w_up_r, w_up_n, w_o, ln1_g, ln1_b, ln2_g, ln2_b, ffn_w1, ffn_w3, ffn_w2, moe_router, moe_w1, moe_w3, moe_w2):
    depth = w_in.shape[0]
    bc, tc_len, d = x_prompt.shape
    bs, ts_len, _ = x_sample.shape
    dr = D_BRANCH
    alpha = float((2 * depth) ** 0.25)
    n_slab = ts_w.shape[2]
    lora_w = rw_w2.shape[2]
    lora_a = rw_a2.shape[2]

    cvec = jnp.zeros((8, d), F32).at[0].set(c_ctx).at[1:1 + bs].set(c)
    mod = _ada_mod(cvec, w_ada, b_ada)

    half = np.kron(np.eye(2, dtype=np.float32), np.ones((HEAD_DIM, HEAD_DIM), np.float32))
    bd = jnp.asarray(half, BF16)
    rows = ts_len // GRID_W

    xp = x_prompt.reshape(bc * tc_len, d)
    xs = x_sample.reshape(bs * ts_len, d)
    zero_state = jnp.zeros((bc, 2, N_HEADS, HEAD_DIM, HEAD_DIM), F32)
    new_s, new_k, new_v = [], [], []
    for l in range(depth):
        w2blk = jnp.zeros((2 * lora_w, 2 * dr), F32).at[:lora_w, :dr].set(rw_w2[l, 0]).at[lora_w:, dr:].set(rw_w2[l, 1])
        a2blk = jnp.zeros((2 * lora_a, 2 * dr), F32).at[:lora_a, :dr].set(rw_a2[l, 0]).at[lora_a:, dr:].set(rw_a2[l, 1])
        lw = dict(
            n_slab=n_slab,
            w_in=w_in[l].astype(BF16), ts=ts_w[l],
            w0=rw_w0[l].reshape(1, 2 * dr), w2=w2blk.astype(BF16),
            a0=rw_a0[l].reshape(1, 2 * dr), a2=a2blk.astype(BF16),
            g2=rw_g2[l].astype(BF16), kkw=rw_kk[l].reshape(1, dr), ka=rw_ka[l].reshape(1, dr),
            u=rw_u[l].reshape(1, dr), bd=bd,
            gn_g=rw_gn_g[l].reshape(1, dr), gn_b=rw_gn_b[l].reshape(1, dr),
            w_up_r=w_up_r[l].astype(BF16), w_up_n=w_up_n[l].astype(BF16), w_o=w_o[l].astype(BF16),
            ln1_g=ln1_g[l].reshape(1, d), ln1_b=ln1_b[l].reshape(1, d),
            ln2_g=ln2_g[l].reshape(1, d), ln2_b=ln2_b[l].reshape(1, d),
            bias=_nbr_bias_table(na_rpb[l], rows),
        )
        if l % 2 == 0:
            lw.update(ffn_w1=ffn_w1[l // 2].astype(BF16), ffn_w3=ffn_w3[l // 2].astype(BF16),
                      ffn_w2=ffn_w2[l // 2].astype(BF16))
        else:
            lw.update(router=moe_router[l // 2], moe_w1=moe_w1[l // 2].astype(BF16),
                      moe_w3=moe_w3[l // 2].astype(BF16), moe_w2=moe_w2[l // 2].astype(BF16))

        mod_ctx = jnp.broadcast_to(mod[l, 0].reshape(1, 6, d), (bc, 6, d))
        mod_lat = mod[l, 1:1 + bs].reshape(bs, 6, d)

        xp, s_fin, qkv_c = _layer(xp, mod_ctx, lw, bc, tc_len, alpha, zero_state, None)
        new_s.append(s_fin)
        new_k.append(qkv_c[:, dr:2 * dr].reshape(bc, tc_len, N_HEADS, HEAD_DIM))
        new_v.append(qkv_c[:, 2 * dr:3 * dr].reshape(bc, tc_len, N_HEADS, HEAD_DIM))

        kv = (cache_k[:, l].reshape(bs, -1, dr), cache_v[:, l].reshape(bs, -1, dr))
        xs, _, _ = _layer(xs, mod_lat, lw, bs, ts_len, alpha, state_rwkv[:, l], kv)

    return (xp.reshape(bc, tc_len, d), xs.reshape(bs, ts_len, d),
            jnp.stack(new_s, axis=1), jnp.stack(new_k, axis=1), jnp.stack(new_v, axis=1))
```

```python
import functools

import numpy as np
import jax
import jax.numpy as jnp
from jax import lax
from jax.experimental import pallas as pl
from jax.experimental.pallas import tpu as pltpu

F32 = jnp.float32
BF16 = jnp.bfloat16
HIGHEST = lax.Precision.HIGHEST

N_HEADS = 8
HEAD_DIM = 64
D_BRANCH = N_HEADS * HEAD_DIM
V7X_LANES = 128
ROW_BLOCK = 256
SCAN_CHUNK = 64
MOE_ROWS = 256
KH_MAX = 8
KW = 16
GRID_W = 64
LN_EPS = 1e-5
GN_EPS = 64e-5
NEG_BIG = -1e30
VMEM_LIMIT = 56 * 1024 * 1024


def _params(*sem):
    return pltpu.CompilerParams(dimension_semantics=sem, vmem_limit_bytes=VMEM_LIMIT)


def _const_spec(shape):
    nd = len(shape)
    return pl.BlockSpec(shape, lambda *_: (0,) * nd, pipeline_mode=pl.Buffered(1))


def _dot(a, b):
    return jnp.dot(a, b, preferred_element_type=F32)


def _dot_nt(a, b):
    return lax.dot_general(a, b, (((1,), (1,)), ((), ())), preferred_element_type=F32)


def _dot_tn(a, b):
    return lax.dot_general(a, b, (((0,), (0,)), ((), ())), preferred_element_type=F32)


def _split2(x):
    hi = x.astype(BF16)
    return hi, (x - hi.astype(F32)).astype(BF16)


def _mm3(a, b, f=_dot):
    ah, al = _split2(a)
    bh, bl = _split2(b)
    return f(ah, bh) + (f(ah, bl) + f(al, bh))


def _mm_exact_lhs(m, x):
    hi = x.astype(BF16)
    r1 = x - hi.astype(F32)
    mid = r1.astype(BF16)
    lo = (r1 - mid.astype(F32)).astype(BF16)
    return _dot(m, hi) + (_dot(m, mid) + _dot(m, lo))


def _sigmoid(x):
    return 1.0 / (1.0 + jnp.exp(-x))


def _head_sum(x, bd):
    outs = []
    for j in range(x.shape[1] // V7X_LANES):
        xs = x[:, j * V7X_LANES:(j + 1) * V7X_LANES]
        hi = xs.astype(BF16)
        lo = (xs - hi.astype(F32)).astype(BF16)
        outs.append(_dot(hi, bd) + _dot(lo, bd))
    return jnp.concatenate(outs, axis=1)


def _layer_norm(z, g, b):
    mu = jnp.mean(z, axis=-1, keepdims=True)
    zc = z - mu
    var = jnp.mean(zc * zc, axis=-1, keepdims=True)
    return zc * lax.rsqrt(var + LN_EPS) * g + b


def _ada_kernel(c_ref, w_ref, b_ref, o_ref):
    cs = c_ref[...]
    s = cs * _sigmoid(cs)
    o_ref[0] = jnp.dot(s, w_ref[0], preferred_element_type=F32, precision=HIGHEST) + b_ref[0]


def _ada_mod(cvec, w_ada, b_ada):
    depth, d, n6 = w_ada.shape
    tn = n6 // 4
    return pl.pallas_call(
        _ada_kernel,
        grid=(depth, n6 // tn),
        in_specs=[pl.BlockSpec((8, d), lambda l, j: (0, 0)),
                  pl.BlockSpec((1, d, tn), lambda l, j: (l, 0, j)),
                  pl.BlockSpec((1, 1, tn), lambda l, j: (l, 0, j))],
        out_specs=pl.BlockSpec((1, 8, tn), lambda l, j: (l, 0, j)),
        out_shape=jax.ShapeDtypeStruct((depth, 8, n6), F32),
        name="ada_mod",
        compiler_params=_params("parallel", "parallel"),
    )(cvec, w_ada, b_ada.reshape(depth, 1, n6))


def _inproj_kernel(x_ref, mod_ref, w_ref, slab_ref, qkv_ref, gate_ref, *, n_slab, n_qkv):
    sh = mod_ref[0, 0:1, :]
    sc = mod_ref[0, 1:2, :]
    h = (x_ref[...] * (1.0 + sc) + sh).astype(BF16)
    slab_ref[...] = _dot(h, w_ref[:, 0:n_slab])
    qkv_ref[...] = _dot(h, w_ref[:, n_slab:n_slab + n_qkv])
    gate_ref[...] = _sigmoid(_dot(h, w_ref[:, n_slab + n_qkv:]))


def _in_proj(x, modp, w_in, t_len, n_slab, n_qkv):
    n, d = x.shape
    n_cols = w_in.shape[1]
    n_gate = n_cols - n_slab - n_qkv
    per_seq = t_len // ROW_BLOCK
    return pl.pallas_call(
        functools.partial(_inproj_kernel, n_slab=n_slab, n_qkv=n_qkv),
        grid=(n // ROW_BLOCK,),
        in_specs=[pl.BlockSpec((ROW_BLOCK, d), lambda i: (i, 0)),
                  pl.BlockSpec((1, 6, d), lambda i: (i // per_seq, 0, 0)),
                  _const_spec((d, n_cols))],
        out_specs=[pl.BlockSpec((ROW_BLOCK, n_slab), lambda i: (i, 0)),
                   pl.BlockSpec((ROW_BLOCK, n_qkv), lambda i: (i, 0)),
                   pl.BlockSpec((ROW_BLOCK, n_gate), lambda i: (i, 0))],
        out_shape=[jax.ShapeDtypeStruct((n, n_slab), F32),
                   jax.ShapeDtypeStruct((n, n_qkv), F32),
                   jax.ShapeDtypeStruct((n, n_gate), F32)],
        name="in_proj",
        compiler_params=_params("parallel"),
    )(x, modp, w_in)


def _store_head_major(o_ref, x):
    zeros = jnp.zeros((x.shape[0], V7X_LANES - HEAD_DIM), F32)
    for h in range(N_HEADS):
        o_ref[h, :, 0:HEAD_DIM] = x[:, h * HEAD_DIM:(h + 1) * HEAD_DIM]
        o_ref[h, :, HEAD_DIM:] = zeros


def _prep_kernel(cur_ref, prev_ref, next_ref, ts_ref, w0_ref, w2_ref, a0_ref, a2_ref, g2_ref,
                 kkw_ref, ka_ref, u_ref, bd_ref,
                 r_o, kk_o, lwf_o, lwb_o, bf_o, bb_o, ktf_o, ktb_o, v_o, bonus_o, g_o,
                 *, per_seq, lora_w, lora_a):
    i = pl.program_id(0)
    tb, n_slab = cur_ref.shape
    dr = D_BRANCH
    cur = cur_ref[...]
    first = (i % per_seq) == 0
    last = (i % per_seq) == per_seq - 1
    prow = jnp.where(first, 0.0, prev_ref[7:8, :])
    nrow = jnp.where(last, 0.0, next_ref[0:1, :])
    rid = lax.broadcasted_iota(jnp.int32, (tb, n_slab), 0)
    xp = jnp.where(rid == 0, prow, pltpu.roll(cur, 1, 0))
    xn = jnp.where(rid == tb - 1, nrow, pltpu.roll(cur, tb - 1, 0))
    s = ts_ref[0:1, :] * xp + ts_ref[1:2, :] * cur + ts_ref[2:3, :] * xn

    r = s[:, 0:dr]
    k = s[:, dr:2 * dr]
    v = s[:, 2 * dr:3 * dr]
    o = 3 * dr
    wd = s[:, o:o + 2 * lora_w]
    ad = s[:, o + 2 * lora_w:o + 2 * lora_w + 2 * lora_a]
    gd = s[:, o + 2 * lora_w + 2 * lora_a:]

    wl = w0_ref[...] + _dot(jnp.tanh(wd).astype(BF16), w2_ref[...])
    logw = -float(np.exp(-0.5)) * _sigmoid(wl)
    a = _sigmoid(a0_ref[...] + _dot(ad.astype(BF16), a2_ref[...]))
    g = _dot(_sigmoid(gd).astype(BF16), g2_ref[...])

    bd = bd_ref[...]
    kk = k * kkw_ref[...]
    kk = kk * lax.rsqrt(_head_sum(kk * kk, bd) + 1e-12)
    ka = ka_ref[...]
    a_f = a[:, :dr]
    a_b = a[:, dr:]
    kt_f = k * (1.0 + (a_f - 1.0) * ka)
    kt_b = k * (1.0 + (a_b - 1.0) * ka)
    bonus = _head_sum(r * (0.5 * (kt_f + kt_b)) * u_ref[...], bd) * v

    _store_head_major(r_o, r)
    _store_head_major(kk_o, kk)
    _store_head_major(lwf_o, logw[:, :dr])
    _store_head_major(lwb_o, logw[:, dr:])
    _store_head_major(bf_o, kk * a_f)
    _store_head_major(bb_o, kk * a_b)
    _store_head_major(ktf_o, kt_f)
    _store_head_major(ktb_o, kt_b)
    _store_head_major(v_o, v)
    bonus_o[...] = bonus
    g_o[...] = g


def _rwkv_prep(slab, lw, b_len, t_len):
    n, n_slab = slab.shape
    dr = D_BRANCH
    per_seq = t_len // ROW_BLOCK
    sub = ROW_BLOCK // 8
    lora_w = lw["w2"].shape[0] // 2
    lora_a = lw["a2"].shape[0] // 2
    row = lambda i: (i, 0)
    hm = pl.BlockSpec((N_HEADS, ROW_BLOCK, V7X_LANES), lambda i: (0, i, 0))
    consts = [lw["ts"], lw["w0"], lw["w2"], lw["a0"], lw["a2"], lw["g2"], lw["kkw"], lw["ka"], lw["u"], lw["bd"]]
    outs = pl.pallas_call(
        functools.partial(_prep_kernel, per_seq=per_seq, lora_w=lora_w, lora_a=lora_a),
        grid=(n // ROW_BLOCK,),
        in_specs=[pl.BlockSpec((ROW_BLOCK, n_slab), row),
                  pl.BlockSpec((8, n_slab), lambda i: (jnp.maximum(i * sub - 1, 0), 0)),
                  pl.BlockSpec((8, n_slab), lambda i: (jnp.minimum((i + 1) * sub, n // 8 - 1), 0))]
                 + [_const_spec(c.shape) for c in consts],
        out_specs=[hm] * 9 + [pl.BlockSpec((ROW_BLOCK, dr), row), pl.BlockSpec((ROW_BLOCK, dr), row)],
        out_shape=[jax.ShapeDtypeStruct((N_HEADS, n, V7X_LANES), F32)] * 9
                  + [jax.ShapeDtypeStruct((n, dr), F32), jax.ShapeDtypeStruct((n, dr), F32)],
        name="rwkv_prep",
        compiler_params=_params("parallel"),
    )(slab, slab, slab, *consts)
    return outs


def _tri_masks(rev):
    c = SCAN_CHUNK
    ti = lax.broadcasted_iota(jnp.int32, (c, c), 0)
    ii = lax.broadcasted_iota(jnp.int32, (c, c), 1)
    if rev:
        ti, ii = c - 1 - ti, c - 1 - ii
    one = lambda m: jnp.where(m, 1.0, 0.0)
    levels = []
    s = 1
    while s < c:
        same = (ti // (2 * s)) == (ii // (2 * s))
        levels.append(one(same & ((ti % (2 * s)) >= s) & ((ii % (2 * s)) < s)))
        s *= 2
    return one(ii <= ti), one(ii < ti), levels


def _scan_kernel(z0_ref, rf, kkf, lwf, bf, ktf, vf, rb, kkb, lwb, bb, ktb, vb, yf_o, yb_o, zfin_o, z_scr):
    j = pl.program_id(1)
    nj = pl.num_programs(1)
    c = SCAN_CHUNK
    nu = 2 * N_HEADS

    @pl.when(j == 0)
    def _():
        z_scr[...] = jnp.zeros(z_scr.shape, F32)
        z_scr[:, :, 0:HEAD_DIM] = z0_ref[0].reshape(nu, HEAD_DIM, HEAD_DIM)

    eye = jnp.where(lax.broadcasted_iota(jnp.int32, (c, c), 0) == lax.broadcasted_iota(jnp.int32, (c, c), 1),
                    1.0, 0.0)
    eye_kp = jnp.where(lax.broadcasted_iota(jnp.int32, (HEAD_DIM, V7X_LANES), 0)
                       == lax.broadcasted_iota(jnp.int32, (HEAD_DIM, V7X_LANES), 1), 1.0, 0.0)
    refs = ((rf, kkf, lwf, bf, ktf, vf, yf_o), (rb, kkb, lwb, bb, ktb, vb, yb_o))
    masks = (_tri_masks(False), _tri_masks(True))
    cum_lhs = [jnp.concatenate([m[0].astype(BF16), jnp.ones((c, c), BF16)], axis=0) for m in masks]
    units = [(d, h) for d in range(2) for h in range(N_HEADS)]

    def per(fn):
        return [fn(i, d, h) for i, (d, h) in enumerate(units)]

    lw = per(lambda i, d, h: refs[d][2][h])
    cs = per(lambda i, d, h: _mm_exact_lhs(cum_lhs[d], lw[i]))
    cum = [x[0:c] for x in cs]
    tot = [x[c:] for x in cs]
    r_s = per(lambda i, d, h: refs[d][0][h] * jnp.exp(cum[i]))
    k_s = per(lambda i, d, h: refs[d][1][h] * jnp.exp(cum[i] - lw[i]))
    e_neg = per(lambda i, d, h: jnp.exp(-cum[i]))
    e_end = per(lambda i, d, h: jnp.exp(tot[i] - cum[i]))
    b = per(lambda i, d, h: refs[d][3][h])
    kt = per(lambda i, d, h: refs[d][4][h])
    v = per(lambda i, d, h: refs[d][5][h])
    kr = per(lambda i, d, h: jnp.concatenate([k_s[i], r_s[i]], axis=0))
    ab = per(lambda i, d, h: _mm3(kr[i], b[i] * e_neg[i], _dot_nt))
    ak = per(lambda i, d, h: _mm3(kr[i], kt[i] * e_neg[i], _dot_nt))
    a_b = per(lambda i, d, h: ab[i][0:c] * masks[d][1])
    b_b = per(lambda i, d, h: ab[i][c:] * masks[d][0])
    a_kt = per(lambda i, d, h: ak[i][0:c] * masks[d][1])
    b_kt = per(lambda i, d, h: ak[i][c:] * masks[d][0])
    dm = per(lambda i, d, h: eye - a_b[i] * masks[d][2][0])
    for lvl in range(1, len(masks[0][2])):
        zd = per(lambda i, d, h: _mm3(a_b[i] * masks[d][2][lvl], dm[i]))
        dm = per(lambda i, d, h: dm[i] - _mm3(dm[i], zd[i]))
    w1 = per(lambda i, d, h: _mm3(a_kt[i], v[i]))
    g = per(lambda i, d, h: _mm3(dm[i], jnp.concatenate([k_s[i], w1[i]], axis=1)))
    bg = per(lambda i, d, h: _mm3(b_b[i], g[i]))
    q = per(lambda i, d, h: r_s[i] - bg[i][:, 0:V7X_LANES])
    y0 = per(lambda i, d, h: _mm3(b_kt[i], v[i]) - bg[i][:, V7X_LANES:])
    cg = per(lambda i, d, h: _mm3((b[i] * e_end[i])[:, 0:HEAD_DIM], g[i], _dot_tn))
    phi = per(lambda i, d, h: eye_kp * jnp.exp(tot[i][0:HEAD_DIM]) - cg[i][:, 0:V7X_LANES])
    psi = per(lambda i, d, h: _mm3((kt[i] * e_end[i])[:, 0:HEAD_DIM], v[i], _dot_tn) - cg[i][:, V7X_LANES:])
    z = per(lambda i, d, h: z_scr[i])
    y = per(lambda i, d, h: _mm3(q[i][:, 0:HEAD_DIM], z[i]) + y0[i])
    zn = per(lambda i, d, h: _mm3(phi[i][:, 0:HEAD_DIM], z[i]) + psi[i])
    for i, (d, h) in enumerate(units):
        refs[d][6][h] = y[i]
        z_scr[i] = zn[i]

    @pl.when(j == nj - 1)
    def _():
        zfin_o[0] = z_scr[:, :, 0:HEAD_DIM].reshape(2, N_HEADS, HEAD_DIM, HEAD_DIM)


def _rwkv_scan(z0, r, kk, lwf, lwb, bf, bb, ktf, ktb, v, b_len, t_len):
    c = SCAN_CHUNK
    nj = t_len // c
    n = b_len * t_len
    fwd = pl.BlockSpec((N_HEADS, c, V7X_LANES), lambda b, j: (0, b * nj + j, 0))
    bwd = pl.BlockSpec((N_HEADS, c, V7X_LANES), lambda b, j: (0, b * nj + nj - 1 - j, 0))
    st = pl.BlockSpec((1, 2, N_HEADS, HEAD_DIM, HEAD_DIM), lambda b, j: (b, 0, 0, 0, 0))
    return pl.pallas_call(
        _scan_kernel,
        grid=(b_len, nj),
        in_specs=[st] + [fwd] * 6 + [bwd] * 6,
        out_specs=[fwd, bwd, st],
        out_shape=[jax.ShapeDtypeStruct((N_HEADS, n, V7X_LANES), F32),
                   jax.ShapeDtypeStruct((N_HEADS, n, V7X_LANES), F32),
                   jax.ShapeDtypeStruct((b_len, 2, N_HEADS, HEAD_DIM, HEAD_DIM), F32)],
        scratch_shapes=[pltpu.VMEM((2 * N_HEADS, HEAD_DIM, V7X_LANES), F32)],
        name="rwkv_scan",
        compiler_params=_params("parallel", "arbitrary"),
    )(z0, r, kk, lwf, bf, ktf, v, r, kk, lwb, bb, ktb, v)


def _softmax_rows(parts):
    m = parts[0].max(axis=-1, keepdims=True)
    for p in parts[1:]:
        m = jnp.maximum(m, p.max(axis=-1, keepdims=True))
    es = [jnp.exp(p - m) for p in parts]
    den = es[0].sum(axis=-1, keepdims=True)
    for e in es[1:]:
        den = den + e.sum(axis=-1, keepdims=True)
    inv = 1.0 / den
    return [e * inv for e in es]


def _ctx_attn_kernel(q_ref, k_ref, v_ref, o_ref):
    scale = HEAD_DIM ** -0.5
    outs = []
    for hh in range(2):
        cols = slice(hh * HEAD_DIM, (hh + 1) * HEAD_DIM)
        q = q_ref[:, cols].astype(BF16)
        k = k_ref[:, cols].astype(BF16)
        v = v_ref[:, cols].astype(BF16)
        (p,) = _softmax_rows([_dot_nt(q, k) * scale])
        outs.append(_dot(p.astype(BF16), v))
    o_ref[...] = jnp.concatenate(outs, axis=1).astype(o_ref.dtype)


def _ctx_attention(qkv, b_len, t_len):
    n = qkv.shape[0]
    npair = N_HEADS // 2
    return pl.pallas_call(
        _ctx_attn_kernel,
        grid=(b_len, npair),
        in_specs=[pl.BlockSpec((t_len, V7X_LANES), lambda b, p: (b, p)),
                  pl.BlockSpec((t_len, V7X_LANES), lambda b, p: (b, npair + p)),
                  pl.BlockSpec((t_len, V7X_LANES), lambda b, p: (b, 2 * npair + p))],
        out_specs=pl.BlockSpec((t_len, V7X_LANES), lambda b, p: (b, p)),
        out_shape=jax.ShapeDtypeStruct((n, D_BRANCH), BF16),
        name="ctx_attn",
        compiler_params=_params("parallel", "parallel"),
    )(qkv, qkv, qkv)


def _nbr_attn_kernel(q_ref, k_ref, v_ref, kc_ref, vc_ref, bias_ref, o_ref, *, rows, kh):
    scale = HEAD_DIM ** -0.5
    win = kh * GRID_W

    def body(i, carry):
        rs = jnp.clip(i - kh // 2, 0, rows - kh)
        dd = i - rs
        q0 = pl.multiple_of(i * GRID_W, GRID_W)
        k0 = pl.multiple_of(rs * GRID_W, GRID_W)
        qb = q_ref[pl.ds(q0, GRID_W), :]
        kw = k_ref[pl.ds(k0, win), :]
        vw = v_ref[pl.ds(k0, win), :]
        outs = []
        for hh in range(2):
            cols = slice(hh * HEAD_DIM, (hh + 1) * HEAD_DIM)
            q = qb[:, cols].astype(BF16)
            s_loc = _dot_nt(q, kw[:, cols].astype(BF16)) * scale + bias_ref[hh, dd]
            s_ctx = _dot_nt(q, kc_ref[0, :, cols].astype(BF16)) * scale
            p_loc, p_ctx = _softmax_rows([s_loc, s_ctx])
            outs.append(_dot(p_loc.astype(BF16), vw[:, cols].astype(BF16))
                        + _dot(p_ctx.astype(BF16), vc_ref[0, :, cols].astype(BF16)))
        o_ref[pl.ds(q0, GRID_W), :] = jnp.concatenate(outs, axis=1).astype(o_ref.dtype)
        return carry

    lax.fori_loop(0, rows, body, 0)


def _nbr_attention(qkv, k_ctx, v_ctx, bias, b_len, t_len):
    n = qkv.shape[0]
    npair = N_HEADS // 2
    rows = t_len // GRID_W
    kh = min(KH_MAX, rows)
    c_len = k_ctx.shape[1]
    return pl.pallas_call(
        functools.partial(_nbr_attn_kernel, rows=rows, kh=kh),
        grid=(b_len, npair),
        in_specs=[pl.BlockSpec((t_len, V7X_LANES), lambda b, p: (b, p)),
                  pl.BlockSpec((t_len, V7X_LANES), lambda b, p: (b, npair + p)),
                  pl.BlockSpec((t_len, V7X_LANES), lambda b, p: (b, 2 * npair + p)),
                  pl.BlockSpec((1, c_len, V7X_LANES), lambda b, p: (b, 0, p)),
                  pl.BlockSpec((1, c_len, V7X_LANES), lambda b, p: (b, 0, p)),
                  pl.BlockSpec((2, kh, GRID_W, kh * GRID_W), lambda b, p: (p, 0, 0, 0))],
        out_specs=pl.BlockSpec((t_len, V7X_LANES), lambda b, p: (b, p)),
        out_shape=jax.ShapeDtypeStruct((n, D_BRANCH), BF16),
        name="nbr_attn",
        compiler_params=_params("parallel", "parallel"),
    )(qkv, qkv, qkv, k_ctx, v_ctx, bias)


def _nbr_bias_table(rpb, rows):
    kh = min(KH_MAX, rows)
    w = np.arange(GRID_W)
    col_start = np.clip(w - KW // 2, 0, GRID_W - KW)
    c = np.arange(GRID_W)
    inside = (c[None, :] >= col_start[:, None]) & (c[None, :] < col_start[:, None] + KW)
    rp = jnp.pad(rpb, ((0, 0), (0, 0), (GRID_W - KW, GRID_W - KW)))
    toep = jnp.stack([rp[:, :, GRID_W - 1 - wi:2 * GRID_W - 1 - wi] for wi in range(GRID_W)], axis=2)
    toep = jnp.where(inside[None, None], toep, NEG_BIG)
    tab = jnp.stack([toep[:, KH_MAX - 1 - dd:KH_MAX - 1 - dd + kh] for dd in range(kh)], axis=1)
    tab = jnp.transpose(tab, (0, 1, 3, 2, 4))
    return tab.reshape(rpb.shape[0], kh, GRID_W, kh * GRID_W)


def _top2_route(logits, n_exp):
    lane = lax.broadcasted_iota(jnp.int32, logits.shape, 1)
    lane_f = lane.astype(F32)
    lg = jnp.where(lane < n_exp, logits, -jnp.inf)
    m1 = lg.max(axis=-1, keepdims=True)
    i1 = jnp.where(lg == m1, lane_f, float(V7X_LANES)).min(axis=-1, keepdims=True)
    lg2 = jnp.where(lane_f == i1, -jnp.inf, lg)
    m2 = lg2.max(axis=-1, keepdims=True)
    i2 = jnp.where(lg2 == m2, lane_f, float(V7X_LANES)).min(axis=-1, keepdims=True)
    e = jnp.exp(m2 - m1)
    g1 = 1.0 / (1.0 + e)
    g2 = e * g1
    out = jnp.where(lane == 0, i1, 0.0)
    out = jnp.where(lane == 1, i2, out)
    out = jnp.where(lane == 2, g1, out)
    return jnp.where(lane == 3, g2, out)


def _mix_kernel(yf_ref, yb_ref, bonus_ref, g_ref, yn_ref, gate_ref, x_ref, mod_ref,
                gng_ref, gnb_ref, bd_ref, wur_ref, wun_ref, wo_ref, lng_ref, lnb_ref, *rest,
                alpha, n_exp):
    if n_exp:
        router_ref, x1_o, h2_o, route_o = rest
    else:
        x1_o, h2_o = rest
    d = x_ref.shape[1]
    bd = bd_ref[...]
    y = jnp.concatenate([yf_ref[h, :, 0:HEAD_DIM] + yb_ref[h, :, 0:HEAD_DIM] for h in range(N_HEADS)],
                        axis=1)
    inv_n = 1.0 / HEAD_DIM
    mu = _head_sum(y, bd) * inv_n
    yc = y - mu
    var = _head_sum(yc * yc, bd) * inv_n
    yr = yc * lax.rsqrt(var + GN_EPS) * gng_ref[...] + gnb_ref[...]
    yr = (yr + bonus_ref[...]) * g_ref[...]
    m = (gate_ref[:, 0:d] * _dot(yr.astype(BF16), wur_ref[...])
         + gate_ref[:, d:] * _dot(yn_ref[...], wun_ref[...]))
    o = _dot(m.astype(BF16), wo_ref[...])
    g1 = mod_ref[0, 2:3, :]
    sh2 = mod_ref[0, 3:4, :]
    sc2 = mod_ref[0, 4:5, :]
    x1 = _layer_norm(alpha * x_ref[...] + g1 * o, lng_ref[...], lnb_ref[...])
    x1_o[...] = x1
    h2 = x1 * (1.0 + sc2) + sh2
    h2_o[...] = h2.astype(BF16)
    if n_exp:
        logits = jnp.dot(h2, router_ref[...], preferred_element_type=F32, precision=HIGHEST)
        route_o[...] = _top2_route(logits, n_exp)


def _mix_out(yf, yb, bonus, g, yn, gates, x, modp, lw, t_len, alpha, router):
    n, d = x.shape
    dr = D_BRANCH
    per_seq = t_len // ROW_BLOCK
    row = lambda i: (i, 0)
    yt_spec = pl.BlockSpec((N_HEADS, ROW_BLOCK, V7X_LANES), lambda i: (0, i, 0))
    consts = [lw["gn_g"], lw["gn_b"], lw["bd"], lw["w_up_r"], lw["w_up_n"], lw["w_o"], lw["ln1_g"], lw["ln1_b"]]
    n_exp = 0
    out_specs = [pl.BlockSpec((ROW_BLOCK, d), row), pl.BlockSpec((ROW_BLOCK, d), row)]
    out_shape = [jax.ShapeDtypeStruct((n, d), F32), jax.ShapeDtypeStruct((n, d), BF16)]
    if router is not None:
        n_exp = router.shape[1]
        consts.append(jnp.pad(router, ((0, 0), (0, V7X_LANES - n_exp))))
        out_specs.append(pl.BlockSpec((ROW_BLOCK, V7X_LANES), row))
        out_shape.append(jax.ShapeDtypeStruct((n, V7X_LANES), F32))
    return pl.pallas_call(
        functools.partial(_mix_kernel, alpha=alpha, n_exp=n_exp),
        grid=(n // ROW_BLOCK,),
        in_specs=[yt_spec, yt_spec,
                  pl.BlockSpec((ROW_BLOCK, dr), row), pl.BlockSpec((ROW_BLOCK, dr), row),
                  pl.BlockSpec((ROW_BLOCK, dr), row), pl.BlockSpec((ROW_BLOCK, 2 * d), row),
                  pl.BlockSpec((ROW_BLOCK, d), row),
                  pl.BlockSpec((1, 6, d), lambda i: (i // per_seq, 0, 0))]
                 + [_const_spec(c.shape) for c in consts],
        out_specs=out_specs,
        out_shape=out_shape,
        name="mix_out",
        compiler_params=_params("parallel"),
    )(yf, yb, bonus, g, yn, gates, x, modp, *consts)


def _swiglu(h, w1, w3, w2):
    u = _dot(h, w1)
    act = u * _sigmoid(u) * _dot(h, w3)
    return _dot(act.astype(BF16), w2)


def _ffn_kernel(h_ref, x1_ref, mod_ref, w1_ref, w3_ref, w2_ref, lng_ref, lnb_ref, o_ref, *, alpha):
    f = _swiglu(h_ref[...], w1_ref[...], w3_ref[...], w2_ref[...])
    g2 = mod_ref[0, 5:6, :]
    o_ref[...] = _layer_norm(alpha * x1_ref[...] + g2 * f, lng_ref[...], lnb_ref[...])


def _ffn_dense(h2, x1, modp, w1, w3, w2, ln_g, ln_b, t_len, alpha):
    n, d = x1.shape
    per_seq = t_len // ROW_BLOCK
    row = lambda i: (i, 0)
    consts = [w1, w3, w2, ln_g, ln_b]
    return pl.pallas_call(
        functools.partial(_ffn_kernel, alpha=alpha),
        grid=(n // ROW_BLOCK,),
        in_specs=[pl.BlockSpec((ROW_BLOCK, d), row), pl.BlockSpec((ROW_BLOCK, d), row),
                  pl.BlockSpec((1, 6, d), lambda i: (i // per_seq, 0, 0))]
                 + [_const_spec(c.shape) for c in consts],
        out_specs=pl.BlockSpec((ROW_BLOCK, d), row),
        out_shape=jax.ShapeDtypeStruct((n, d), F32),
        name="ffn_dense",
        compiler_params=_params("parallel"),
    )(h2, x1, modp, *consts)


def _moe_kernel(be_ref, nb_ref, x_ref, w1_ref, w3_ref, w2_ref, o_ref):
    i = pl.program_id(0)

    @pl.when(i < nb_ref[0])
    def _():
        o_ref[...] = _swiglu(x_ref[...], w1_ref[0], w3_ref[0], w2_ref[0])

    @pl.when(i >= nb_ref[0])
    def _():
        o_ref[...] = jnp.zeros(o_ref.shape, F32)


def _moe_experts(block_e, n_used, xb, w1, w3, w2):
    p, d = xb.shape
    _, _, dff = w1.shape
    nblk = p // MOE_ROWS
    one = pl.Buffered(1)
    grid_spec = pltpu.PrefetchScalarGridSpec(
        num_scalar_prefetch=2,
        grid=(nblk,),
        in_specs=[pl.BlockSpec((MOE_ROWS, d), lambda i, be, nb: (i, 0)),
                  pl.BlockSpec((1, d, dff), lambda i, be, nb: (be[i], 0, 0), pipeline_mode=one),
                  pl.BlockSpec((1, d, dff), lambda i, be, nb: (be[i], 0, 0), pipeline_mode=one),
                  pl.BlockSpec((1, dff, d), lambda i, be, nb: (be[i], 0, 0), pipeline_mode=one)],
        out_specs=pl.BlockSpec((MOE_ROWS, d), lambda i, be, nb: (i, 0)),
    )
    return pl.pallas_call(
        _moe_kernel,
        grid_spec=grid_spec,
        out_shape=jax.ShapeDtypeStruct((p, d), F32),
        name="moe_experts",
        compiler_params=_params("arbitrary"),
    )(block_e, n_used, xb, w1, w3, w2)


def _combine_kernel(ya_ref, yb_ref, route_ref, x1_ref, mod_ref, lng_ref, lnb_ref, o_ref, *, alpha):
    f = route_ref[:, 2:3] * ya_ref[...] + route_ref[:, 3:4] * yb_ref[...]
    g2 = mod_ref[0, 5:6, :]
    o_ref[...] = _layer_norm(alpha * x1_ref[...] + g2 * f, lng_ref[...], lnb_ref[...])


def _moe_combine(ya, yb, route, x1, modp, ln_g, ln_b, t_len, alpha):
    n, d = x1.shape
    per_seq = t_len // ROW_BLOCK
    row = lambda i: (i, 0)
    return pl.pallas_call(
        functools.partial(_combine_kernel, alpha=alpha),
        grid=(n // ROW_BLOCK,),
        in_specs=[pl.BlockSpec((ROW_BLOCK, d), row), pl.BlockSpec((ROW_BLOCK, d), row),
                  pl.BlockSpec((ROW_BLOCK, V7X_LANES), row), pl.BlockSpec((ROW_BLOCK, d), row),
                  pl.BlockSpec((1, 6, d), lambda i: (i // per_seq, 0, 0)),
                  _const_spec(ln_g.shape), _const_spec(ln_b.shape)],
        out_specs=pl.BlockSpec((ROW_BLOCK, d), row),
        out_shape=jax.ShapeDtypeStruct((n, d), F32),
        name="moe_combine",
        compiler_params=_params("parallel"),
    )(ya, yb, route, x1, modp, ln_g, ln_b)


def _moe(h2, route, x1, modp, w1, w3, w2, ln_g, ln_b, t_len, alpha):
    n, d = x1.shape
    n_exp = w1.shape[0]
    top_e = route[:, 0:2].astype(jnp.int32)
    flat_e = top_e.reshape(-1)
    onehot = (flat_e[:, None] == jnp.arange(n_exp, dtype=jnp.int32)[None, :]).astype(jnp.int32)
    rank = jnp.sum((jnp.cumsum(onehot, axis=0) - onehot) * onehot, axis=1)
    counts = jnp.sum(onehot, axis=0)
    padded = (counts + MOE_ROWS - 1) // MOE_ROWS * MOE_ROWS
    pad_end = jnp.cumsum(padded)
    pad_start = pad_end - padded
    dest = pad_start[flat_e] + rank
    nblk = -(-(2 * n + n_exp * (MOE_ROWS - 1)) // MOE_ROWS)
    p = nblk * MOE_ROWS
    flat_tok = jnp.repeat(jnp.arange(n, dtype=jnp.int32), 2)
    buf_tok = jnp.full((p,), n, jnp.int32).at[dest].set(flat_tok)
    blk_start = jnp.arange(nblk, dtype=jnp.int32) * MOE_ROWS
    block_e = jnp.minimum(jnp.sum((pad_end[None, :] <= blk_start[:, None]).astype(jnp.int32), axis=1), n_exp - 1)
    n_used = (pad_end[-1] // MOE_ROWS).astype(jnp.int32).reshape(1)
    xb = jnp.concatenate([h2, jnp.zeros((1, d), h2.dtype)], axis=0)[buf_tok]
    yb = _moe_experts(block_e, n_used, xb, w1, w3, w2)
    dest2 = dest.reshape(n, 2)
    return _moe_combine(yb[dest2[:, 0]], yb[dest2[:, 1]], route, x1, modp, ln_g, ln_b, t_len, alpha)


def _layer(x, modp, lw, b_len, t_len, alpha, s0, ctx_kv):
    slab, qkv, gates = _in_proj(x, modp, lw["w_in"], t_len, lw["n_slab"], 3 * D_BRANCH)
    r, kk, lwf, lwb, bf, bb, ktf, ktb, v, bonus, g = _rwkv_prep(slab, lw, b_len, t_len)
    ytf, ytb, z_fin = _rwkv_scan(jnp.swapaxes(s0, -1, -2), r, kk, lwf, lwb, bf, bb, ktf, ktb, v, b_len, t_len)
    s_fin = jnp.swapaxes(z_fin, -1, -2)
    if ctx_kv is None:
        yn = _ctx_attention(qkv, b_len, t_len)
    else:
        yn = _nbr_attention(qkv, ctx_kv[0], ctx_kv[1], lw["bias"], b_len, t_len)
    outs = _mix_out(ytf, ytb, bonus, g, yn, gates, x, modp, lw, t_len, alpha, lw.get("router"))
    if "router" in lw:
        x1, h2, route = outs
        x2 = _moe(h2, route, x1, modp, lw["moe_w1"], lw["moe_w3"], lw["moe_w2"], lw["ln2_g"], lw["ln2_b"],
                  t_len, alpha)
    else:
        x1, h2 = outs
        x2 = _ffn_dense(h2, x1, modp, lw["ffn_w1"], lw["ffn_w3"], lw["ffn_w2"], lw["ln2_g"], lw["ln2_b"],
                        t_len, alpha)
    return x2, s_fin, qkv


def kernel(x_prompt, x_sample, c, state_rwkv, cache_k, cache_v, c_ctx, w_ada, b_ada, w_in, ts_w, rw_w0, rw_w2, rw_a0, rw_a2, rw_g2, rw_kk, rw_ka, rw_u, rw_gn_g, rw_gn_b, na_rpb, w_up_r, w_up_n, w_o, ln1_g, ln1_b, ln2_g, ln2_b, ffn_w1, ffn_w3, ffn_w2, moe_router, moe_w1, moe_w3, moe_w2):
    depth = w_in.shape[0]
    bc, tc_len, d = x_prompt.shape
    bs, ts_len, _ = x_sample.shape
    dr = D_BRANCH
    alpha = float((2 * depth) ** 0.25)
    n_slab = ts_w.shape[2]
    lora_w = rw_w2.shape[2]
    lora_a = rw_a2.shape[2]

    cvec = jnp.zeros((8, d), F32).at[0].set(c_ctx).at[1:1 + bs].set(c)
    mod = _ada_mod(cvec, w_ada, b_ada)

    half = np.kron(np.eye(2, dtype=np.float32), np.ones((HEAD_DIM, HEAD_DIM), np.float32))
    bd = jnp.asarray(half, BF16)
    rows = ts_len // GRID_W

    xp = x_prompt.reshape(bc * tc_len, d)
    xs = x_sample.reshape(bs * ts_len, d)
    zero_state = jnp.zeros((bc, 2, N_HEADS, HEAD_DIM, HEAD_DIM), F32)
    new_s, new_k, new_v = [], [], []
    for l in range(depth):
        w2blk = jnp.zeros((2 * lora_w, 2 * dr), F32).at[:lora_w, :dr].set(rw_w2[l, 0]).at[lora_w:, dr:].set(rw_w2[l, 1])
        a2blk = jnp.zeros((2 * lora_a, 2 * dr), F32).at[:lora_a, :dr].set(rw_a2[l, 0]).at[lora_a:, dr:].set(rw_a2[l, 1])
        lw = dict(
            n_slab=n_slab,
            w_in=w_in[l].astype(BF16), ts=ts_w[l],
            w0=rw_w0[l].reshape(1, 2 * dr), w2=w2blk.astype(BF16),
            a0=rw_a0[l].reshape(1, 2 * dr), a2=a2blk.astype(BF16),
            g2=rw_g2[l].astype(BF16), kkw=rw_kk[l].reshape(1, dr), ka=rw_ka[l].reshape(1, dr),
            u=rw_u[l].reshape(1, dr), bd=bd,
            gn_g=rw_gn_g[l].reshape(1, dr), gn_b=rw_gn_b[l].reshape(1, dr),
            w_up_r=w_up_r[l].astype(BF16), w_up_n=w_up_n[l].astype(BF16), w_o=w_o[l].astype(BF16),
            ln1_g=ln1_g[l].reshape(1, d), ln1_b=ln1_b[l].reshape(1, d),
            ln2_g=ln2_g[l].reshape(1, d), ln2_b=ln2_b[l].reshape(1, d),
            bias=_nbr_bias_table(na_rpb[l], rows),
        )
        if l % 2 == 0:
            lw.update(ffn_w1=ffn_w1[l // 2].astype(BF16), ffn_w3=ffn_w3[l // 2].astype(BF16),
                      ffn_w2=ffn_w2[l // 2].astype(BF16))
        else:
            lw.update(router=moe_router[l // 2], moe_w1=moe_w1[l // 2].astype(BF16),
                      moe_w3=moe_w3[l // 2].astype(BF16), moe_w2=moe_w2[l // 2].astype(BF16))

        mod_ctx = jnp.broadcast_to(mod[l, 0].reshape(1, 6, d), (bc, 6, d))
        mod_lat = mod[l, 1:1 + bs].reshape(bs, 6, d)

        xp, s_fin, qkv_c = _layer(xp, mod_ctx, lw, bc, tc_len, alpha, zero_state, None)
        new_s.append(s_fin)
        new_k.append(qkv_c[:, dr:2 * dr].reshape(bc, tc_len, N_HEADS, HEAD_DIM))
        new_v.append(qkv_c[:, 2 * dr:3 * dr].reshape(bc, tc_len, N_HEADS, HEAD_DIM))

        kv = (cache_k[:, l].reshape(bs, -1, dr), cache_v[:, l].reshape(bs, -1, dr))
        xs, _, _ = _layer(xs, mod_lat, lw, bs, ts_len, alpha, state_rwkv[:, l], kv)

    return (xp.reshape(bc, tc_len, d), xs.reshape(bs, ts_len, d),
            jnp.stack(new_s, axis=1), jnp.stack(new_k, axis=1), jnp.stack(new_v, axis=1))
```

```python
import functools

import numpy as np
import jax
import jax.numpy as jnp
from jax import lax
from jax.experimental import pallas as pl
from jax.experimental.pallas import tpu as pltpu

F32 = jnp.float32
BF16 = jnp.bfloat16
HIGHEST = lax.Precision.HIGHEST

N_HEADS = 8
HEAD_DIM = 64
D_BRANCH = N_HEADS * HEAD_DIM
V7X_LANES = 128
ROW_BLOCK = 256
SCAN_CHUNK = 64
SCAN_CHUNKS_PER_STEP = 2
MOE_ROWS = 256
KH_MAX = 8
KW = 16
GRID_W = 64
LN_EPS = 1e-5
GN_EPS = 64e-5
NEG_BIG = -1e30
VMEM_LIMIT = 56 * 1024 * 1024


def _params(*sem):
    return pltpu.CompilerParams(dimension_semantics=sem, vmem_limit_bytes=VMEM_LIMIT)


def _const_spec(shape):
    nd = len(shape)
    return pl.BlockSpec(shape, lambda *_: (0,) * nd, pipeline_mode=pl.Buffered(1))


def _dot(a, b):
    return jnp.dot(a, b, preferred_element_type=F32)


def _dot_nt(a, b):
    return lax.dot_general(a, b, (((1,), (1,)), ((), ())), preferred_element_type=F32)


def _dot_tn(a, b):
    return lax.dot_general(a, b, (((0,), (0,)), ((), ())), preferred_element_type=F32)


def _split2(x):
    hi = x.astype(BF16)
    return hi, (x - hi.astype(F32)).astype(BF16)


def _mm3(a, b, f=_dot):
    ah, al = _split2(a)
    bh, bl = _split2(b)
    ax = 1 if f is _dot_tn else 0
    m = a.shape[1] if f is _dot_tn else a.shape[0]
    both = f(jnp.concatenate([ah, al], axis=ax), bh)
    return both[0:m] + (f(ah, bl) + both[m:])


def _mm1(a, b, f=_dot):
    return f(a.astype(BF16), b.astype(BF16))


def _mm_exact_lhs(m, x):
    hi = x.astype(BF16)
    r1 = x - hi.astype(F32)
    mid = r1.astype(BF16)
    lo = (r1 - mid.astype(F32)).astype(BF16)
    return _dot(m, hi) + (_dot(m, mid) + _dot(m, lo))


def _sigmoid(x):
    return 1.0 / (1.0 + jnp.exp(-x))


def _head_sum(x, bd):
    outs = []
    for j in range(x.shape[1] // V7X_LANES):
        xs = x[:, j * V7X_LANES:(j + 1) * V7X_LANES]
        hi = xs.astype(BF16)
        lo = (xs - hi.astype(F32)).astype(BF16)
        outs.append(_dot(hi, bd) + _dot(lo, bd))
    return jnp.concatenate(outs, axis=1)


def _layer_norm(z, g, b):
    mu = jnp.mean(z, axis=-1, keepdims=True)
    zc = z - mu
    var = jnp.mean(zc * zc, axis=-1, keepdims=True)
    return zc * lax.rsqrt(var + LN_EPS) * g + b


def _ada_kernel(c_ref, w_ref, b_ref, o_ref):
    cs = c_ref[...]
    s = cs * _sigmoid(cs)
    o_ref[0] = jnp.dot(s, w_ref[0], preferred_element_type=F32, precision=HIGHEST) + b_ref[0]


def _ada_mod(cvec, w_ada, b_ada):
    depth, d, n6 = w_ada.shape
    tn = n6 // 4
    return pl.pallas_call(
        _ada_kernel,
        grid=(depth, n6 // tn),
        in_specs=[pl.BlockSpec((8, d), lambda l, j: (0, 0)),
                  pl.BlockSpec((1, d, tn), lambda l, j: (l, 0, j)),
                  pl.BlockSpec((1, 1, tn), lambda l, j: (l, 0, j))],
        out_specs=pl.BlockSpec((1, 8, tn), lambda l, j: (l, 0, j)),
        out_shape=jax.ShapeDtypeStruct((depth, 8, n6), F32),
        name="ada_mod",
        compiler_params=_params("parallel", "parallel"),
    )(cvec, w_ada, b_ada.reshape(depth, 1, n6))


def _inproj_kernel(x_ref, mod_ref, w_ref, slab_ref, qkv_ref, gate_ref, *, n_slab, n_qkv):
    sh = mod_ref[0, 0:1, :]
    sc = mod_ref[0, 1:2, :]
    h = (x_ref[...] * (1.0 + sc) + sh).astype(BF16)
    slab_ref[...] = _dot(h, w_ref[:, 0:n_slab])
    qkv_ref[...] = _dot(h, w_ref[:, n_slab:n_slab + n_qkv])
    gate_ref[...] = _sigmoid(_dot(h, w_ref[:, n_slab + n_qkv:]))


def _in_proj(x, modp, w_in, t_len, n_slab, n_qkv):
    n, d = x.shape
    n_cols = w_in.shape[1]
    n_gate = n_cols - n_slab - n_qkv
    per_seq = t_len // ROW_BLOCK
    return pl.pallas_call(
        functools.partial(_inproj_kernel, n_slab=n_slab, n_qkv=n_qkv),
        grid=(n // ROW_BLOCK,),
        in_specs=[pl.BlockSpec((ROW_BLOCK, d), lambda i: (i, 0)),
                  pl.BlockSpec((1, 6, d), lambda i: (i // per_seq, 0, 0)),
                  _const_spec((d, n_cols))],
        out_specs=[pl.BlockSpec((ROW_BLOCK, n_slab), lambda i: (i, 0)),
                   pl.BlockSpec((ROW_BLOCK, n_qkv), lambda i: (i, 0)),
                   pl.BlockSpec((ROW_BLOCK, n_gate), lambda i: (i, 0))],
        out_shape=[jax.ShapeDtypeStruct((n, n_slab), F32),
                   jax.ShapeDtypeStruct((n, n_qkv), F32),
                   jax.ShapeDtypeStruct((n, n_gate), F32)],
        name="in_proj",
        compiler_params=_params("parallel"),
    )(x, modp, w_in)


def _prep_kernel(cur_ref, prev_ref, next_ref, ts_ref, w0_ref, w2_ref, a0_ref, a2_ref, g2_ref,
                 kkw_ref, ka_ref, u_ref, bd_ref,
                 r_o, kk_o, lwf_o, lwb_o, bf_o, bb_o, ktf_o, ktb_o, v_o, bonus_o, g_o,
                 *, per_seq, lora_w, lora_a):
    i = pl.program_id(0)
    tb, n_slab = cur_ref.shape
    dr = D_BRANCH
    cur = cur_ref[...]
    first = (i % per_seq) == 0
    last = (i % per_seq) == per_seq - 1
    prow = jnp.where(first, 0.0, prev_ref[7:8, :])
    nrow = jnp.where(last, 0.0, next_ref[0:1, :])
    rid = lax.broadcasted_iota(jnp.int32, (tb, n_slab), 0)
    xp = jnp.where(rid == 0, prow, pltpu.roll(cur, 1, 0))
    xn = jnp.where(rid == tb - 1, nrow, pltpu.roll(cur, tb - 1, 0))
    s = ts_ref[0:1, :] * xp + ts_ref[1:2, :] * cur + ts_ref[2:3, :] * xn

    r = s[:, 0:dr]
    k = s[:, dr:2 * dr]
    v = s[:, 2 * dr:3 * dr]
    o = 3 * dr
    wd = s[:, o:o + 2 * lora_w]
    ad = s[:, o + 2 * lora_w:o + 2 * lora_w + 2 * lora_a]
    gd = s[:, o + 2 * lora_w + 2 * lora_a:]

    wl = w0_ref[...] + _dot(jnp.tanh(wd).astype(BF16), w2_ref[...])
    logw = -float(np.exp(-0.5)) * _sigmoid(wl)
    a = _sigmoid(a0_ref[...] + _dot(ad.astype(BF16), a2_ref[...]))
    g = _dot(_sigmoid(gd).astype(BF16), g2_ref[...])

    bd = bd_ref[...]
    kk = k * kkw_ref[...]
    kk = kk * lax.rsqrt(_head_sum(kk * kk, bd) + 1e-12)
    ka = ka_ref[...]
    a_f = a[:, :dr]
    a_b = a[:, dr:]
    kt_f = k * (1.0 + (a_f - 1.0) * ka)
    kt_b = k * (1.0 + (a_b - 1.0) * ka)
    bonus = _head_sum(r * (0.5 * (kt_f + kt_b)) * u_ref[...], bd) * v

    r_o[...] = r
    kk_o[...] = kk
    lwf_o[...] = logw[:, :dr]
    lwb_o[...] = logw[:, dr:]
    bf_o[...] = kk * a_f
    bb_o[...] = kk * a_b
    ktf_o[...] = kt_f
    ktb_o[...] = kt_b
    v_o[...] = v
    bonus_o[...] = bonus
    g_o[...] = g


def _rwkv_prep(slab, lw, b_len, t_len):
    n, n_slab = slab.shape
    dr = D_BRANCH
    per_seq = t_len // ROW_BLOCK
    sub = ROW_BLOCK // 8
    lora_w = lw["w2"].shape[0] // 2
    lora_a = lw["a2"].shape[0] // 2
    row = lambda i: (i, 0)
    consts = [lw["ts"], lw["w0"], lw["w2"], lw["a0"], lw["a2"], lw["g2"], lw["kkw"], lw["ka"], lw["u"], lw["bd"]]
    outs = pl.pallas_call(
        functools.partial(_prep_kernel, per_seq=per_seq, lora_w=lora_w, lora_a=lora_a),
        grid=(n // ROW_BLOCK,),
        in_specs=[pl.BlockSpec((ROW_BLOCK, n_slab), row),
                  pl.BlockSpec((8, n_slab), lambda i: (jnp.maximum(i * sub - 1, 0), 0)),
                  pl.BlockSpec((8, n_slab), lambda i: (jnp.minimum((i + 1) * sub, n // 8 - 1), 0))]
                 + [_const_spec(c.shape) for c in consts],
        out_specs=[pl.BlockSpec((ROW_BLOCK, dr), row)] * 11,
        out_shape=[jax.ShapeDtypeStruct((n, dr), F32)] * 11,
        name="rwkv_prep",
        compiler_params=_params("parallel"),
    )(slab, slab, slab, *consts)
    return outs


def _tri_masks(rev):
    c = SCAN_CHUNK
    ti = lax.broadcasted_iota(jnp.int32, (c, 2 * c), 0)
    ii = lax.broadcasted_iota(jnp.int32, (c, 2 * c), 1) % c
    if rev:
        ti, ii = c - 1 - ti, c - 1 - ii
    one = lambda m: jnp.where(m, 1.0, 0.0)
    levels = []
    s = 1
    while s < c:
        same = (ti // (2 * s)) == (ii // (2 * s))
        levels.append(one(same & ((ti % (2 * s)) >= s) & ((ii % (2 * s)) < s)))
        s *= 2
    return one(ii <= ti), one(ii < ti), levels


def _scan_kernel(z0_ref, rf, kkf, lwf, bf, ktf, vf, rb, kkb, lwb, bb, ktb, vb, yf_o, yb_o, zfin_o, z_scr):
    j = pl.program_id(1)
    nj = pl.num_programs(1)
    c = SCAN_CHUNK
    lanes = V7X_LANES
    npair = N_HEADS // 2
    assert c == HEAD_DIM and 2 * HEAD_DIM == lanes

    row_blk = lax.broadcasted_iota(jnp.int32, (lanes, lanes), 0) // HEAD_DIM
    col_blk = lax.broadcasted_iota(jnp.int32, (lanes, lanes), 1) // HEAD_DIM
    bd_mask = jnp.where(row_blk == col_blk, 1.0, 0.0)
    eye_full = jnp.where(lax.broadcasted_iota(jnp.int32, (lanes, lanes), 0)
                         == lax.broadcasted_iota(jnp.int32, (lanes, lanes), 1), 1.0, 0.0)
    lane_c = lax.broadcasted_iota(jnp.int32, (c, lanes), 1)
    lo = jnp.where(lane_c < HEAD_DIM, 1.0, 0.0)
    hi = 1.0 - lo
    eye_pair = jnp.where(lax.broadcasted_iota(jnp.int32, (c, lanes), 0) == lane_c % HEAD_DIM, 1.0, 0.0)

    def bd(x):
        return jnp.concatenate([x * lo, x * hi], axis=0)

    @pl.when(j == 0)
    def _():
        for d in range(2):
            for p in range(npair):
                za = jnp.concatenate([z0_ref[0, d, 2 * p], z0_ref[0, d, 2 * p + 1]], axis=1)
                z_scr[d * npair + p] = bd(za)

    refs = ((rf, kkf, lwf, bf, ktf, vf, yf_o), (rb, kkb, lwb, bb, ktb, vb, yb_o))
    masks = (_tri_masks(False), _tri_masks(True))
    cum_lhs = [jnp.concatenate([m[0][:, 0:c].astype(BF16), jnp.ones((c, c), BF16)], axis=0) for m in masks]
    nck = rf.shape[0] // c
    units = [(d, p, ck) for d in range(2) for p in range(npair) for ck in range(nck)]

    def tile(d, p, ck):
        row0 = (nck - 1 - ck if d else ck) * c
        return slice(row0, row0 + c), slice(p * lanes, (p + 1) * lanes)

    def per(fn):
        return [fn(i, d, tile(d, p, ck)) for i, (d, p, ck) in enumerate(units)]

    lw = per(lambda i, d, sl: refs[d][2][sl])
    cs = per(lambda i, d, sl: _mm_exact_lhs(cum_lhs[d], lw[i]))
    cum = [x[0:c] for x in cs]
    tot = [x[c:] for x in cs]
    r_s = per(lambda i, d, sl: refs[d][0][sl] * jnp.exp(cum[i]))
    k_s = per(lambda i, d, sl: refs[d][1][sl] * jnp.exp(cum[i] - lw[i]))
    e_neg = per(lambda i, d, sl: jnp.exp(-cum[i]))
    e_end = per(lambda i, d, sl: jnp.exp(tot[i] - cum[i]))
    b = per(lambda i, d, sl: refs[d][3][sl])
    kt = per(lambda i, d, sl: refs[d][4][sl])
    v = per(lambda i, d, sl: refs[d][5][sl])
    v_bd = per(lambda i, d, sl: bd(v[i]))
    kr = per(lambda i, d, sl: jnp.concatenate([k_s[i], r_s[i]], axis=0))
    ab = per(lambda i, d, sl: _mm3(kr[i], bd(b[i] * e_neg[i]), _dot_nt))
    ak = per(lambda i, d, sl: _mm3(kr[i], bd(kt[i] * e_neg[i]), _dot_nt))
    a_b = per(lambda i, d, sl: ab[i][0:c] * masks[d][1])
    b_b = per(lambda i, d, sl: ab[i][c:] * masks[d][0])
    a_kt = per(lambda i, d, sl: ak[i][0:c] * masks[d][1])
    b_kt = per(lambda i, d, sl: ak[i][c:] * masks[d][0])
    dm = per(lambda i, d, sl: eye_pair - a_b[i] * masks[d][2][0])
    for lvl in range(1, len(masks[0][2])):
        zd = per(lambda i, d, sl: _mm1(a_b[i] * masks[d][2][lvl], bd(dm[i])))
        dm = per(lambda i, d, sl: dm[i] - _mm1(dm[i], bd(zd[i])))
    w1 = per(lambda i, d, sl: _mm1(a_kt[i], v_bd[i]))
    g = per(lambda i, d, sl: _mm1(dm[i], jnp.concatenate([bd(k_s[i]), bd(w1[i])], axis=1)))
    bg = per(lambda i, d, sl: _mm3(b_b[i], jnp.concatenate([bd(g[i][:, 0:lanes]), bd(g[i][:, lanes:])], axis=1)))
    q = per(lambda i, d, sl: r_s[i] - bg[i][:, 0:lanes])
    y0 = per(lambda i, d, sl: _mm3(b_kt[i], v_bd[i]) - bg[i][:, lanes:])
    cg = per(lambda i, d, sl: _mm3(b[i] * e_end[i], g[i], _dot_tn))
    phi = per(lambda i, d, sl: eye_full * jnp.exp(jnp.concatenate([tot[i], tot[i]], axis=0))
              - cg[i][:, 0:lanes] * bd_mask)
    psi = per(lambda i, d, sl: (_mm3(kt[i] * e_end[i], v[i], _dot_tn) - cg[i][:, lanes:]) * bd_mask)
    z = [z_scr[s] for s in range(2 * npair)]
    for ck in range(nck):
        sel = [(i, d, p) for i, (d, p, k2) in enumerate(units) if k2 == ck]
        y = [_mm1(q[i], z[d * npair + p]) + y0[i] for i, d, p in sel]
        z = [_mm1(phi[i], z[d * npair + p]) + psi[i] for i, d, p in sel]
        for yy, (i, d, p) in zip(y, sel):
            refs[d][6][tile(d, p, ck)] = yy
    for s in range(2 * npair):
        z_scr[s] = z[s]

    @pl.when(j == nj - 1)
    def _():
        for d in range(2):
            for p in range(npair):
                zz = z_scr[d * npair + p]
                zfin_o[0, d, 2 * p] = zz[0:HEAD_DIM, 0:HEAD_DIM]
                zfin_o[0, d, 2 * p + 1] = zz[HEAD_DIM:, HEAD_DIM:]


def _rwkv_scan(z0, r, kk, lwf, lwb, bf, bb, ktf, ktb, v, b_len, t_len):
    blk = SCAN_CHUNK * SCAN_CHUNKS_PER_STEP
    nj = t_len // blk
    n = b_len * t_len
    dr = D_BRANCH
    fwd = pl.BlockSpec((blk, dr), lambda b, j: (b * nj + j, 0))
    bwd = pl.BlockSpec((blk, dr), lambda b, j: (b * nj + nj - 1 - j, 0))
    st = pl.BlockSpec((1, 2, N_HEADS, HEAD_DIM, HEAD_DIM), lambda b, j: (b, 0, 0, 0, 0))
    return pl.pallas_call(
        _scan_kernel,
        grid=(b_len, nj),
        in_specs=[st] + [fwd] * 6 + [bwd] * 6,
        out_specs=[fwd, bwd, st],
        out_shape=[jax.ShapeDtypeStruct((n, dr), F32),
                   jax.ShapeDtypeStruct((n, dr), F32),
                   jax.ShapeDtypeStruct((b_len, 2, N_HEADS, HEAD_DIM, HEAD_DIM), F32)],
        scratch_shapes=[pltpu.VMEM((N_HEADS, V7X_LANES, V7X_LANES), F32)],
        name="rwkv_scan",
        compiler_params=_params("parallel", "arbitrary"),
    )(z0, r, kk, lwf, bf, ktf, v, r, kk, lwb, bb, ktb, v)


def _softmax_rows(parts):
    m = parts[0].max(axis=-1, keepdims=True)
    for p in parts[1:]:
        m = jnp.maximum(m, p.max(axis=-1, keepdims=True))
    es = [jnp.exp(p - m) for p in parts]
    den = es[0].sum(axis=-1, keepdims=True)
    for e in es[1:]:
        den = den + e.sum(axis=-1, keepdims=True)
    inv = 1.0 / den
    return [e * inv for e in es]


def _ctx_attn_kernel(q_ref, k_ref, v_ref, o_ref):
    scale = HEAD_DIM ** -0.5
    outs = []
    for hh in range(2):
        cols = slice(hh * HEAD_DIM, (hh + 1) * HEAD_DIM)
        q = q_ref[:, cols].astype(BF16)
        k = k_ref[:, cols].astype(BF16)
        v = v_ref[:, cols].astype(BF16)
        (p,) = _softmax_rows([_dot_nt(q, k) * scale])
        outs.append(_dot(p.astype(BF16), v))
    o_ref[...] = jnp.concatenate(outs, axis=1).astype(o_ref.dtype)


def _ctx_attention(qkv, b_len, t_len):
    n = qkv.shape[0]
    npair = N_HEADS // 2
    return pl.pallas_call(
        _ctx_attn_kernel,
        grid=(b_len, npair),
        in_specs=[pl.BlockSpec((t_len, V7X_LANES), lambda b, p: (b, p)),
                  pl.BlockSpec((t_len, V7X_LANES), lambda b, p: (b, npair + p)),
                  pl.BlockSpec((t_len, V7X_LANES), lambda b, p: (b, 2 * npair + p))],
        out_specs=pl.BlockSpec((t_len, V7X_LANES), lambda b, p: (b, p)),
        out_shape=jax.ShapeDtypeStruct((n, D_BRANCH), BF16),
        name="ctx_attn",
        compiler_params=_params("parallel", "parallel"),
    )(qkv, qkv, qkv)


def _nbr_attn_kernel(q_ref, k_ref, v_ref, kc_ref, vc_ref, bias_ref, o_ref, *, rows, kh):
    scale = HEAD_DIM ** -0.5
    win = kh * GRID_W

    group = 4 if rows % 4 == 0 else 1
    units = [(r, hh) for r in range(group) for hh in range(2)]
    head = lambda x, hh: x[:, hh * HEAD_DIM:(hh + 1) * HEAD_DIM]

    def body(gi, carry):
        def per(fn):
            return [fn(u, r, hh) for u, (r, hh) in enumerate(units)]

        i = [gi * group + r for r in range(group)]
        rs = [jnp.clip(x - kh // 2, 0, rows - kh) for x in i]
        q0 = [pl.multiple_of(x * GRID_W, GRID_W) for x in i]
        k0 = [pl.multiple_of(x * GRID_W, GRID_W) for x in rs]
        qb = [q_ref[pl.ds(x, GRID_W), :].astype(BF16) for x in q0]
        kw = [k_ref[pl.ds(x, win), :].astype(BF16) for x in k0]
        vw = [v_ref[pl.ds(x, win), :].astype(BF16) for x in k0]
        kc = kc_ref[0].astype(BF16)
        vc = vc_ref[0].astype(BF16)
        q = per(lambda u, r, hh: head(qb[r], hh))
        s_loc = per(lambda u, r, hh: _dot_nt(q[u], head(kw[r], hh)) * scale + bias_ref[hh, i[r] - rs[r]])
        s_ctx = per(lambda u, r, hh: _dot_nt(q[u], head(kc, hh)) * scale)
        m = per(lambda u, r, hh: jnp.maximum(s_loc[u].max(axis=-1, keepdims=True), s_ctx[u].max(axis=-1, keepdims=True)))
        e_loc = per(lambda u, r, hh: jnp.exp(s_loc[u] - m[u]))
        e_ctx = per(lambda u, r, hh: jnp.exp(s_ctx[u] - m[u]))
        inv = per(lambda u, r, hh: 1.0 / (e_loc[u].sum(axis=-1, keepdims=True) + e_ctx[u].sum(axis=-1, keepdims=True)))
        out = per(lambda u, r, hh: _dot((e_loc[u] * inv[u]).astype(BF16), head(vw[r], hh))
                  + _dot((e_ctx[u] * inv[u]).astype(BF16), head(vc, hh)))
        for r in range(group):
            o_ref[pl.ds(q0[r], GRID_W), :] = jnp.concatenate([out[2 * r], out[2 * r + 1]], axis=1).astype(o_ref.dtype)
        return carry

    lax.fori_loop(0, rows // group, body, 0)


def _nbr_attention(qkv, k_ctx, v_ctx, bias, b_len, t_len):
    n = qkv.shape[0]
    npair = N_HEADS // 2
    rows = t_len // GRID_W
    kh = min(KH_MAX, rows)
    c_len = k_ctx.shape[1]
    return pl.pallas_call(
        functools.partial(_nbr_attn_kernel, rows=rows, kh=kh),
        grid=(b_len, npair),
        in_specs=[pl.BlockSpec((t_len, V7X_LANES), lambda b, p: (b, p)),
                  pl.BlockSpec((t_len, V7X_LANES), lambda b, p: (b, npair + p)),
                  pl.BlockSpec((t_len, V7X_LANES), lambda b, p: (b, 2 * npair + p)),
                  pl.BlockSpec((1, c_len, V7X_LANES), lambda b, p: (b, 0, p)),
                  pl.BlockSpec((1, c_len, V7X_LANES), lambda b, p: (b, 0, p)),
                  pl.BlockSpec((2, kh, GRID_W, kh * GRID_W), lambda b, p: (p, 0, 0, 0))],
        out_specs=pl.BlockSpec((t_len, V7X_LANES), lambda b, p: (b, p)),
        out_shape=jax.ShapeDtypeStruct((n, D_BRANCH), BF16),
        name="nbr_attn",
        compiler_params=_params("parallel", "parallel"),
    )(qkv, qkv, qkv, k_ctx, v_ctx, bias)


def _nbr_bias_table(rpb, rows):
    kh = min(KH_MAX, rows)
    w = np.arange(GRID_W)
    col_start = np.clip(w - KW // 2, 0, GRID_W - KW)
    c = np.arange(GRID_W)
    inside = (c[None, :] >= col_start[:, None]) & (c[None, :] < col_start[:, None] + KW)
    rp = jnp.pad(rpb, ((0, 0), (0, 0), (GRID_W - KW, GRID_W - KW)))
    toep = jnp.stack([rp[:, :, GRID_W - 1 - wi:2 * GRID_W - 1 - wi] for wi in range(GRID_W)], axis=2)
    toep = jnp.where(inside[None, None], toep, NEG_BIG)
    tab = jnp.stack([toep[:, KH_MAX - 1 - dd:KH_MAX - 1 - dd + kh] for dd in range(kh)], axis=1)
    tab = jnp.transpose(tab, (0, 1, 3, 2, 4))
    return tab.reshape(rpb.shape[0], kh, GRID_W, kh * GRID_W)


def _top2_route(logits, n_exp):
    lane = lax.broadcasted_iota(jnp.int32, logits.shape, 1)
    lane_f = lane.astype(F32)
    lg = jnp.where(lane < n_exp, logits, -jnp.inf)
    m1 = lg.max(axis=-1, keepdims=True)
    i1 = jnp.where(lg == m1, lane_f, float(V7X_LANES)).min(axis=-1, keepdims=True)
    lg2 = jnp.where(lane_f == i1, -jnp.inf, lg)
    m2 = lg2.max(axis=-1, keepdims=True)
    i2 = jnp.where(lg2 == m2, lane_f, float(V7X_LANES)).min(axis=-1, keepdims=True)
    e = jnp.exp(m2 - m1)
    g1 = 1.0 / (1.0 + e)
    g2 = e * g1
    out = jnp.where(lane == 0, i1, 0.0)
    out = jnp.where(lane == 1, i2, out)
    out = jnp.where(lane == 2, g1, out)
    return jnp.where(lane == 3, g2, out)


def _mix_kernel(yf_ref, yb_ref, bonus_ref, g_ref, yn_ref, gate_ref, x_ref, mod_ref,
                gng_ref, gnb_ref, bd_ref, wur_ref, wun_ref, wo_ref, lng_ref, lnb_ref, *rest,
                alpha, n_exp):
    if n_exp:
        router_ref, x1_o, h2_o, route_o = rest
    else:
        x1_o, h2_o = rest
    d = x_ref.shape[1]
    bd = bd_ref[...]
    y = yf_ref[...] + yb_ref[...]
    inv_n = 1.0 / HEAD_DIM
    mu = _head_sum(y, bd) * inv_n
    yc = y - mu
    var = _head_sum(yc * yc, bd) * inv_n
    yr = yc * lax.rsqrt(var + GN_EPS) * gng_ref[...] + gnb_ref[...]
    yr = (yr + bonus_ref[...]) * g_ref[...]
    m = (gate_ref[:, 0:d] * _dot(yr.astype(BF16), wur_ref[...])
         + gate_ref[:, d:] * _dot(yn_ref[...], wun_ref[...]))
    o = _dot(m.astype(BF16), wo_ref[...])
    g1 = mod_ref[0, 2:3, :]
    sh2 = mod_ref[0, 3:4, :]
    sc2 = mod_ref[0, 4:5, :]
    x1 = _layer_norm(alpha * x_ref[...] + g1 * o, lng_ref[...], lnb_ref[...])
    x1_o[...] = x1
    h2 = x1 * (1.0 + sc2) + sh2
    h2_o[...] = h2.astype(BF16)
    if n_exp:
        logits = jnp.dot(h2, router_ref[...], preferred_element_type=F32, precision=HIGHEST)
        route_o[...] = _top2_route(logits, n_exp)


def _mix_out(yf, yb, bonus, g, yn, gates, x, modp, lw, t_len, alpha, router):
    n, d = x.shape
    dr = D_BRANCH
    per_seq = t_len // ROW_BLOCK
    row = lambda i: (i, 0)
    yt_spec = pl.BlockSpec((ROW_BLOCK, dr), row)
    consts = [lw["gn_g"], lw["gn_b"], lw["bd"], lw["w_up_r"], lw["w_up_n"], lw["w_o"], lw["ln1_g"], lw["ln1_b"]]
    n_exp = 0
    out_specs = [pl.BlockSpec((ROW_BLOCK, d), row), pl.BlockSpec((ROW_BLOCK, d), row)]
    out_shape = [jax.ShapeDtypeStruct((n, d), F32), jax.ShapeDtypeStruct((n, d), BF16)]
    if router is not None:
        n_exp = router.shape[1]
        consts.append(jnp.pad(router, ((0, 0), (0, V7X_LANES - n_exp))))
        out_specs.append(pl.BlockSpec((ROW_BLOCK, V7X_LANES), row))
        out_shape.append(jax.ShapeDtypeStruct((n, V7X_LANES), F32))
    return pl.pallas_call(
        functools.partial(_mix_kernel, alpha=alpha, n_exp=n_exp),
        grid=(n // ROW_BLOCK,),
        in_specs=[yt_spec, yt_spec,
                  pl.BlockSpec((ROW_BLOCK, dr), row), pl.BlockSpec((ROW_BLOCK, dr), row),
                  pl.BlockSpec((ROW_BLOCK, dr), row), pl.BlockSpec((ROW_BLOCK, 2 * d), row),
                  pl.BlockSpec((ROW_BLOCK, d), row),
                  pl.BlockSpec((1, 6, d), lambda i: (i // per_seq, 0, 0))]
                 + [_const_spec(c.shape) for c in consts],
        out_specs=out_specs,
        out_shape=out_shape,
        name="mix_out",
        compiler_params=_params("parallel"),
    )(yf, yb, bonus, g, yn, gates, x, modp, *consts)


def _swiglu(h, w1, w3, w2):
    u = _dot(h, w1)
    act = u * _sigmoid(u) * _dot(h, w3)
    return _dot(act.astype(BF16), w2)


def _ffn_kernel(h_ref, x1_ref, mod_ref, w1_ref, w3_ref, w2_ref, lng_ref, lnb_ref, o_ref, *, alpha):
    f = _swiglu(h_ref[...], w1_ref[...], w3_ref[...], w2_ref[...])
    g2 = mod_ref[0, 5:6, :]
    o_ref[...] = _layer_norm(alpha * x1_ref[...] + g2 * f, lng_ref[...], lnb_ref[...])


def _ffn_dense(h2, x1, modp, w1, w3, w2, ln_g, ln_b, t_len, alpha):
    n, d = x1.shape
    per_seq = t_len // ROW_BLOCK
    row = lambda i: (i, 0)
    consts = [w1, w3, w2, ln_g, ln_b]
    return pl.pallas_call(
        functools.partial(_ffn_kernel, alpha=alpha),
        grid=(n // ROW_BLOCK,),
        in_specs=[pl.BlockSpec((ROW_BLOCK, d), row), pl.BlockSpec((ROW_BLOCK, d), row),
                  pl.BlockSpec((1, 6, d), lambda i: (i // per_seq, 0, 0))]
                 + [_const_spec(c.shape) for c in consts],
        out_specs=pl.BlockSpec((ROW_BLOCK, d), row),
        out_shape=jax.ShapeDtypeStruct((n, d), F32),
        name="ffn_dense",
        compiler_params=_params("parallel"),
    )(h2, x1, modp, *consts)


def _moe_kernel(be_ref, nb_ref, x_ref, w1_ref, w3_ref, w2_ref, o_ref):
    i = pl.program_id(0)

    @pl.when(i < nb_ref[0])
    def _():
        o_ref[...] = _swiglu(x_ref[...], w1_ref[0], w3_ref[0], w2_ref[0])

    @pl.when(i >= nb_ref[0])
    def _():
        o_ref[...] = jnp.zeros(o_ref.shape, F32)


def _moe_experts(block_e, n_used, xb, w1, w3, w2):
    p, d = xb.shape
    _, _, dff = w1.shape
    nblk = p // MOE_ROWS
    one = pl.Buffered(1)
    grid_spec = pltpu.PrefetchScalarGridSpec(
        num_scalar_prefetch=2,
        grid=(nblk,),
        in_specs=[pl.BlockSpec((MOE_ROWS, d), lambda i, be, nb: (i, 0)),
                  pl.BlockSpec((1, d, dff), lambda i, be, nb: (be[i], 0, 0), pipeline_mode=one),
                  pl.BlockSpec((1, d, dff), lambda i, be, nb: (be[i], 0, 0), pipeline_mode=one),
                  pl.BlockSpec((1, dff, d), lambda i, be, nb: (be[i], 0, 0), pipeline_mode=one)],
        out_specs=pl.BlockSpec((MOE_ROWS, d), lambda i, be, nb: (i, 0)),
    )
    return pl.pallas_call(
        _moe_kernel,
        grid_spec=grid_spec,
        out_shape=jax.ShapeDtypeStruct((p, d), F32),
        name="moe_experts",
        compiler_params=_params("arbitrary"),
    )(block_e, n_used, xb, w1, w3, w2)


def _combine_kernel(ya_ref, yb_ref, route_ref, x1_ref, mod_ref, lng_ref, lnb_ref, o_ref, *, alpha):
    f = route_ref[:, 2:3] * ya_ref[...] + route_ref[:, 3:4] * yb_ref[...]
    g2 = mod_ref[0, 5:6, :]
    o_ref[...] = _layer_norm(alpha * x1_ref[...] + g2 * f, lng_ref[...], lnb_ref[...])


def _moe_combine(ya, yb, route, x1, modp, ln_g, ln_b, t_len, alpha):
    n, d = x1.shape
    per_seq = t_len // ROW_BLOCK
    row = lambda i: (i, 0)
    return pl.pallas_call(
        functools.partial(_combine_kernel, alpha=alpha),
        grid=(n // ROW_BLOCK,),
        in_specs=[pl.BlockSpec((ROW_BLOCK, d), row), pl.BlockSpec((ROW_BLOCK, d), row),
                  pl.BlockSpec((ROW_BLOCK, V7X_LANES), row), pl.BlockSpec((ROW_BLOCK, d), row),
                  pl.BlockSpec((1, 6, d), lambda i: (i // per_seq, 0, 0)),
                  _const_spec(ln_g.shape), _const_spec(ln_b.shape)],
        out_specs=pl.BlockSpec((ROW_BLOCK, d), row),
        out_shape=jax.ShapeDtypeStruct((n, d), F32),
        name="moe_combine",
        compiler_params=_params("parallel"),
    )(ya, yb, route, x1, modp, ln_g, ln_b)


def _moe(h2, route, x1, modp, w1, w3, w2, ln_g, ln_b, t_len, alpha):
    n, d = x1.shape
    n_exp = w1.shape[0]
    top_e = route[:, 0:2].astype(jnp.int32)
    flat_e = top_e.reshape(-1)
    onehot = (flat_e[:, None] == jnp.arange(n_exp, dtype=jnp.int32)[None, :]).astype(jnp.int32)
    rank = jnp.sum((jnp.cumsum(onehot, axis=0) - onehot) * onehot, axis=1)
    counts = jnp.sum(onehot, axis=0)
    padded = (counts + MOE_ROWS - 1) // MOE_ROWS * MOE_ROWS
    pad_end = jnp.cumsum(padded)
    pad_start = pad_end - padded
    dest = pad_start[flat_e] + rank
    nblk = -(-(2 * n + n_exp * (MOE_ROWS - 1)) // MOE_ROWS)
    p = nblk * MOE_ROWS
    flat_tok = jnp.repeat(jnp.arange(n, dtype=jnp.int32), 2)
    buf_tok = jnp.full((p,), n, jnp.int32).at[dest].set(flat_tok)
    blk_start = jnp.arange(nblk, dtype=jnp.int32) * MOE_ROWS
    block_e = jnp.minimum(jnp.sum((pad_end[None, :] <= blk_start[:, None]).astype(jnp.int32), axis=1), n_exp - 1)
    n_used = (pad_end[-1] // MOE_ROWS).astype(jnp.int32).reshape(1)
    xb = jnp.concatenate([h2, jnp.zeros((1, d), h2.dtype)], axis=0)[buf_tok]
    yb = _moe_experts(block_e, n_used, xb, w1, w3, w2)
    dest2 = dest.reshape(n, 2)
    return _moe_combine(yb[dest2[:, 0]], yb[dest2[:, 1]], route, x1, modp, ln_g, ln_b, t_len, alpha)


def _layer(x, modp, lw, b_len, t_len, alpha, s0, ctx_kv):
    slab, qkv, gates = _in_proj(x, modp, lw["w_in"], t_len, lw["n_slab"], 3 * D_BRANCH)
    r, kk, lwf, lwb, bf, bb, ktf, ktb, v, bonus, g = _rwkv_prep(slab, lw, b_len, t_len)
    ytf, ytb, z_fin = _rwkv_scan(jnp.swapaxes(s0, -1, -2), r, kk, lwf, lwb, bf, bb, ktf, ktb, v, b_len, t_len)
    s_fin = jnp.swapaxes(z_fin, -1, -2)
    if ctx_kv is None:
        yn = _ctx_attention(qkv, b_len, t_len)
    else:
        yn = _nbr_attention(qkv, ctx_kv[0], ctx_kv[1], lw["bias"], b_len, t_len)
    outs = _mix_out(ytf, ytb, bonus, g, yn, gates, x, modp, lw, t_len, alpha, lw.get("router"))
    if "router" in lw:
        x1, h2, route = outs
        x2 = _moe(h2, route, x1, modp, lw["moe_w1"], lw["moe_w3"], lw["moe_w2"], lw["ln2_g"], lw["ln2_b"],
                  t_len, alpha)
    else:
        x1, h2 = outs
        x2 = _ffn_dense(h2, x1, modp, lw["ffn_w1"], lw["ffn_w3"], lw["ffn_w2"], lw["ln2_g"], lw["ln2_b"],
                        t_len, alpha)
    return x2, s_fin, qkv


def kernel(x_prompt, x_sample, c, state_rwkv, cache_k, cache_v, c_ctx, w_ada, b_ada, w_in, ts_w, rw_w0, rw_w2, rw_a0, rw_a2, rw_g2, rw_kk, rw_ka, rw_u, rw_gn_g, rw_gn_b, na_rpb, w_up_r, w_up_n, w_o, ln1_g, ln1_b, ln2_g, ln2_b, ffn_w1, ffn_w3, ffn_w2, moe_router, moe_w1, moe_w3, moe_w2):
    depth = w_in.shape[0]
    bc, tc_len, d = x_prompt.shape
    bs, ts_len, _ = x_sample.shape
    dr = D_BRANCH
    alpha = float((2 * depth) ** 0.25)
    n_slab = ts_w.shape[2]
    lora_w = rw_w2.shape[2]
    lora_a = rw_a2.shape[2]

    cvec = jnp.zeros((8, d), F32).at[0].set(c_ctx).at[1:1 + bs].set(c)
    mod = _ada_mod(cvec, w_ada, b_ada)

    half = np.kron(np.eye(2, dtype=np.float32), np.ones((HEAD_DIM, HEAD_DIM), np.float32))
    bd = jnp.asarray(half, BF16)
    rows = ts_len // GRID_W

    xp = x_prompt.reshape(bc * tc_len, d)
    xs = x_sample.reshape(bs * ts_len, d)
    zero_state = jnp.zeros((bc, 2, N_HEADS, HEAD_DIM, HEAD_DIM), F32)
    new_s, new_k, new_v = [], [], []
    for l in range(depth):
        w2blk = jnp.zeros((2 * lora_w, 2 * dr), F32).at[:lora_w, :dr].set(rw_w2[l, 0]).at[lora_w:, dr:].set(rw_w2[l, 1])
        a2blk = jnp.zeros((2 * lora_a, 2 * dr), F32).at[:lora_a, :dr].set(rw_a2[l, 0]).at[lora_a:, dr:].set(rw_a2[l, 1])
        lw = dict(
            n_slab=n_slab,
            w_in=w_in[l].astype(BF16), ts=ts_w[l],
            w0=rw_w0[l].reshape(1, 2 * dr), w2=w2blk.astype(BF16),
            a0=rw_a0[l].reshape(1, 2 * dr), a2=a2blk.astype(BF16),
            g2=rw_g2[l].astype(BF16), kkw=rw_kk[l].reshape(1, dr), ka=rw_ka[l].reshape(1, dr),
            u=rw_u[l].reshape(1, dr), bd=bd,
            gn_g=rw_gn_g[l].reshape(1, dr), gn_b=rw_gn_b[l].reshape(1, dr),
            w_up_r=w_up_r[l].astype(BF16), w_up_n=w_up_n[l].astype(BF16), w_o=w_o[l].astype(BF16),
            ln1_g=ln1_g[l].reshape(1, d), ln1_b=ln1_b[l].reshape(1, d),
            ln2_g=ln2_g[l].reshape(1, d), ln2_b=ln2_b[l].reshape(1, d),
            bias=_nbr_bias_table(na_rpb[l], rows),
        )
        if l % 2 == 0:
            lw.update(ffn_w1=ffn_w1[l // 2].astype(BF16), ffn_w3=ffn_w3[l // 2].astype(BF16),
                      ffn_w2=ffn_w2[l // 2].astype(BF16))
        else:
            lw.update(router=moe_router[l // 2], moe_w1=moe_w1[l // 2].astype(BF16),
                      moe_w3=moe_w3[l // 2].astype(BF16), moe_w2=moe_w2[l // 2].astype(BF16))

        mod_ctx = jnp.broadcast_to(mod[l, 0].reshape(1, 6, d), (bc, 6, d))
        mod_lat = mod[l, 1:1 + bs].reshape(bs, 6, d)

        xp, s_fin, qkv_c = _layer(xp, mod_ctx, lw, bc, tc_len, alpha, zero_state, None)
        new_s.append(s_fin)
        new_k.append(qkv_c[:, dr:2 * dr].reshape(bc, tc_len, N_HEADS, HEAD_DIM))
        new_v.append(qkv_c[:, 2 * dr:3 * dr].reshape(bc, tc_len, N_HEADS, HEAD_DIM))

        kv = (cache_k[:, l].reshape(bs, -1, dr), cache_v[:, l].reshape(bs, -1, dr))
        xs, _, _ = _layer(xs, mod_lat, lw, bs, ts_len, alpha, state_rwkv[:, l], kv)

    return (xp.reshape(bc, tc_len, d), xs.reshape(bs, ts_len, d),
            jnp.stack(new_s, axis=1), jnp.stack(new_k, axis=1), jnp.stack(new_v, axis=1))
```

```python
import functools

import numpy as np
import jax
import jax.numpy as jnp
from jax import lax
from jax.experimental import pallas as pl
from jax.experimental.pallas import tpu as pltpu

F32 = jnp.float32
BF16 = jnp.bfloat16
HIGHEST = lax.Precision.HIGHEST

N_HEADS = 8
HEAD_DIM = 64
D_BRANCH = N_HEADS * HEAD_DIM
V7X_LANES = 128
ROW_BLOCK = 256
SCAN_CHUNK = 64
SCAN_CHUNKS_PER_STEP = 2
MOE_ROWS = 256
KH_MAX = 8
KW = 16
GRID_W = 64
LN_EPS = 1e-5
GN_EPS = 64e-5
NEG_BIG = -1e30
VMEM_LIMIT = 56 * 1024 * 1024


def _params(*sem):
    return pltpu.CompilerParams(dimension_semantics=sem, vmem_limit_bytes=VMEM_LIMIT)


def _const_spec(shape):
    nd = len(shape)
    return pl.BlockSpec(shape, lambda *_: (0,) * nd, pipeline_mode=pl.Buffered(1))


def _dot(a, b):
    return jnp.dot(a, b, preferred_element_type=F32)


def _dot_nt(a, b):
    return lax.dot_general(a, b, (((1,), (1,)), ((), ())), preferred_element_type=F32)


def _dot_tn(a, b):
    return lax.dot_general(a, b, (((0,), (0,)), ((), ())), preferred_element_type=F32)


def _mm1(a, b, f=_dot):
    return f(a.astype(BF16), b.astype(BF16))


def _mm_exact_lhs(m, x):
    hi = x.astype(BF16)
    r1 = x - hi.astype(F32)
    mid = r1.astype(BF16)
    lo = (r1 - mid.astype(F32)).astype(BF16)
    return _dot(m, hi) + (_dot(m, mid) + _dot(m, lo))


def _sigmoid(x):
    return 1.0 / (1.0 + jnp.exp(-x))


def _head_sum(x, bd):
    outs = []
    for j in range(x.shape[1] // V7X_LANES):
        xs = x[:, j * V7X_LANES:(j + 1) * V7X_LANES]
        hi = xs.astype(BF16)
        lo = (xs - hi.astype(F32)).astype(BF16)
        outs.append(_dot(hi, bd) + _dot(lo, bd))
    return jnp.concatenate(outs, axis=1)


def _layer_norm(z, g, b):
    mu = jnp.mean(z, axis=-1, keepdims=True)
    zc = z - mu
    var = jnp.mean(zc * zc, axis=-1, keepdims=True)
    return zc * lax.rsqrt(var + LN_EPS) * g + b


def _ada_kernel(c_ref, w_ref, b_ref, o_ref):
    cs = c_ref[...]
    s = cs * _sigmoid(cs)
    o_ref[0] = jnp.dot(s, w_ref[0], preferred_element_type=F32, precision=HIGHEST) + b_ref[0]


def _ada_mod(cvec, w_ada, b_ada):
    depth, d, n6 = w_ada.shape
    tn = n6 // 4
    return pl.pallas_call(
        _ada_kernel,
        grid=(depth, n6 // tn),
        in_specs=[pl.BlockSpec((8, d), lambda l, j: (0, 0)),
                  pl.BlockSpec((1, d, tn), lambda l, j: (l, 0, j)),
                  pl.BlockSpec((1, 1, tn), lambda l, j: (l, 0, j))],
        out_specs=pl.BlockSpec((1, 8, tn), lambda l, j: (l, 0, j)),
        out_shape=jax.ShapeDtypeStruct((depth, 8, n6), F32),
        name="ada_mod",
        compiler_params=_params("parallel", "parallel"),
    )(cvec, w_ada, b_ada.reshape(depth, 1, n6))


def _inproj_kernel(x_ref, mod_ref, w_ref, slab_ref, q_ref, k_ref, v_ref, gate_ref, *, n_slab):
    sh = mod_ref[0, 0:1, :]
    sc = mod_ref[0, 1:2, :]
    h = (x_ref[...] * (1.0 + sc) + sh).astype(BF16)
    dn = D_BRANCH
    slab_ref[...] = _dot(h, w_ref[:, 0:n_slab])
    q_ref[...] = _dot(h, w_ref[:, n_slab:n_slab + dn]).astype(q_ref.dtype)
    k_ref[...] = _dot(h, w_ref[:, n_slab + dn:n_slab + 2 * dn]).astype(k_ref.dtype)
    v_ref[...] = _dot(h, w_ref[:, n_slab + 2 * dn:n_slab + 3 * dn]).astype(v_ref.dtype)
    gate_ref[...] = _sigmoid(_dot(h, w_ref[:, n_slab + 3 * dn:]))


def _in_proj(x, modp, w_in, t_len, n_slab, kv_dtype):
    n, d = x.shape
    n_cols = w_in.shape[1]
    dn = D_BRANCH
    n_gate = n_cols - n_slab - 3 * dn
    per_seq = t_len // ROW_BLOCK
    row = lambda i: (i, 0)
    return pl.pallas_call(
        functools.partial(_inproj_kernel, n_slab=n_slab),
        grid=(n // ROW_BLOCK,),
        in_specs=[pl.BlockSpec((ROW_BLOCK, d), row),
                  pl.BlockSpec((1, 6, d), lambda i: (i // per_seq, 0, 0)),
                  _const_spec((d, n_cols))],
        out_specs=[pl.BlockSpec((ROW_BLOCK, n_slab), row),
                   pl.BlockSpec((ROW_BLOCK, dn), row), pl.BlockSpec((ROW_BLOCK, dn), row),
                   pl.BlockSpec((ROW_BLOCK, dn), row),
                   pl.BlockSpec((ROW_BLOCK, n_gate), row)],
        out_shape=[jax.ShapeDtypeStruct((n, n_slab), F32),
                   jax.ShapeDtypeStruct((n, dn), BF16), jax.ShapeDtypeStruct((n, dn), kv_dtype),
                   jax.ShapeDtypeStruct((n, dn), kv_dtype),
                   jax.ShapeDtypeStruct((n, n_gate), F32)],
        name="in_proj",
        compiler_params=_params("parallel"),
    )(x, modp, w_in)


def _prep_kernel(cur_ref, prev_ref, next_ref, ts_ref, w0_ref, w2_ref, a0_ref, a2_ref, g2_ref,
                 kkw_ref, ka_ref, u_ref, bd_ref,
                 r_o, kk_o, lwf_o, lwb_o, bf_o, bb_o, ktf_o, ktb_o, v_o, bonus_o, g_o,
                 *, per_seq, lora_w, lora_a):
    i = pl.program_id(0)
    tb, n_slab = cur_ref.shape
    dr = D_BRANCH
    cur = cur_ref[...]
    first = (i % per_seq) == 0
    last = (i % per_seq) == per_seq - 1
    prow = jnp.where(first, 0.0, prev_ref[7:8, :])
    nrow = jnp.where(last, 0.0, next_ref[0:1, :])
    rid = lax.broadcasted_iota(jnp.int32, (tb, n_slab), 0)
    xp = jnp.where(rid == 0, prow, pltpu.roll(cur, 1, 0))
    xn = jnp.where(rid == tb - 1, nrow, pltpu.roll(cur, tb - 1, 0))
    s = ts_ref[0:1, :] * xp + ts_ref[1:2, :] * cur + ts_ref[2:3, :] * xn

    r = s[:, 0:dr]
    k = s[:, dr:2 * dr]
    v = s[:, 2 * dr:3 * dr]
    o = 3 * dr
    wd = s[:, o:o + 2 * lora_w]
    ad = s[:, o + 2 * lora_w:o + 2 * lora_w + 2 * lora_a]
    gd = s[:, o + 2 * lora_w + 2 * lora_a:]

    wl = w0_ref[...] + _dot(jnp.tanh(wd).astype(BF16), w2_ref[...])
    logw = -float(np.exp(-0.5)) * _sigmoid(wl)
    a = _sigmoid(a0_ref[...] + _dot(ad.astype(BF16), a2_ref[...]))
    g = _dot(_sigmoid(gd).astype(BF16), g2_ref[...])

    bd = bd_ref[...]
    kk = k * kkw_ref[...]
    kk = kk * lax.rsqrt(_head_sum(kk * kk, bd) + 1e-12)
    ka = ka_ref[...]
    a_f = a[:, :dr]
    a_b = a[:, dr:]
    kt_f = k * (1.0 + (a_f - 1.0) * ka)
    kt_b = k * (1.0 + (a_b - 1.0) * ka)
    bonus = _head_sum(r * (0.5 * (kt_f + kt_b)) * u_ref[...], bd) * v

    r_o[...] = r
    kk_o[...] = kk
    lwf_o[...] = logw[:, :dr]
    lwb_o[...] = logw[:, dr:]
    bf_o[...] = kk * a_f
    bb_o[...] = kk * a_b
    ktf_o[...] = kt_f
    ktb_o[...] = kt_b
    v_o[...] = v
    bonus_o[...] = bonus
    g_o[...] = g


def _rwkv_prep(slab, lw, b_len, t_len):
    n, n_slab = slab.shape
    dr = D_BRANCH
    per_seq = t_len // ROW_BLOCK
    sub = ROW_BLOCK // 8
    lora_w = lw["w2"].shape[0] // 2
    lora_a = lw["a2"].shape[0] // 2
    row = lambda i: (i, 0)
    consts = [lw["ts"], lw["w0"], lw["w2"], lw["a0"], lw["a2"], lw["g2"], lw["kkw"], lw["ka"], lw["u"], lw["bd"]]
    outs = pl.pallas_call(
        functools.partial(_prep_kernel, per_seq=per_seq, lora_w=lora_w, lora_a=lora_a),
        grid=(n // ROW_BLOCK,),
        in_specs=[pl.BlockSpec((ROW_BLOCK, n_slab), row),
                  pl.BlockSpec((8, n_slab), lambda i: (jnp.maximum(i * sub - 1, 0), 0)),
                  pl.BlockSpec((8, n_slab), lambda i: (jnp.minimum((i + 1) * sub, n // 8 - 1), 0))]
                 + [_const_spec(c.shape) for c in consts],
        out_specs=[pl.BlockSpec((ROW_BLOCK, dr), row)] * 11,
        out_shape=[jax.ShapeDtypeStruct((n, dr), F32)] * 11,
        name="rwkv_prep",
        compiler_params=_params("parallel"),
    )(slab, slab, slab, *consts)
    return outs


def _tri_masks(rev):
    c = SCAN_CHUNK
    ti = lax.broadcasted_iota(jnp.int32, (c, 2 * c), 0)
    ii = lax.broadcasted_iota(jnp.int32, (c, 2 * c), 1) % c
    if rev:
        ti, ii = c - 1 - ti, c - 1 - ii
    one = lambda m: jnp.where(m, 1.0, 0.0)
    levels = []
    s = 1
    while s < c:
        same = (ti // (2 * s)) == (ii // (2 * s))
        levels.append(one(same & ((ti % (2 * s)) >= s) & ((ii % (2 * s)) < s)))
        s *= 2
    return one(ii <= ti), one(ii < ti), levels


def _scan_kernel(z0_ref, rf, kkf, lwf, bf, ktf, vf, rb, kkb, lwb, bb, ktb, vb, yf_o, yb_o, zfin_o, z_scr):
    j = pl.program_id(1)
    nj = pl.num_programs(1)
    c = SCAN_CHUNK
    lanes = V7X_LANES
    npair = N_HEADS // 2
    assert c == HEAD_DIM and 2 * HEAD_DIM == lanes

    row_blk = lax.broadcasted_iota(jnp.int32, (lanes, lanes), 0) // HEAD_DIM
    col_blk = lax.broadcasted_iota(jnp.int32, (lanes, lanes), 1) // HEAD_DIM
    bd_mask = jnp.where(row_blk == col_blk, 1.0, 0.0)
    eye_full = jnp.where(lax.broadcasted_iota(jnp.int32, (lanes, lanes), 0)
                         == lax.broadcasted_iota(jnp.int32, (lanes, lanes), 1), 1.0, 0.0)
    lane_c = lax.broadcasted_iota(jnp.int32, (c, lanes), 1)
    lo = jnp.where(lane_c < HEAD_DIM, 1.0, 0.0)
    hi = 1.0 - lo
    eye_pair = jnp.where(lax.broadcasted_iota(jnp.int32, (c, lanes), 0) == lane_c % HEAD_DIM, 1.0, 0.0)

    def bd(x):
        return jnp.concatenate([x * lo, x * hi], axis=0)

    @pl.when(j == 0)
    def _():
        for d in range(2):
            for p in range(npair):
                za = jnp.concatenate([z0_ref[0, d, 2 * p], z0_ref[0, d, 2 * p + 1]], axis=1)
                z_scr[d * npair + p] = bd(za)

    refs = ((rf, kkf, lwf, bf, ktf, vf, yf_o), (rb, kkb, lwb, bb, ktb, vb, yb_o))
    masks = (_tri_masks(False), _tri_masks(True))
    cum_lhs = [jnp.concatenate([m[0][:, 0:c].astype(BF16), jnp.ones((c, c), BF16)], axis=0) for m in masks]
    nck = rf.shape[0] // c
    units = [(d, p, ck) for d in range(2) for p in range(npair) for ck in range(nck)]

    def tile(d, p, ck):
        row0 = (nck - 1 - ck if d else ck) * c
        return slice(row0, row0 + c), slice(p * lanes, (p + 1) * lanes)

    def per(fn):
        return [fn(i, d, tile(d, p, ck)) for i, (d, p, ck) in enumerate(units)]

    lw = per(lambda i, d, sl: refs[d][2][sl])
    cs = per(lambda i, d, sl: _mm_exact_lhs(cum_lhs[d], lw[i]))
    cum = [x[0:c] for x in cs]
    tot = [x[c:] for x in cs]
    r_s = per(lambda i, d, sl: refs[d][0][sl] * jnp.exp(cum[i]))
    k_s = per(lambda i, d, sl: refs[d][1][sl] * jnp.exp(cum[i] - lw[i]))
    e_neg = per(lambda i, d, sl: jnp.exp(-cum[i]))
    e_end = per(lambda i, d, sl: jnp.exp(tot[i] - cum[i]))
    b = per(lambda i, d, sl: refs[d][3][sl])
    kt = per(lambda i, d, sl: refs[d][4][sl])
    v = per(lambda i, d, sl: refs[d][5][sl])
    v_bd = per(lambda i, d, sl: bd(v[i]))
    kr = per(lambda i, d, sl: jnp.concatenate([k_s[i], r_s[i]], axis=0))
    ab = per(lambda i, d, sl: _mm1(kr[i], bd(b[i] * e_neg[i]), _dot_nt))
    ak = per(lambda i, d, sl: _mm1(kr[i], bd(kt[i] * e_neg[i]), _dot_nt))
    a_b = per(lambda i, d, sl: ab[i][0:c] * masks[d][1])
    b_b = per(lambda i, d, sl: ab[i][c:] * masks[d][0])
    a_kt = per(lambda i, d, sl: ak[i][0:c] * masks[d][1])
    b_kt = per(lambda i, d, sl: ak[i][c:] * masks[d][0])
    dm = per(lambda i, d, sl: eye_pair - a_b[i] * masks[d][2][0])
    for lvl in range(1, len(masks[0][2])):
        zd = per(lambda i, d, sl: _mm1(a_b[i] * masks[d][2][lvl], bd(dm[i])))
        dm = per(lambda i, d, sl: dm[i] - _mm1(dm[i], bd(zd[i])))
    w1 = per(lambda i, d, sl: _mm1(a_kt[i], v_bd[i]))
    g = per(lambda i, d, sl: _mm1(dm[i], jnp.concatenate([bd(k_s[i]), bd(w1[i])], axis=1)))
    bg = per(lambda i, d, sl: _mm1(b_b[i], jnp.concatenate([bd(g[i][:, 0:lanes]), bd(g[i][:, lanes:])], axis=1)))
    q = per(lambda i, d, sl: r_s[i] - bg[i][:, 0:lanes])
    y0 = per(lambda i, d, sl: _mm1(b_kt[i], v_bd[i]) - bg[i][:, lanes:])
    cg = per(lambda i, d, sl: _mm1(b[i] * e_end[i], g[i], _dot_tn))
    phi = per(lambda i, d, sl: eye_full * jnp.exp(jnp.concatenate([tot[i], tot[i]], axis=0))
              - cg[i][:, 0:lanes] * bd_mask)
    psi = per(lambda i, d, sl: (_mm1(kt[i] * e_end[i], v[i], _dot_tn) - cg[i][:, lanes:]) * bd_mask)
    z = [z_scr[s] for s in range(2 * npair)]
    for ck in range(nck):
        sel = [(i, d, p) for i, (d, p, k2) in enumerate(units) if k2 == ck]
        y = [_mm1(q[i], z[d * npair + p]) + y0[i] for i, d, p in sel]
        z = [_mm1(phi[i], z[d * npair + p]) + psi[i] for i, d, p in sel]
        for yy, (i, d, p) in zip(y, sel):
            refs[d][6][tile(d, p, ck)] = yy
    for s in range(2 * npair):
        z_scr[s] = z[s]

    @pl.when(j == nj - 1)
    def _():
        for d in range(2):
            for p in range(npair):
                zz = z_scr[d * npair + p]
                zfin_o[0, d, 2 * p] = zz[0:HEAD_DIM, 0:HEAD_DIM]
                zfin_o[0, d, 2 * p + 1] = zz[HEAD_DIM:, HEAD_DIM:]


def _rwkv_scan(z0, r, kk, lwf, lwb, bf, bb, ktf, ktb, v, b_len, t_len):
    blk = SCAN_CHUNK * SCAN_CHUNKS_PER_STEP
    nj = t_len // blk
    n = b_len * t_len
    dr = D_BRANCH
    fwd = pl.BlockSpec((blk, dr), lambda b, j: (b * nj + j, 0))
    bwd = pl.BlockSpec((blk, dr), lambda b, j: (b * nj + nj - 1 - j, 0))
    st = pl.BlockSpec((1, 2, N_HEADS, HEAD_DIM, HEAD_DIM), lambda b, j: (b, 0, 0, 0, 0))
    return pl.pallas_call(
        _scan_kernel,
        grid=(b_len, nj),
        in_specs=[st] + [fwd] * 6 + [bwd] * 6,
        out_specs=[fwd, bwd, st],
        out_shape=[jax.ShapeDtypeStruct((n, dr), F32),
                   jax.ShapeDtypeStruct((n, dr), F32),
                   jax.ShapeDtypeStruct((b_len, 2, N_HEADS, HEAD_DIM, HEAD_DIM), F32)],
        scratch_shapes=[pltpu.VMEM((N_HEADS, V7X_LANES, V7X_LANES), F32)],
        name="rwkv_scan",
        compiler_params=_params("parallel", "arbitrary"),
    )(z0, r, kk, lwf, bf, ktf, v, r, kk, lwb, bb, ktb, v)


def _softmax_rows(parts):
    m = parts[0].max(axis=-1, keepdims=True)
    for p in parts[1:]:
        m = jnp.maximum(m, p.max(axis=-1, keepdims=True))
    es = [jnp.exp(p - m) for p in parts]
    den = es[0].sum(axis=-1, keepdims=True)
    for e in es[1:]:
        den = den + e.sum(axis=-1, keepdims=True)
    inv = 1.0 / den
    return [e * inv for e in es]


def _ctx_attn_kernel(q_ref, k_ref, v_ref, o_ref):
    scale = HEAD_DIM ** -0.5
    npair = q_ref.shape[1] // V7X_LANES
    tile = lambda ref, p: ref[:, p * V7X_LANES:(p + 1) * V7X_LANES].astype(BF16)
    qb = [tile(q_ref, p) for p in range(npair)]
    k = [tile(k_ref, p) for p in range(npair)]
    v = [tile(v_ref, p) for p in range(npair)]
    first_head = lax.broadcasted_iota(jnp.int32, qb[0].shape, 1) < HEAD_DIM
    head_lanes = (first_head, jnp.logical_not(first_head))
    units = [(p, hh) for p in range(npair) for hh in range(2)]

    def per(fn):
        return [fn(u, p, hh) for u, (p, hh) in enumerate(units)]

    q = per(lambda u, p, hh: jnp.where(head_lanes[hh], qb[p], jnp.zeros_like(qb[p])))
    s = per(lambda u, p, hh: _dot_nt(q[u], k[p]) * scale)
    e = per(lambda u, p, hh: jnp.exp(s[u] - s[u].max(axis=-1, keepdims=True)))
    pr = per(lambda u, p, hh: (e[u] * (1.0 / e[u].sum(axis=-1, keepdims=True))).astype(BF16))
    out = per(lambda u, p, hh: _dot(pr[u], v[p]))
    for p in range(npair):
        o_ref[:, p * V7X_LANES:(p + 1) * V7X_LANES] = jnp.where(first_head, out[2 * p], out[2 * p + 1]).astype(o_ref.dtype)


def _ctx_attention(q, k, v, b_len, t_len):
    n = q.shape[0]
    seq = pl.BlockSpec((t_len, D_BRANCH), lambda b: (b, 0))
    return pl.pallas_call(
        _ctx_attn_kernel,
        grid=(b_len,),
        in_specs=[seq, seq, seq],
        out_specs=seq,
        out_shape=jax.ShapeDtypeStruct((n, D_BRANCH), BF16),
        name="ctx_attn",
        compiler_params=_params("parallel"),
    )(q, k, v)


def _nbr_attn_kernel(q_ref, k_ref, v_ref, kc_ref, vc_ref, bias_ref, o_ref, *, rows, kh):
    scale = HEAD_DIM ** -0.5
    win = kh * GRID_W

    group = 4 if rows % 4 == 0 else 1
    units = [(r, hh) for r in range(group) for hh in range(2)]
    first_head = lax.broadcasted_iota(jnp.int32, (GRID_W, V7X_LANES), 1) < HEAD_DIM
    head_lanes = (first_head, jnp.logical_not(first_head))

    def body(gi, carry):
        def per(fn):
            return [fn(u, r, hh) for u, (r, hh) in enumerate(units)]

        i = [gi * group + r for r in range(group)]
        rs = [jnp.clip(x - kh // 2, 0, rows - kh) for x in i]
        q0 = [pl.multiple_of(x * GRID_W, GRID_W) for x in i]
        k0 = [pl.multiple_of(x * GRID_W, GRID_W) for x in rs]
        qb = [q_ref[pl.ds(x, GRID_W), :].astype(BF16) for x in q0]
        kw = [k_ref[pl.ds(x, win), :].astype(BF16) for x in k0]
        vw = [v_ref[pl.ds(x, win), :].astype(BF16) for x in k0]
        kc = kc_ref[0].astype(BF16)
        vc = vc_ref[0].astype(BF16)
        q = per(lambda u, r, hh: jnp.where(head_lanes[hh], qb[r], jnp.zeros_like(qb[r])))
        s_loc = per(lambda u, r, hh: _dot_nt(q[u], kw[r]) * scale + bias_ref[hh, i[r] - rs[r]])
        s_ctx = per(lambda u, r, hh: _dot_nt(q[u], kc) * scale)
        m = per(lambda u, r, hh: jnp.maximum(s_loc[u].max(axis=-1, keepdims=True), s_ctx[u].max(axis=-1, keepdims=True)))
        e_loc = per(lambda u, r, hh: jnp.exp(s_loc[u] - m[u]))
        e_ctx = per(lambda u, r, hh: jnp.exp(s_ctx[u] - m[u]))
        inv = per(lambda u, r, hh: 1.0 / (e_loc[u].sum(axis=-1, keepdims=True) + e_ctx[u].sum(axis=-1, keepdims=True)))
        out = per(lambda u, r, hh: _dot((e_loc[u] * inv[u]).astype(BF16), vw[r])
                  + _dot((e_ctx[u] * inv[u]).astype(BF16), vc))
        for r in range(group):
            o_ref[pl.ds(q0[r], GRID_W), :] = jnp.where(first_head, out[2 * r], out[2 * r + 1]).astype(o_ref.dtype)
        return carry

    lax.fori_loop(0, rows // group, body, 0)


def _nbr_attention(q, k, v, k_ctx, v_ctx, bias, b_len, t_len):
    n = q.shape[0]
    npair = N_HEADS // 2
    rows = t_len // GRID_W
    kh = min(KH_MAX, rows)
    c_len = k_ctx.shape[1]
    pair = pl.BlockSpec((t_len, V7X_LANES), lambda b, p: (b, p))
    return pl.pallas_call(
        functools.partial(_nbr_attn_kernel, rows=rows, kh=kh),
        grid=(b_len, npair),
        in_specs=[pair, pair, pair,
                  pl.BlockSpec((1, c_len, V7X_LANES), lambda b, p: (b, 0, p)),
                  pl.BlockSpec((1, c_len, V7X_LANES), lambda b, p: (b, 0, p)),
                  pl.BlockSpec((2, kh, GRID_W, kh * GRID_W), lambda b, p: (p, 0, 0, 0))],
        out_specs=pair,
        out_shape=jax.ShapeDtypeStruct((n, D_BRANCH), BF16),
        name="nbr_attn",
        compiler_params=_params("parallel", "parallel"),
    )(q, k, v, k_ctx, v_ctx, bias)


def _nbr_bias_table(rpb, rows):
    kh = min(KH_MAX, rows)
    w = np.arange(GRID_W)
    col_start = np.clip(w - KW // 2, 0, GRID_W - KW)
    c = np.arange(GRID_W)
    inside = (c[None, :] >= col_start[:, None]) & (c[None, :] < col_start[:, None] + KW)
    rp = jnp.pad(rpb, ((0, 0), (0, 0), (GRID_W - KW, GRID_W - KW)))
    toep = jnp.stack([rp[:, :, GRID_W - 1 - wi:2 * GRID_W - 1 - wi] for wi in range(GRID_W)], axis=2)
    toep = jnp.where(inside[None, None], toep, NEG_BIG)
    tab = jnp.stack([toep[:, KH_MAX - 1 - dd:KH_MAX - 1 - dd + kh] for dd in range(kh)], axis=1)
    tab = jnp.transpose(tab, (0, 1, 3, 2, 4))
    return tab.reshape(rpb.shape[0], kh, GRID_W, kh * GRID_W)


def _top2_route(logits, n_exp):
    lane = lax.broadcasted_iota(jnp.int32, logits.shape, 1)
    lane_f = lane.astype(F32)
    lg = jnp.where(lane < n_exp, logits, -jnp.inf)
    m1 = lg.max(axis=-1, keepdims=True)
    i1 = jnp.where(lg == m1, lane_f, float(V7X_LANES)).min(axis=-1, keepdims=True)
    lg2 = jnp.where(lane_f == i1, -jnp.inf, lg)
    m2 = lg2.max(axis=-1, keepdims=True)
    i2 = jnp.where(lg2 == m2, lane_f, float(V7X_LANES)).min(axis=-1, keepdims=True)
    e = jnp.exp(m2 - m1)
    g1 = 1.0 / (1.0 + e)
    g2 = e * g1
    out = jnp.where(lane == 0, i1, 0.0)
    out = jnp.where(lane == 1, i2, out)
    out = jnp.where(lane == 2, g1, out)
    return jnp.where(lane == 3, g2, out)


def _mix_kernel(yf_ref, yb_ref, bonus_ref, g_ref, yn_ref, gate_ref, x_ref, mod_ref,
                gng_ref, gnb_ref, bd_ref, wur_ref, wun_ref, wo_ref, lng_ref, lnb_ref, *rest,
                alpha, n_exp):
    if n_exp:
        router_ref, x1_o, h2_o, route_o = rest
    else:
        x1_o, h2_o = rest
    d = x_ref.shape[1]
    bd = bd_ref[...]
    y = yf_ref[...] + yb_ref[...]
    inv_n = 1.0 / HEAD_DIM
    mu = _head_sum(y, bd) * inv_n
    yc = y - mu
    var = _head_sum(yc * yc, bd) * inv_n
    yr = yc * lax.rsqrt(var + GN_EPS) * gng_ref[...] + gnb_ref[...]
    yr = (yr + bonus_ref[...]) * g_ref[...]
    m = (gate_ref[:, 0:d] * _dot(yr.astype(BF16), wur_ref[...])
         + gate_ref[:, d:] * _dot(yn_ref[...], wun_ref[...]))
    o = _dot(m.astype(BF16), wo_ref[...])
    g1 = mod_ref[0, 2:3, :]
    sh2 = mod_ref[0, 3:4, :]
    sc2 = mod_ref[0, 4:5, :]
    x1 = _layer_norm(alpha * x_ref[...] + g1 * o, lng_ref[...], lnb_ref[...])
    x1_o[...] = x1
    h2 = x1 * (1.0 + sc2) + sh2
    h2_o[...] = h2.astype(BF16)
    if n_exp:
        logits = jnp.dot(h2, router_ref[...], preferred_element_type=F32, precision=HIGHEST)
        route_o[...] = _top2_route(logits, n_exp)


def _mix_out(yf, yb, bonus, g, yn, gates, x, modp, lw, t_len, alpha, router):
    n, d = x.shape
    dr = D_BRANCH
    per_seq = t_len // ROW_BLOCK
    row = lambda i: (i, 0)
    yt_spec = pl.BlockSpec((ROW_BLOCK, dr), row)
    consts = [lw["gn_g"], lw["gn_b"], lw["bd"], lw["w_up_r"], lw["w_up_n"], lw["w_o"], lw["ln1_g"], lw["ln1_b"]]
    n_exp = 0
    out_specs = [pl.BlockSpec((ROW_BLOCK, d), row), pl.BlockSpec((ROW_BLOCK, d), row)]
    out_shape = [jax.ShapeDtypeStruct((n, d), F32), jax.ShapeDtypeStruct((n, d), BF16)]
    if router is not None:
        n_exp = router.shape[1]
        consts.append(jnp.pad(router, ((0, 0), (0, V7X_LANES - n_exp))))
        out_specs.append(pl.BlockSpec((ROW_BLOCK, V7X_LANES), row))
        out_shape.append(jax.ShapeDtypeStruct((n, V7X_LANES), F32))
    return pl.pallas_call(
        functools.partial(_mix_kernel, alpha=alpha, n_exp=n_exp),
        grid=(n // ROW_BLOCK,),
        in_specs=[yt_spec, yt_spec,
                  pl.BlockSpec((ROW_BLOCK, dr), row), pl.BlockSpec((ROW_BLOCK, dr), row),
                  pl.BlockSpec((ROW_BLOCK, dr), row), pl.BlockSpec((ROW_BLOCK, 2 * d), row),
                  pl.BlockSpec((ROW_BLOCK, d), row),
                  pl.BlockSpec((1, 6, d), lambda i: (i // per_seq, 0, 0))]
                 + [_const_spec(c.shape) for c in consts],
        out_specs=out_specs,
        out_shape=out_shape,
        name="mix_out",
        compiler_params=_params("parallel"),
    )(yf, yb, bonus, g, yn, gates, x, modp, *consts)


def _swiglu(h, w1, w3, w2):
    u = _dot(h, w1)
    act = u * _sigmoid(u) * _dot(h, w3)
    return _dot(act.astype(BF16), w2)


def _ffn_kernel(h_ref, x1_ref, mod_ref, w1_ref, w3_ref, w2_ref, lng_ref, lnb_ref, o_ref, *, alpha):
    f = _swiglu(h_ref[...], w1_ref[...], w3_ref[...], w2_ref[...])
    g2 = mod_ref[0, 5:6, :]
    o_ref[...] = _layer_norm(alpha * x1_ref[...] + g2 * f, lng_ref[...], lnb_ref[...])


def _ffn_dense(h2, x1, modp, w1, w3, w2, ln_g, ln_b, t_len, alpha):
    n, d = x1.shape
    per_seq = t_len // ROW_BLOCK
    row = lambda i: (i, 0)
    consts = [w1, w3, w2, ln_g, ln_b]
    return pl.pallas_call(
        functools.partial(_ffn_kernel, alpha=alpha),
        grid=(n // ROW_BLOCK,),
        in_specs=[pl.BlockSpec((ROW_BLOCK, d), row), pl.BlockSpec((ROW_BLOCK, d), row),
                  pl.BlockSpec((1, 6, d), lambda i: (i // per_seq, 0, 0))]
                 + [_const_spec(c.shape) for c in consts],
        out_specs=pl.BlockSpec((ROW_BLOCK, d), row),
        out_shape=jax.ShapeDtypeStruct((n, d), F32),
        name="ffn_dense",
        compiler_params=_params("parallel"),
    )(h2, x1, modp, *consts)


def _moe_kernel(be_ref, nb_ref, x_ref, w1_ref, w3_ref, w2_ref, o_ref):
    i = pl.program_id(0)

    @pl.when(i < nb_ref[0])
    def _():
        o_ref[...] = _swiglu(x_ref[...], w1_ref[0], w3_ref[0], w2_ref[0])

    @pl.when(i >= nb_ref[0])
    def _():
        o_ref[...] = jnp.zeros(o_ref.shape, F32)


def _moe_experts(block_e, n_used, xb, w1, w3, w2):
    p, d = xb.shape
    _, _, dff = w1.shape
    nblk = p // MOE_ROWS
    one = pl.Buffered(1)
    grid_spec = pltpu.PrefetchScalarGridSpec(
        num_scalar_prefetch=2,
        grid=(nblk,),
        in_specs=[pl.BlockSpec((MOE_ROWS, d), lambda i, be, nb: (i, 0)),
                  pl.BlockSpec((1, d, dff), lambda i, be, nb: (be[i], 0, 0), pipeline_mode=one),
                  pl.BlockSpec((1, d, dff), lambda i, be, nb: (be[i], 0, 0), pipeline_mode=one),
                  pl.BlockSpec((1, dff, d), lambda i, be, nb: (be[i], 0, 0), pipeline_mode=one)],
        out_specs=pl.BlockSpec((MOE_ROWS, d), lambda i, be, nb: (i, 0)),
    )
    return pl.pallas_call(
        _moe_kernel,
        grid_spec=grid_spec,
        out_shape=jax.ShapeDtypeStruct((p, d), F32),
        name="moe_experts",
        compiler_params=_params("arbitrary"),
    )(block_e, n_used, xb, w1, w3, w2)


def _combine_kernel(ya_ref, yb_ref, route_ref, x1_ref, mod_ref, lng_ref, lnb_ref, o_ref, *, alpha):
    f = route_ref[:, 2:3] * ya_ref[...] + route_ref[:, 3:4] * yb_ref[...]
    g2 = mod_ref[0, 5:6, :]
    o_ref[...] = _layer_norm(alpha * x1_ref[...] + g2 * f, lng_ref[...], lnb_ref[...])


def _moe_combine(ya, yb, route, x1, modp, ln_g, ln_b, t_len, alpha):
    n, d = x1.shape
    per_seq = t_len // ROW_BLOCK
    row = lambda i: (i, 0)
    return pl.pallas_call(
        functools.partial(_combine_kernel, alpha=alpha),
        grid=(n // ROW_BLOCK,),
        in_specs=[pl.BlockSpec((ROW_BLOCK, d), row), pl.BlockSpec((ROW_BLOCK, d), row),
                  pl.BlockSpec((ROW_BLOCK, V7X_LANES), row), pl.BlockSpec((ROW_BLOCK, d), row),
                  pl.BlockSpec((1, 6, d), lambda i: (i // per_seq, 0, 0)),
                  _const_spec(ln_g.shape), _const_spec(ln_b.shape)],
        out_specs=pl.BlockSpec((ROW_BLOCK, d), row),
        out_shape=jax.ShapeDtypeStruct((n, d), F32),
        name="moe_combine",
        compiler_params=_params("parallel"),
    )(ya, yb, route, x1, modp, ln_g, ln_b)


def _moe(h2, route, x1, modp, w1, w3, w2, ln_g, ln_b, t_len, alpha):
    n, d = x1.shape
    n_exp = w1.shape[0]
    top_e = route[:, 0:2].astype(jnp.int32)
    flat_e = top_e.reshape(-1)
    onehot = (flat_e[:, None] == jnp.arange(n_exp, dtype=jnp.int32)[None, :]).astype(jnp.int32)
    rank = jnp.sum((jnp.cumsum(onehot, axis=0) - onehot) * onehot, axis=1)
    counts = jnp.sum(onehot, axis=0)
    padded = (counts + MOE_ROWS - 1) // MOE_ROWS * MOE_ROWS
    pad_end = jnp.cumsum(padded)
    pad_start = pad_end - padded
    dest = pad_start[flat_e] + rank
    nblk = -(-(2 * n + n_exp * (MOE_ROWS - 1)) // MOE_ROWS)
    p = nblk * MOE_ROWS
    flat_tok = jnp.repeat(jnp.arange(n, dtype=jnp.int32), 2)
    buf_tok = jnp.full((p,), n, jnp.int32).at[dest].set(flat_tok)
    blk_start = jnp.arange(nblk, dtype=jnp.int32) * MOE_ROWS
    block_e = jnp.minimum(jnp.sum((pad_end[None, :] <= blk_start[:, None]).astype(jnp.int32), axis=1), n_exp - 1)
    n_used = (pad_end[-1] // MOE_ROWS).astype(jnp.int32).reshape(1)
    xb = jnp.concatenate([h2, jnp.zeros((1, d), h2.dtype)], axis=0)[buf_tok]
    yb = _moe_experts(block_e, n_used, xb, w1, w3, w2)
    dest2 = dest.reshape(n, 2)
    return _moe_combine(yb[dest2[:, 0]], yb[dest2[:, 1]], route, x1, modp, ln_g, ln_b, t_len, alpha)


def _layer(x, modp, lw, b_len, t_len, alpha, s0, ctx_kv):
    slab, qn, kn, vn, gates = _in_proj(x, modp, lw["w_in"], t_len, lw["n_slab"], F32 if ctx_kv is None else BF16)
    r, kk, lwf, lwb, bf, bb, ktf, ktb, v, bonus, g = _rwkv_prep(slab, lw, b_len, t_len)
    ytf, ytb, z_fin = _rwkv_scan(jnp.swapaxes(s0, -1, -2), r, kk, lwf, lwb, bf, bb, ktf, ktb, v, b_len, t_len)
    s_fin = jnp.swapaxes(z_fin, -1, -2)
    if ctx_kv is None:
        yn = _ctx_attention(qn, kn, vn, b_len, t_len)
    else:
        yn = _nbr_attention(qn, kn, vn, ctx_kv[0], ctx_kv[1], lw["bias"], b_len, t_len)
    outs = _mix_out(ytf, ytb, bonus, g, yn, gates, x, modp, lw, t_len, alpha, lw.get("router"))
    if "router" in lw:
        x1, h2, route = outs
        x2 = _moe(h2, route, x1, modp, lw["moe_w1"], lw["moe_w3"], lw["moe_w2"], lw["ln2_g"], lw["ln2_b"],
                  t_len, alpha)
    else:
        x1, h2 = outs
        x2 = _ffn_dense(h2, x1, modp, lw["ffn_w1"], lw["ffn_w3"], lw["ffn_w2"], lw["ln2_g"], lw["ln2_b"],
                        t_len, alpha)
    return x2, s_fin, kn, vn


def kernel(x_prompt, x_sample, c, state_rwkv, cache_k, cache_v, c_ctx, w_ada, b_ada, w_in, ts_w, rw_w0, rw_w2, rw_a0, rw_a2, rw_g2, rw_kk, rw_ka, rw_u, rw_gn_g, rw_gn_b, na_rpb, w_up_r, w_up_n, w_o, ln1_g, ln1_b, ln2_g, ln2_b, ffn_w1, ffn_w3, ffn_w2, moe_router, moe_w1, moe_w3, moe_w2):
    depth = w_in.shape[0]
    bc, tc_len, d = x_prompt.shape
    bs, ts_len, _ = x_sample.shape
    dr = D_BRANCH
    alpha = float((2 * depth) ** 0.25)
    n_slab = ts_w.shape[2]
    lora_w = rw_w2.shape[2]
    lora_a = rw_a2.shape[2]

    cvec = jnp.zeros((8, d), F32).at[0].set(c_ctx).at[1:1 + bs].set(c)
    mod = _ada_mod(cvec, w_ada, b_ada)

    half = np.kron(np.eye(2, dtype=np.float32), np.ones((HEAD_DIM, HEAD_DIM), np.float32))
    bd = jnp.asarray(half, BF16)
    rows = ts_len // GRID_W

    xp = x_prompt.reshape(bc * tc_len, d)
    xs = x_sample.reshape(bs * ts_len, d)
    zero_state = jnp.zeros((bc, 2, N_HEADS, HEAD_DIM, HEAD_DIM), F32)
    new_s, new_k, new_v = [], [], []
    for l in range(depth):
        w2blk = jnp.zeros((2 * lora_w, 2 * dr), F32).at[:lora_w, :dr].set(rw_w2[l, 0]).at[lora_w:, dr:].set(rw_w2[l, 1])
        a2blk = jnp.zeros((2 * lora_a, 2 * dr), F32).at[:lora_a, :dr].set(rw_a2[l, 0]).at[lora_a:, dr:].set(rw_a2[l, 1])
        lw = dict(
            n_slab=n_slab,
            w_in=w_in[l].astype(BF16), ts=ts_w[l],
            w0=rw_w0[l].reshape(1, 2 * dr), w2=w2blk.astype(BF16),
            a0=rw_a0[l].reshape(1, 2 * dr), a2=a2blk.astype(BF16),
            g2=rw_g2[l].astype(BF16), kkw=rw_kk[l].reshape(1, dr), ka=rw_ka[l].reshape(1, dr),
            u=rw_u[l].reshape(1, dr), bd=bd,
            gn_g=rw_gn_g[l].reshape(1, dr), gn_b=rw_gn_b[l].reshape(1, dr),
            w_up_r=w_up_r[l].astype(BF16), w_up_n=w_up_n[l].astype(BF16), w_o=w_o[l].astype(BF16),
            ln1_g=ln1_g[l].reshape(1, d), ln1_b=ln1_b[l].reshape(1, d),
            ln2_g=ln2_g[l].reshape(1, d), ln2_b=ln2_b[l].reshape(1, d),
            bias=_nbr_bias_table(na_rpb[l], rows),
        )
        if l % 2 == 0:
            lw.update(ffn_w1=ffn_w1[l // 2].astype(BF16), ffn_w3=ffn_w3[l // 2].astype(BF16),
                      ffn_w2=ffn_w2[l // 2].astype(BF16))
        else:
            lw.update(router=moe_router[l // 2], moe_w1=moe_w1[l // 2].astype(BF16),
                      moe_w3=moe_w3[l // 2].astype(BF16), moe_w2=moe_w2[l // 2].astype(BF16))

        mod_ctx = jnp.broadcast_to(mod[l, 0].reshape(1, 6, d), (bc, 6, d))
        mod_lat = mod[l, 1:1 + bs].reshape(bs, 6, d)

        xp, s_fin, k_c, v_c = _layer(xp, mod_ctx, lw, bc, tc_len, alpha, zero_state, None)
        new_s.append(s_fin)
        new_k.append(k_c.reshape(bc, tc_len, N_HEADS, HEAD_DIM))
        new_v.append(v_c.reshape(bc, tc_len, N_HEADS, HEAD_DIM))

        kv = (cache_k[:, l].reshape(bs, -1, dr), cache_v[:, l].reshape(bs, -1, dr))
        xs, _, _, _ = _layer(xs, mod_lat, lw, bs, ts_len, alpha, state_rwkv[:, l], kv)

    return (xp.reshape(bc, tc_len, d), xs.reshape(bs, ts_len, d),
            jnp.stack(new_s, axis=1), jnp.stack(new_k, axis=1), jnp.stack(new_v, axis=1))
```

```python
import functools

import numpy as np
import jax
import jax.numpy as jnp
from jax import lax
from jax.experimental import pallas as pl
from jax.experimental.pallas import tpu as pltpu

F32 = jnp.float32
BF16 = jnp.bfloat16
HIGHEST = lax.Precision.HIGHEST

N_HEADS = 8
HEAD_DIM = 64
D_BRANCH = N_HEADS * HEAD_DIM
V7X_LANES = 128
ROW_BLOCK = 256
SCAN_CHUNK = 64
SCAN_CHUNKS_PER_STEP = 2
MOE_ROWS = 256
KH_MAX = 8
KW = 16
GRID_W = 64
LN_EPS = 1e-5
GN_EPS = 64e-5
NEG_BIG = -1e30
VMEM_LIMIT = 56 * 1024 * 1024


def _params(*sem):
    return pltpu.CompilerParams(dimension_semantics=sem, vmem_limit_bytes=VMEM_LIMIT)


def _const_spec(shape):
    nd = len(shape)
    return pl.BlockSpec(shape, lambda *_: (0,) * nd, pipeline_mode=pl.Buffered(1))


def _layer_spec(stack, l):
    nd = stack.ndim - 1
    return pl.BlockSpec((None,) + stack.shape[1:], lambda *_: (l,) + (0,) * nd, pipeline_mode=pl.Buffered(1))


def _param_specs(params):
    specs = [_layer_spec(*p) if isinstance(p, tuple) else _const_spec(p.shape) for p in params]
    return specs, [p[0] if isinstance(p, tuple) else p for p in params]


class _Layout:
    def __init__(self, passes):
        self.passes = passes
        self.n_seq = sum(b for b, _ in passes)

    def _per_block(self, fn):
        def at(i):
            out, blk0, seq0 = None, 0, 0
            for b, t in self.passes:
                per = t // ROW_BLOCK
                val = fn(i - blk0, per, seq0)
                out = val if out is None else jnp.where(i >= blk0, val, out)
                blk0 += b * per
                seq0 += b
            return out
        return at

    def seq_of_block(self, i):
        return self._per_block(lambda j, per, s0: s0 + j // per)(i)

    def is_first(self, i):
        return self._per_block(lambda j, per, s0: (j % per) == 0)(i)

    def is_last(self, i):
        return self._per_block(lambda j, per, s0: (j % per) == per - 1)(i)

    def mod_spec(self, d):
        return pl.BlockSpec((1, 6, d), lambda i: (self.seq_of_block(i), 0, 0))


def _dot(a, b):
    return jnp.dot(a, b, preferred_element_type=F32)


def _dot_nt(a, b):
    return lax.dot_general(a, b, (((1,), (1,)), ((), ())), preferred_element_type=F32)


def _dot_tn(a, b):
    return lax.dot_general(a, b, (((0,), (0,)), ((), ())), preferred_element_type=F32)


def _mm1(a, b, f=_dot):
    return f(a.astype(BF16), b.astype(BF16))


def _mm_exact_lhs(m, x):
    hi = x.astype(BF16)
    r1 = x - hi.astype(F32)
    mid = r1.astype(BF16)
    lo = (r1 - mid.astype(F32)).astype(BF16)
    return _dot(m, hi) + (_dot(m, mid) + _dot(m, lo))


def _sigmoid(x):
    return 1.0 / (1.0 + jnp.exp(-x))


def _head_sum(x, bd):
    outs = []
    for j in range(x.shape[1] // V7X_LANES):
        xs = x[:, j * V7X_LANES:(j + 1) * V7X_LANES]
        hi = xs.astype(BF16)
        lo = (xs - hi.astype(F32)).astype(BF16)
        outs.append(_dot(hi, bd) + _dot(lo, bd))
    return jnp.concatenate(outs, axis=1)


def _layer_norm(z, g, b):
    mu = jnp.mean(z, axis=-1, keepdims=True)
    zc = z - mu
    var = jnp.mean(zc * zc, axis=-1, keepdims=True)
    return zc * lax.rsqrt(var + LN_EPS) * g + b


def _ada_kernel(c_ref, w_ref, b_ref, o_ref):
    cs = c_ref[...]
    s = cs * _sigmoid(cs)
    o_ref[0] = jnp.dot(s, w_ref[0], preferred_element_type=F32, precision=HIGHEST) + b_ref[0]


def _ada_mod(cvec, w_ada, b_ada):
    depth, d, n6 = w_ada.shape
    tn = n6 // 4
    return pl.pallas_call(
        _ada_kernel,
        grid=(depth, n6 // tn),
        in_specs=[pl.BlockSpec((8, d), lambda l, j: (0, 0)),
                  pl.BlockSpec((1, d, tn), lambda l, j: (l, 0, j)),
                  pl.BlockSpec((1, 1, tn), lambda l, j: (l, 0, j))],
        out_specs=pl.BlockSpec((1, 8, tn), lambda l, j: (l, 0, j)),
        out_shape=jax.ShapeDtypeStruct((depth, 8, n6), F32),
        name="ada_mod",
        compiler_params=_params("parallel", "parallel"),
    )(cvec, w_ada, b_ada.reshape(depth, 1, n6))


def _inproj_kernel(x_ref, mod_ref, w_ref, slab_ref, q_ref, k_ref, v_ref, gate_ref, *, n_slab):
    sh = mod_ref[0, 0:1, :]
    sc = mod_ref[0, 1:2, :]
    h = (x_ref[...] * (1.0 + sc) + sh).astype(BF16)
    dn = D_BRANCH
    slab_ref[...] = _dot(h, w_ref[:, 0:n_slab])
    q_ref[...] = _dot(h, w_ref[:, n_slab:n_slab + dn]).astype(q_ref.dtype)
    k_ref[...] = _dot(h, w_ref[:, n_slab + dn:n_slab + 2 * dn]).astype(k_ref.dtype)
    v_ref[...] = _dot(h, w_ref[:, n_slab + 2 * dn:n_slab + 3 * dn]).astype(v_ref.dtype)
    gate_ref[...] = _sigmoid(_dot(h, w_ref[:, n_slab + 3 * dn:]))


def _in_proj(x, modp, w_in, l, lay, n_slab):
    n, d = x.shape
    n_cols = w_in.shape[2]
    dn = D_BRANCH
    n_gate = n_cols - n_slab - 3 * dn
    kv_dtype = F32
    row = lambda i: (i, 0)
    return pl.pallas_call(
        functools.partial(_inproj_kernel, n_slab=n_slab),
        grid=(n // ROW_BLOCK,),
        in_specs=[pl.BlockSpec((ROW_BLOCK, d), row),
                  lay.mod_spec(d),
                  _layer_spec(w_in, l)],
        out_specs=[pl.BlockSpec((ROW_BLOCK, n_slab), row),
                   pl.BlockSpec((ROW_BLOCK, dn), row), pl.BlockSpec((ROW_BLOCK, dn), row),
                   pl.BlockSpec((ROW_BLOCK, dn), row),
                   pl.BlockSpec((ROW_BLOCK, n_gate), row)],
        out_shape=[jax.ShapeDtypeStruct((n, n_slab), F32),
                   jax.ShapeDtypeStruct((n, dn), BF16), jax.ShapeDtypeStruct((n, dn), kv_dtype),
                   jax.ShapeDtypeStruct((n, dn), kv_dtype),
                   jax.ShapeDtypeStruct((n, n_gate), F32)],
        name="in_proj",
        compiler_params=_params("parallel"),
    )(x, modp, w_in)


def _prep_kernel(cur_ref, prev_ref, next_ref, ts_ref, w0_ref, w2_ref, a0_ref, a2_ref, g2_ref,
                 kkw_ref, ka_ref, u_ref, bd_ref,
                 r_o, kk_o, lwf_o, lwb_o, bf_o, bb_o, ktf_o, ktb_o, v_o, bonus_o, g_o,
                 *, lay, lora_w, lora_a):
    i = pl.program_id(0)
    tb, n_slab = cur_ref.shape
    dr = D_BRANCH
    cur = cur_ref[...]
    first = lay.is_first(i)
    last = lay.is_last(i)
    prow = jnp.where(first, 0.0, prev_ref[7:8, :])
    nrow = jnp.where(last, 0.0, next_ref[0:1, :])
    rid = lax.broadcasted_iota(jnp.int32, (tb, n_slab), 0)
    xp = jnp.where(rid == 0, prow, pltpu.roll(cur, 1, 0))
    xn = jnp.where(rid == tb - 1, nrow, pltpu.roll(cur, tb - 1, 0))
    s = ts_ref[0:1, :] * xp + ts_ref[1:2, :] * cur + ts_ref[2:3, :] * xn

    r = s[:, 0:dr]
    k = s[:, dr:2 * dr]
    v = s[:, 2 * dr:3 * dr]
    o = 3 * dr
    wd = s[:, o:o + 2 * lora_w]
    ad = s[:, o + 2 * lora_w:o + 2 * lora_w + 2 * lora_a]
    gd = s[:, o + 2 * lora_w + 2 * lora_a:]

    wl = w0_ref[...] + _dot(jnp.tanh(wd).astype(BF16), w2_ref[...])
    logw = -float(np.exp(-0.5)) * _sigmoid(wl)
    a = _sigmoid(a0_ref[...] + _dot(ad.astype(BF16), a2_ref[...]))
    g = _dot(_sigmoid(gd).astype(BF16), g2_ref[...])

    bd = bd_ref[...]
    kk = k * kkw_ref[...]
    kk = kk * lax.rsqrt(_head_sum(kk * kk, bd) + 1e-12)
    ka = ka_ref[...]
    a_f = a[:, :dr]
    a_b = a[:, dr:]
    kt_f = k * (1.0 + (a_f - 1.0) * ka)
    kt_b = k * (1.0 + (a_b - 1.0) * ka)
    bonus = _head_sum(r * (0.5 * (kt_f + kt_b)) * u_ref[...], bd) * v

    r_o[...] = r
    kk_o[...] = kk
    lwf_o[...] = logw[:, :dr]
    lwb_o[...] = logw[:, dr:]
    bf_o[...] = kk * a_f
    bb_o[...] = kk * a_b
    ktf_o[...] = kt_f
    ktb_o[...] = kt_b
    v_o[...] = v
    bonus_o[...] = bonus
    g_o[...] = g


def _rwkv_prep(slab, lw, lay):
    n, n_slab = slab.shape
    dr = D_BRANCH
    sub = ROW_BLOCK // 8
    lora_w = lw["w2"].shape[0] // 2
    lora_a = lw["a2"].shape[0] // 2
    row = lambda i: (i, 0)
    consts = [lw["ts"], lw["w0"], lw["w2"], lw["a0"], lw["a2"], lw["g2"], lw["kkw"], lw["ka"], lw["u"], lw["bd"]]
    outs = pl.pallas_call(
        functools.partial(_prep_kernel, lay=lay, lora_w=lora_w, lora_a=lora_a),
        grid=(n // ROW_BLOCK,),
        in_specs=[pl.BlockSpec((ROW_BLOCK, n_slab), row),
                  pl.BlockSpec((8, n_slab), lambda i: (jnp.maximum(i * sub - 1, 0), 0)),
                  pl.BlockSpec((8, n_slab), lambda i: (jnp.minimum((i + 1) * sub, n // 8 - 1), 0))]
                 + [_const_spec(c.shape) for c in consts],
        out_specs=[pl.BlockSpec((ROW_BLOCK, dr), row)] * 11,
        out_shape=[jax.ShapeDtypeStruct((n, dr), F32)] * 11,
        name="rwkv_prep",
        compiler_params=_params("parallel"),
    )(slab, slab, slab, *consts)
    return outs


def _tri_masks(rev):
    c = SCAN_CHUNK
    ti = lax.broadcasted_iota(jnp.int32, (c, 2 * c), 0)
    ii = lax.broadcasted_iota(jnp.int32, (c, 2 * c), 1) % c
    if rev:
        ti, ii = c - 1 - ti, c - 1 - ii
    one = lambda m: jnp.where(m, 1.0, 0.0)
    levels = []
    s = 1
    while s < c:
        same = (ti // (2 * s)) == (ii // (2 * s))
        levels.append(one(same & ((ti % (2 * s)) >= s) & ((ii % (2 * s)) < s)))
        s *= 2
    return one(ii <= ti), one(ii < ti), levels


def _scan_kernel(fwd_tab, bwd_tab, seq_tab, first_tab, last_tab,
                 z0_ref, rf, kkf, lwf, bf, ktf, vf, rb, kkb, lwb, bb, ktb, vb, yf_o, yb_o, zfin_o, z_scr):
    g_step = pl.program_id(0)
    c = SCAN_CHUNK
    lanes = V7X_LANES
    npair = N_HEADS // 2
    assert c == HEAD_DIM and 2 * HEAD_DIM == lanes

    row_blk = lax.broadcasted_iota(jnp.int32, (lanes, lanes), 0) // HEAD_DIM
    col_blk = lax.broadcasted_iota(jnp.int32, (lanes, lanes), 1) // HEAD_DIM
    bd_mask = jnp.where(row_blk == col_blk, 1.0, 0.0)
    eye_full = jnp.where(lax.broadcasted_iota(jnp.int32, (lanes, lanes), 0)
                         == lax.broadcasted_iota(jnp.int32, (lanes, lanes), 1), 1.0, 0.0)
    lane_c = lax.broadcasted_iota(jnp.int32, (c, lanes), 1)
    lo = jnp.where(lane_c < HEAD_DIM, 1.0, 0.0)
    hi = 1.0 - lo
    eye_pair = jnp.where(lax.broadcasted_iota(jnp.int32, (c, lanes), 0) == lane_c % HEAD_DIM, 1.0, 0.0)

    def bd(x):
        return jnp.concatenate([x * lo, x * hi], axis=0)

    @pl.when(first_tab[g_step] == 1)
    def _():
        for d in range(2):
            for p in range(npair):
                za = jnp.concatenate([z0_ref[0, d, 2 * p], z0_ref[0, d, 2 * p + 1]], axis=1)
                z_scr[d * npair + p] = bd(za)

    refs = ((rf, kkf, lwf, bf, ktf, vf, yf_o), (rb, kkb, lwb, bb, ktb, vb, yb_o))
    masks = (_tri_masks(False), _tri_masks(True))
    cum_lhs = [jnp.concatenate([m[0][:, 0:c].astype(BF16), jnp.ones((c, c), BF16)], axis=0) for m in masks]
    nck = rf.shape[0] // c
    units = [(d, p, ck) for d in range(2) for p in range(npair) for ck in range(nck)]

    def tile(d, p, ck):
        row0 = (nck - 1 - ck if d else ck) * c
        return slice(row0, row0 + c), slice(p * lanes, (p + 1) * lanes)

    def per(fn):
        return [fn(i, d, tile(d, p, ck)) for i, (d, p, ck) in enumerate(units)]

    lw = per(lambda i, d, sl: refs[d][2][sl])
    cs = per(lambda i, d, sl: _mm_exact_lhs(cum_lhs[d], lw[i]))
    cum = [x[0:c] for x in cs]
    tot = [x[c:] for x in cs]
    r_s = per(lambda i, d, sl: refs[d][0][sl] * jnp.exp(cum[i]))
    k_s = per(lambda i, d, sl: refs[d][1][sl] * jnp.exp(cum[i] - lw[i]))
    e_neg = per(lambda i, d, sl: jnp.exp(-cum[i]))
    e_end = per(lambda i, d, sl: jnp.exp(tot[i] - cum[i]))
    b = per(lambda i, d, sl: refs[d][3][sl])
    kt = per(lambda i, d, sl: refs[d][4][sl])
    v = per(lambda i, d, sl: refs[d][5][sl])
    v_bd = per(lambda i, d, sl: bd(v[i]))
    kr = per(lambda i, d, sl: jnp.concatenate([k_s[i], r_s[i]], axis=0))
    ab = per(lambda i, d, sl: _mm1(kr[i], bd(b[i] * e_neg[i]), _dot_nt))
    ak = per(lambda i, d, sl: _mm1(kr[i], bd(kt[i] * e_neg[i]), _dot_nt))
    a_b = per(lambda i, d, sl: ab[i][0:c] * masks[d][1])
    b_b = per(lambda i, d, sl: ab[i][c:] * masks[d][0])
    a_kt = per(lambda i, d, sl: ak[i][0:c] * masks[d][1])
    b_kt = per(lambda i, d, sl: ak[i][c:] * masks[d][0])
    dm = per(lambda i, d, sl: eye_pair - a_b[i] * masks[d][2][0])
    for lvl in range(1, len(masks[0][2])):
        zd = per(lambda i, d, sl: _mm1(a_b[i] * masks[d][2][lvl], bd(dm[i])))
        dm = per(lambda i, d, sl: dm[i] - _mm1(dm[i], bd(zd[i])))
    w1 = per(lambda i, d, sl: _mm1(a_kt[i], v_bd[i]))
    g = per(lambda i, d, sl: _mm1(dm[i], jnp.concatenate([bd(k_s[i]), bd(w1[i])], axis=1)))
    bg = per(lambda i, d, sl: _mm1(b_b[i], jnp.concatenate([bd(g[i][:, 0:lanes]), bd(g[i][:, lanes:])], axis=1)))
    q = per(lambda i, d, sl: r_s[i] - bg[i][:, 0:lanes])
    y0 = per(lambda i, d, sl: _mm1(b_kt[i], v_bd[i]) - bg[i][:, lanes:])
    cg = per(lambda i, d, sl: _mm1(b[i] * e_end[i], g[i], _dot_tn))
    phi = per(lambda i, d, sl: eye_full * jnp.exp(jnp.concatenate([tot[i], tot[i]], axis=0))
              - cg[i][:, 0:lanes] * bd_mask)
    psi = per(lambda i, d, sl: (_mm1(kt[i] * e_end[i], v[i], _dot_tn) - cg[i][:, lanes:]) * bd_mask)
    z = [z_scr[s] for s in range(2 * npair)]
    for ck in range(nck):
        sel = [(i, d, p) for i, (d, p, k2) in enumerate(units) if k2 == ck]
        y = [_mm1(q[i], z[d * npair + p]) + y0[i] for i, d, p in sel]
        z = [_mm1(phi[i], z[d * npair + p]) + psi[i] for i, d, p in sel]
        for yy, (i, d, p) in zip(y, sel):
            refs[d][6][tile(d, p, ck)] = yy
    for s in range(2 * npair):
        z_scr[s] = z[s]

    @pl.when(last_tab[g_step] == 1)
    def _():
        for d in range(2):
            for p in range(npair):
                zz = z_scr[d * npair + p]
                zfin_o[0, d, 2 * p] = zz[0:HEAD_DIM, 0:HEAD_DIM]
                zfin_o[0, d, 2 * p + 1] = zz[HEAD_DIM:, HEAD_DIM:]


def _scan_tables(lay):
    blk = SCAN_CHUNK * SCAN_CHUNKS_PER_STEP
    rows, blk0, seq0 = [], 0, 0
    for b, t in lay.passes:
        nj = t // blk
        for s in range(b):
            for j in range(nj):
                rows.append((blk0 + s * nj + j, blk0 + s * nj + nj - 1 - j, seq0 + s, int(j == 0), int(j == nj - 1)))
        blk0 += b * nj
        seq0 += b
    return [jnp.asarray(col, jnp.int32) for col in zip(*rows)]


def _rwkv_scan(z0, r, kk, lwf, lwb, bf, bb, ktf, ktb, v, lay):
    blk = SCAN_CHUNK * SCAN_CHUNKS_PER_STEP
    n, dr = r.shape
    tabs = _scan_tables(lay)
    fwd = pl.BlockSpec((blk, dr), lambda g, fw, bw, sq, fi, la: (fw[g], 0))
    bwd = pl.BlockSpec((blk, dr), lambda g, fw, bw, sq, fi, la: (bw[g], 0))
    st = pl.BlockSpec((1, 2, N_HEADS, HEAD_DIM, HEAD_DIM), lambda g, fw, bw, sq, fi, la: (sq[g], 0, 0, 0, 0))
    grid_spec = pltpu.PrefetchScalarGridSpec(
        num_scalar_prefetch=len(tabs),
        grid=(int(tabs[0].shape[0]),),
        in_specs=[st] + [fwd] * 6 + [bwd] * 6,
        out_specs=[fwd, bwd, st],
        scratch_shapes=[pltpu.VMEM((N_HEADS, V7X_LANES, V7X_LANES), F32)],
    )
    return pl.pallas_call(
        _scan_kernel,
        grid_spec=grid_spec,
        out_shape=[jax.ShapeDtypeStruct((n, dr), F32),
                   jax.ShapeDtypeStruct((n, dr), F32),
                   jax.ShapeDtypeStruct((lay.n_seq, 2, N_HEADS, HEAD_DIM, HEAD_DIM), F32)],
        name="rwkv_scan",
        compiler_params=_params("arbitrary"),
    )(*tabs, z0, r, kk, lwf, bf, ktf, v, r, kk, lwb, bb, ktb, v)


def _softmax_rows(parts):
    m = parts[0].max(axis=-1, keepdims=True)
    for p in parts[1:]:
        m = jnp.maximum(m, p.max(axis=-1, keepdims=True))
    es = [jnp.exp(p - m) for p in parts]
    den = es[0].sum(axis=-1, keepdims=True)
    for e in es[1:]:
        den = den + e.sum(axis=-1, keepdims=True)
    inv = 1.0 / den
    return [e * inv for e in es]


def _ctx_attn_kernel(q_ref, k_ref, v_ref, o_ref):
    scale = HEAD_DIM ** -0.5
    npair = q_ref.shape[1] // V7X_LANES
    tile = lambda ref, p: ref[:, p * V7X_LANES:(p + 1) * V7X_LANES].astype(BF16)
    qb = [tile(q_ref, p) for p in range(npair)]
    k = [tile(k_ref, p) for p in range(npair)]
    v = [tile(v_ref, p) for p in range(npair)]
    first_head = lax.broadcasted_iota(jnp.int32, qb[0].shape, 1) < HEAD_DIM
    head_lanes = (first_head, jnp.logical_not(first_head))
    units = [(p, hh) for p in range(npair) for hh in range(2)]

    def per(fn):
        return [fn(u, p, hh) for u, (p, hh) in enumerate(units)]

    q = per(lambda u, p, hh: jnp.where(head_lanes[hh], qb[p], jnp.zeros_like(qb[p])))
    s = per(lambda u, p, hh: _dot_nt(q[u], k[p]) * scale)
    e = per(lambda u, p, hh: jnp.exp(s[u] - s[u].max(axis=-1, keepdims=True)))
    pr = per(lambda u, p, hh: (e[u] * (1.0 / e[u].sum(axis=-1, keepdims=True))).astype(BF16))
    out = per(lambda u, p, hh: _dot(pr[u], v[p]))
    for p in range(npair):
        o_ref[:, p * V7X_LANES:(p + 1) * V7X_LANES] = jnp.where(first_head, out[2 * p], out[2 * p + 1]).astype(o_ref.dtype)


def _ctx_attention(q, k, v, b_len, t_len):
    n = b_len * t_len
    seq = pl.BlockSpec((t_len, D_BRANCH), lambda b: (b, 0))
    return pl.pallas_call(
        _ctx_attn_kernel,
        grid=(b_len,),
        in_specs=[seq, seq, seq],
        out_specs=seq,
        out_shape=jax.ShapeDtypeStruct((n, D_BRANCH), BF16),
        name="ctx_attn",
        compiler_params=_params("parallel"),
    )(q, k, v)


def _nbr_attn_kernel(q_ref, k_ref, v_ref, kc_ref, vc_ref, bias_ref, o_ref, *, rows, kh):
    scale = HEAD_DIM ** -0.5
    win = kh * GRID_W

    group = 4 if rows % 4 == 0 else 1
    units = [(r, hh) for r in range(group) for hh in range(2)]
    first_head = lax.broadcasted_iota(jnp.int32, (GRID_W, V7X_LANES), 1) < HEAD_DIM
    head_lanes = (first_head, jnp.logical_not(first_head))

    def body(gi, carry):
        def per(fn):
            return [fn(u, r, hh) for u, (r, hh) in enumerate(units)]

        i = [gi * group + r for r in range(group)]
        rs = [jnp.clip(x - kh // 2, 0, rows - kh) for x in i]
        q0 = [pl.multiple_of(x * GRID_W, GRID_W) for x in i]
        k0 = [pl.multiple_of(x * GRID_W, GRID_W) for x in rs]
        qb = [q_ref[pl.ds(x, GRID_W), :].astype(BF16) for x in q0]
        kw = [k_ref[pl.ds(x, win), :].astype(BF16) for x in k0]
        vw = [v_ref[pl.ds(x, win), :].astype(BF16) for x in k0]
        kc = kc_ref[0].astype(BF16)
        vc = vc_ref[0].astype(BF16)
        q = per(lambda u, r, hh: jnp.where(head_lanes[hh], qb[r], jnp.zeros_like(qb[r])))
        s_loc = per(lambda u, r, hh: _dot_nt(q[u], kw[r]) * scale + bias_ref[hh, i[r] - rs[r]])
        s_ctx = per(lambda u, r, hh: _dot_nt(q[u], kc) * scale)
        m = per(lambda u, r, hh: jnp.maximum(s_loc[u].max(axis=-1, keepdims=True), s_ctx[u].max(axis=-1, keepdims=True)))
        e_loc = per(lambda u, r, hh: jnp.exp(s_loc[u] - m[u]))
        e_ctx = per(lambda u, r, hh: jnp.exp(s_ctx[u] - m[u]))
        inv = per(lambda u, r, hh: 1.0 / (e_loc[u].sum(axis=-1, keepdims=True) + e_ctx[u].sum(axis=-1, keepdims=True)))
        out = per(lambda u, r, hh: _dot((e_loc[u] * inv[u]).astype(BF16), vw[r])
                  + _dot((e_ctx[u] * inv[u]).astype(BF16), vc))
        for r in range(group):
            o_ref[pl.ds(q0[r], GRID_W), :] = jnp.where(first_head, out[2 * r], out[2 * r + 1]).astype(o_ref.dtype)
        return carry

    lax.fori_loop(0, rows // group, body, 0)


def _nbr_attention(q, k, v, k_ctx, v_ctx, bias, b_len, t_len, row0):
    n = b_len * t_len
    npair = N_HEADS // 2
    rows = t_len // GRID_W
    kh = min(KH_MAX, rows)
    c_len = k_ctx.shape[1]
    assert row0 % t_len == 0
    seq0 = row0 // t_len
    pair = pl.BlockSpec((t_len, V7X_LANES), lambda b, p: (seq0 + b, p))
    return pl.pallas_call(
        functools.partial(_nbr_attn_kernel, rows=rows, kh=kh),
        grid=(b_len, npair),
        in_specs=[pair, pair, pair,
                  pl.BlockSpec((1, c_len, V7X_LANES), lambda b, p: (b, 0, p)),
                  pl.BlockSpec((1, c_len, V7X_LANES), lambda b, p: (b, 0, p)),
                  pl.BlockSpec((2, kh, GRID_W, kh * GRID_W), lambda b, p: (p, 0, 0, 0))],
        out_specs=pl.BlockSpec((t_len, V7X_LANES), lambda b, p: (b, p)),
        out_shape=jax.ShapeDtypeStruct((n, D_BRANCH), BF16),
        name="nbr_attn",
        compiler_params=_params("parallel", "parallel"),
    )(q, k, v, k_ctx, v_ctx, bias)


def _nbr_bias_table(rpb, rows):
    kh = min(KH_MAX, rows)
    w = np.arange(GRID_W)
    col_start = np.clip(w - KW // 2, 0, GRID_W - KW)
    c = np.arange(GRID_W)
    inside = (c[None, :] >= col_start[:, None]) & (c[None, :] < col_start[:, None] + KW)
    rp = jnp.pad(rpb, ((0, 0), (0, 0), (GRID_W - KW, GRID_W - KW)))
    toep = jnp.stack([rp[:, :, GRID_W - 1 - wi:2 * GRID_W - 1 - wi] for wi in range(GRID_W)], axis=2)
    toep = jnp.where(inside[None, None], toep, NEG_BIG)
    tab = jnp.stack([toep[:, KH_MAX - 1 - dd:KH_MAX - 1 - dd + kh] for dd in range(kh)], axis=1)
    tab = jnp.transpose(tab, (0, 1, 3, 2, 4))
    return tab.reshape(rpb.shape[0], kh, GRID_W, kh * GRID_W)


def _top2_route(logits, n_exp):
    lane = lax.broadcasted_iota(jnp.int32, logits.shape, 1)
    lane_f = lane.astype(F32)
    lg = jnp.where(lane < n_exp, logits, -jnp.inf)
    m1 = lg.max(axis=-1, keepdims=True)
    i1 = jnp.where(lg == m1, lane_f, float(V7X_LANES)).min(axis=-1, keepdims=True)
    lg2 = jnp.where(lane_f == i1, -jnp.inf, lg)
    m2 = lg2.max(axis=-1, keepdims=True)
    i2 = jnp.where(lg2 == m2, lane_f, float(V7X_LANES)).min(axis=-1, keepdims=True)
    e = jnp.exp(m2 - m1)
    g1 = 1.0 / (1.0 + e)
    g2 = e * g1
    out = jnp.where(lane == 0, i1, 0.0)
    out = jnp.where(lane == 1, i2, out)
    out = jnp.where(lane == 2, g1, out)
    return jnp.where(lane == 3, g2, out)


def _mix_kernel(yf_ref, yb_ref, bonus_ref, g_ref, yn_ref, gate_ref, x_ref, mod_ref,
                gng_ref, gnb_ref, bd_ref, wur_ref, wun_ref, wo_ref, lng_ref, lnb_ref, *rest,
                alpha, n_exp):
    if n_exp:
        router_ref, x1_o, h2_o, route_o = rest
    else:
        x1_o, h2_o = rest
    d = x_ref.shape[1]
    bd = bd_ref[...]
    y = yf_ref[...] + yb_ref[...]
    inv_n = 1.0 / HEAD_DIM
    mu = _head_sum(y, bd) * inv_n
    yc = y - mu
    var = _head_sum(yc * yc, bd) * inv_n
    yr = yc * lax.rsqrt(var + GN_EPS) * gng_ref[...] + gnb_ref[...]
    yr = (yr + bonus_ref[...]) * g_ref[...]
    m = (gate_ref[:, 0:d] * _dot(yr.astype(BF16), wur_ref[...])
         + gate_ref[:, d:] * _dot(yn_ref[...], wun_ref[...]))
    o = _dot(m.astype(BF16), wo_ref[...])
    g1 = mod_ref[0, 2:3, :]
    sh2 = mod_ref[0, 3:4, :]
    sc2 = mod_ref[0, 4:5, :]
    x1 = _layer_norm(alpha * x_ref[...] + g1 * o, lng_ref[...], lnb_ref[...])
    x1_o[...] = x1
    h2 = x1 * (1.0 + sc2) + sh2
    h2_o[...] = h2.astype(BF16)
    if n_exp:
        logits = jnp.dot(h2, router_ref[...], preferred_element_type=F32, precision=HIGHEST)
        route_o[...] = _top2_route(logits, n_exp)


def _mix_out(yf, yb, bonus, g, yn, gates, x, modp, lw, lay, alpha, router):
    n, d = x.shape
    dr = D_BRANCH
    row = lambda i: (i, 0)
    yt_spec = pl.BlockSpec((ROW_BLOCK, dr), row)
    consts = [lw["gn_g"], lw["gn_b"], lw["bd"], lw["w_up_r"], lw["w_up_n"], lw["w_o"], lw["ln1_g"], lw["ln1_b"]]
    n_exp = 0
    out_specs = [pl.BlockSpec((ROW_BLOCK, d), row), pl.BlockSpec((ROW_BLOCK, d), row)]
    out_shape = [jax.ShapeDtypeStruct((n, d), F32), jax.ShapeDtypeStruct((n, d), BF16)]
    if router is not None:
        n_exp = router.shape[1]
        consts.append(jnp.pad(router, ((0, 0), (0, V7X_LANES - n_exp))))
        out_specs.append(pl.BlockSpec((ROW_BLOCK, V7X_LANES), row))
        out_shape.append(jax.ShapeDtypeStruct((n, V7X_LANES), F32))
    const_specs, const_ops = _param_specs(consts)
    return pl.pallas_call(
        functools.partial(_mix_kernel, alpha=alpha, n_exp=n_exp),
        grid=(n // ROW_BLOCK,),
        in_specs=[yt_spec, yt_spec,
                  pl.BlockSpec((ROW_BLOCK, dr), row), pl.BlockSpec((ROW_BLOCK, dr), row),
                  pl.BlockSpec((ROW_BLOCK, dr), row), pl.BlockSpec((ROW_BLOCK, 2 * d), row),
                  pl.BlockSpec((ROW_BLOCK, d), row),
                  lay.mod_spec(d)]
                 + const_specs,
        out_specs=out_specs,
        out_shape=out_shape,
        name="mix_out",
        compiler_params=_params("parallel"),
    )(yf, yb, bonus, g, yn, gates, x, modp, *const_ops)


def _swiglu(h, w1, w3, w2):
    u = _dot(h, w1)
    act = u * _sigmoid(u) * _dot(h, w3)
    return _dot(act.astype(BF16), w2)


def _ffn_kernel(h_ref, x1_ref, mod_ref, w1_ref, w3_ref, w2_ref, lng_ref, lnb_ref, o_ref, *, alpha):
    f = _swiglu(h_ref[...], w1_ref[...], w3_ref[...], w2_ref[...])
    g2 = mod_ref[0, 5:6, :]
    o_ref[...] = _layer_norm(alpha * x1_ref[...] + g2 * f, lng_ref[...], lnb_ref[...])


def _ffn_dense(h2, x1, modp, w1, w3, w2, ln_g, ln_b, lay, alpha):
    n, d = x1.shape
    row = lambda i: (i, 0)
    const_specs, const_ops = _param_specs([w1, w3, w2, ln_g, ln_b])
    return pl.pallas_call(
        functools.partial(_ffn_kernel, alpha=alpha),
        grid=(n // ROW_BLOCK,),
        in_specs=[pl.BlockSpec((ROW_BLOCK, d), row), pl.BlockSpec((ROW_BLOCK, d), row), lay.mod_spec(d)]
                 + const_specs,
        out_specs=pl.BlockSpec((ROW_BLOCK, d), row),
        out_shape=jax.ShapeDtypeStruct((n, d), F32),
        name="ffn_dense",
        compiler_params=_params("parallel"),
    )(h2, x1, modp, *const_ops)


def _moe_kernel(be_ref, nb_ref, x_ref, w1_ref, w3_ref, w2_ref, o_ref):
    i = pl.program_id(0)

    @pl.when(i < nb_ref[0])
    def _():
        o_ref[...] = _swiglu(x_ref[...], w1_ref[0], w3_ref[0], w2_ref[0])

    @pl.when(i >= nb_ref[0])
    def _():
        o_ref[...] = jnp.zeros(o_ref.shape, F32)


def _moe_experts(block_e, n_used, xb, w1, w3, w2, lm):
    p, d = xb.shape
    dff = w1.shape[3]
    nblk = p // MOE_ROWS
    one = pl.Buffered(1)
    expert = lambda i, be, nb: (lm, be[i], 0, 0)
    grid_spec = pltpu.PrefetchScalarGridSpec(
        num_scalar_prefetch=2,
        grid=(nblk,),
        in_specs=[pl.BlockSpec((MOE_ROWS, d), lambda i, be, nb: (i, 0)),
                  pl.BlockSpec((None, 1, d, dff), expert, pipeline_mode=one),
                  pl.BlockSpec((None, 1, d, dff), expert, pipeline_mode=one),
                  pl.BlockSpec((None, 1, dff, d), expert, pipeline_mode=one)],
        out_specs=pl.BlockSpec((MOE_ROWS, d), lambda i, be, nb: (i, 0)),
    )
    return pl.pallas_call(
        _moe_kernel,
        grid_spec=grid_spec,
        out_shape=jax.ShapeDtypeStruct((p, d), F32),
        name="moe_experts",
        compiler_params=_params("arbitrary"),
    )(block_e, n_used, xb, w1, w3, w2)


def _combine_kernel(ya_ref, yb_ref, route_ref, x1_ref, mod_ref, lng_ref, lnb_ref, o_ref, *, alpha):
    f = route_ref[:, 2:3] * ya_ref[...] + route_ref[:, 3:4] * yb_ref[...]
    g2 = mod_ref[0, 5:6, :]
    o_ref[...] = _layer_norm(alpha * x1_ref[...] + g2 * f, lng_ref[...], lnb_ref[...])


def _moe_combine(ya, yb, route, x1, modp, ln_g, ln_b, lay, alpha):
    n, d = x1.shape
    row = lambda i: (i, 0)
    return pl.pallas_call(
        functools.partial(_combine_kernel, alpha=alpha),
        grid=(n // ROW_BLOCK,),
        in_specs=[pl.BlockSpec((ROW_BLOCK, d), row), pl.BlockSpec((ROW_BLOCK, d), row),
                  pl.BlockSpec((ROW_BLOCK, V7X_LANES), row), pl.BlockSpec((ROW_BLOCK, d), row),
                  lay.mod_spec(d),
                  _const_spec(ln_g.shape), _const_spec(ln_b.shape)],
        out_specs=pl.BlockSpec((ROW_BLOCK, d), row),
        out_shape=jax.ShapeDtypeStruct((n, d), F32),
        name="moe_combine",
        compiler_params=_params("parallel"),
    )(ya, yb, route, x1, modp, ln_g, ln_b)


def _moe(h2, route, x1, modp, w1, w3, w2, lm, ln_g, ln_b, lay, alpha):
    n, d = x1.shape
    n_exp = w1.shape[1]
    top_e = route[:, 0:2].astype(jnp.int32)
    flat_e = top_e.reshape(-1)
    onehot = (flat_e[:, None] == jnp.arange(n_exp, dtype=jnp.int32)[None, :]).astype(jnp.int32)
    rank = jnp.sum((jnp.cumsum(onehot, axis=0) - onehot) * onehot, axis=1)
    counts = jnp.sum(onehot, axis=0)
    padded = (counts + MOE_ROWS - 1) // MOE_ROWS * MOE_ROWS
    pad_end = jnp.cumsum(padded)
    pad_start = pad_end - padded
    dest = pad_start[flat_e] + rank
    nblk = -(-(2 * n + n_exp * (MOE_ROWS - 1)) // MOE_ROWS)
    p = nblk * MOE_ROWS
    flat_tok = jnp.repeat(jnp.arange(n, dtype=jnp.int32), 2)
    buf_tok = jnp.full((p,), n, jnp.int32).at[dest].set(flat_tok)
    blk_start = jnp.arange(nblk, dtype=jnp.int32) * MOE_ROWS
    block_e = jnp.minimum(jnp.sum((pad_end[None, :] <= blk_start[:, None]).astype(jnp.int32), axis=1), n_exp - 1)
    n_used = (pad_end[-1] // MOE_ROWS).astype(jnp.int32).reshape(1)
    xb = jnp.concatenate([h2, jnp.zeros((1, d), h2.dtype)], axis=0)[buf_tok]
    yb = _moe_experts(block_e, n_used, xb, w1, w3, w2, lm)
    dest2 = dest.reshape(n, 2)
    return _moe_combine(yb[dest2[:, 0]], yb[dest2[:, 1]], route, x1, modp, ln_g, ln_b, lay, alpha)


def _layer(x, modp, lw, l, lay, alpha, z0, ctx_kv):
    (bc, tc_len), (bs, ts_len) = lay.passes
    n_ctx = bc * tc_len
    slab, qn, kn, vn, gates = _in_proj(x, modp, lw["w_in"], l, lay, lw["n_slab"])
    r, kk, lwf, lwb, bf, bb, ktf, ktb, v, bonus, g = _rwkv_prep(slab, lw, lay)
    yf, yb, z_fin = _rwkv_scan(z0, r, kk, lwf, lwb, bf, bb, ktf, ktb, v, lay)
    yn = jnp.concatenate([_ctx_attention(qn, kn, vn, bc, tc_len),
                          _nbr_attention(qn, kn, vn, ctx_kv[0], ctx_kv[1], lw["bias"], bs, ts_len, n_ctx)], axis=0)
    outs = _mix_out(yf, yb, bonus, g, yn, gates, x, modp, lw, lay, alpha, lw.get("router"))
    if "router" in lw:
        x1, h2, route = outs
        x2 = _moe(h2, route, x1, modp, lw["moe_w1"], lw["moe_w3"], lw["moe_w2"], l // 2, lw["ln2_g"], lw["ln2_b"],
                  lay, alpha)
    else:
        x1, h2 = outs
        x2 = _ffn_dense(h2, x1, modp, (lw["ffn_w1"], l // 2), (lw["ffn_w3"], l // 2), (lw["ffn_w2"], l // 2),
                        lw["ln2_g"], lw["ln2_b"], lay, alpha)
    return x2, z_fin, kn[:n_ctx], vn[:n_ctx]


def kernel(x_prompt, x_sample, c, state_rwkv, cache_k, cache_v, c_ctx, w_ada, b_ada, w_in, ts_w, rw_w0, rw_w2, rw_a0, rw_a2, rw_g2, rw_kk, rw_ka, rw_u, rw_gn_g, rw_gn_b, na_rpb, w_up_r, w_up_n, w_o, ln1_g, ln1_b, ln2_g, ln2_b, ffn_w1, ffn_w3, ffn_w2, moe_router, moe_w1, moe_w3, moe_w2):
    depth = w_in.shape[0]
    bc, tc_len, d = x_prompt.shape
    bs, ts_len, _ = x_sample.shape
    dr = D_BRANCH
    alpha = float((2 * depth) ** 0.25)
    n_slab = ts_w.shape[2]
    lora_w = rw_w2.shape[2]
    lora_a = rw_a2.shape[2]

    cvec = jnp.zeros((8, d), F32).at[0].set(c_ctx).at[1:1 + bs].set(c)
    mod = _ada_mod(cvec, w_ada, b_ada)

    half = np.kron(np.eye(2, dtype=np.float32), np.ones((HEAD_DIM, HEAD_DIM), np.float32))
    bd = jnp.asarray(half, BF16)
    rows = ts_len // GRID_W

    lay = _Layout(((bc, tc_len), (bs, ts_len)))
    n_ctx = bc * tc_len
    x = jnp.concatenate([x_prompt.reshape(n_ctx, d), x_sample.reshape(bs * ts_len, d)], axis=0)
    zero_state = jnp.zeros((bc, 2, N_HEADS, HEAD_DIM, HEAD_DIM), F32)
    w_in_b, w_up_r_b, w_up_n_b, w_o_b = (w.astype(BF16) for w in (w_in, w_up_r, w_up_n, w_o))
    ffn_b = [w.astype(BF16) for w in (ffn_w1, ffn_w3, ffn_w2)]
    moe_b = [w.astype(BF16) for w in (moe_w1, moe_w3, moe_w2)]
    new_s, new_k, new_v = [], [], []
    for l in range(depth):
        w2blk = jnp.zeros((2 * lora_w, 2 * dr), F32).at[:lora_w, :dr].set(rw_w2[l, 0]).at[lora_w:, dr:].set(rw_w2[l, 1])
        a2blk = jnp.zeros((2 * lora_a, 2 * dr), F32).at[:lora_a, :dr].set(rw_a2[l, 0]).at[lora_a:, dr:].set(rw_a2[l, 1])
        lw = dict(
            n_slab=n_slab,
            w_in=w_in_b, ts=ts_w[l],
            w0=rw_w0[l].reshape(1, 2 * dr), w2=w2blk.astype(BF16),
            a0=rw_a0[l].reshape(1, 2 * dr), a2=a2blk.astype(BF16),
            g2=rw_g2[l].astype(BF16), kkw=rw_kk[l].reshape(1, dr), ka=rw_ka[l].reshape(1, dr),
            u=rw_u[l].reshape(1, dr), bd=bd,
            gn_g=rw_gn_g[l].reshape(1, dr), gn_b=rw_gn_b[l].reshape(1, dr),
            w_up_r=(w_up_r_b, l), w_up_n=(w_up_n_b, l), w_o=(w_o_b, l),
            ln1_g=ln1_g[l].reshape(1, d), ln1_b=ln1_b[l].reshape(1, d),
            ln2_g=ln2_g[l].reshape(1, d), ln2_b=ln2_b[l].reshape(1, d),
            bias=_nbr_bias_table(na_rpb[l], rows),
        )
        if l % 2 == 0:
            lw.update(ffn_w1=ffn_b[0], ffn_w3=ffn_b[1], ffn_w2=ffn_b[2])
        else:
            lw.update(router=moe_router[l // 2], moe_w1=moe_b[0], moe_w3=moe_b[1], moe_w2=moe_b[2])

        modp = jnp.concatenate([jnp.broadcast_to(mod[l, 0].reshape(1, 6, d), (bc, 6, d)),
                                mod[l, 1:1 + bs].reshape(bs, 6, d)], axis=0)
        z0 = jnp.concatenate([zero_state, jnp.swapaxes(state_rwkv[:, l], -1, -2)], axis=0)
        kv = (cache_k[:, l].reshape(bs, -1, dr), cache_v[:, l].reshape(bs, -1, dr))
        x, z_fin, k_c, v_c = _layer(x, modp, lw, l, lay, alpha, z0, kv)
        new_s.append(jnp.swapaxes(z_fin[:bc], -1, -2))
        new_k.append(k_c.reshape(bc, tc_len, N_HEADS, HEAD_DIM))
        new_v.append(v_c.reshape(bc, tc_len, N_HEADS, HEAD_DIM))

    return (x[:n_ctx].reshape(bc, tc_len, d), x[n_ctx:].reshape(bs, ts_len, d),
            jnp.stack(new_s, axis=1), jnp.stack(new_k, axis=1), jnp.stack(new_v, axis=1))
```

```python
import functools

import numpy as np
import jax
import jax.numpy as jnp
from jax import lax
from jax.experimental import pallas as pl
from jax.experimental.pallas import tpu as pltpu

F32 = jnp.float32
BF16 = jnp.bfloat16
HIGHEST = lax.Precision.HIGHEST

N_HEADS = 8
HEAD_DIM = 64
D_BRANCH = N_HEADS * HEAD_DIM
V7X_LANES = 128
ROW_BLOCK = 256
SCAN_CHUNK = 64
SCAN_CHUNKS_PER_STEP = 2
MOE_ROWS = 256
KH_MAX = 8
KW = 16
GRID_W = 64
LN_EPS = 1e-5
GN_EPS = 64e-5
NEG_BIG = -1e30
VMEM_LIMIT = 56 * 1024 * 1024


def _params(*sem):
    return pltpu.CompilerParams(dimension_semantics=sem, vmem_limit_bytes=VMEM_LIMIT)


def _const_spec(shape):
    nd = len(shape)
    return pl.BlockSpec(shape, lambda *_: (0,) * nd, pipeline_mode=pl.Buffered(1))


def _layer_spec(stack, l):
    nd = stack.ndim - 1
    return pl.BlockSpec((None,) + stack.shape[1:], lambda *_: (l,) + (0,) * nd, pipeline_mode=pl.Buffered(1))


def _param_specs(params):
    specs = [_layer_spec(*p) if isinstance(p, tuple) else _const_spec(p.shape) for p in params]
    return specs, [p[0] if isinstance(p, tuple) else p for p in params]


class _Layout:
    def __init__(self, passes, layer=0):
        self.passes = passes
        self.layer = layer
        self.n_seq = sum(b for b, _ in passes)

    def _per_block(self, fn):
        def at(i):
            out, blk0, seq0 = None, 0, 0
            for b, t in self.passes:
                per = t // ROW_BLOCK
                val = fn(i - blk0, per, seq0)
                out = val if out is None else jnp.where(i >= blk0, val, out)
                blk0 += b * per
                seq0 += b
            return out
        return at

    def seq_of_block(self, i):
        return self._per_block(lambda j, per, s0: s0 + j // per)(i)

    def is_first(self, i):
        return self._per_block(lambda j, per, s0: (j % per) == 0)(i)

    def is_last(self, i):
        return self._per_block(lambda j, per, s0: (j % per) == per - 1)(i)

    def mod_spec(self, d):
        return pl.BlockSpec((None, 1, 6, d), lambda i: (self.layer, self.seq_of_block(i), 0, 0))


def _dot(a, b):
    return jnp.dot(a, b, preferred_element_type=F32)


def _dot_nt(a, b):
    return lax.dot_general(a, b, (((1,), (1,)), ((), ())), preferred_element_type=F32)


def _dot_tn(a, b):
    return lax.dot_general(a, b, (((0,), (0,)), ((), ())), preferred_element_type=F32)


def _mm1(a, b, f=_dot):
    return f(a.astype(BF16), b.astype(BF16))


def _mm_mask_lhs(m, x):
    hi = x.astype(BF16)
    lo = (x - hi.astype(F32)).astype(BF16)
    return _dot(m, hi) + _dot(m, lo)


def _sigmoid(x):
    return 1.0 / (1.0 + jnp.exp(-x))


def _head_sum(x, bd):
    outs = []
    for j in range(x.shape[1] // V7X_LANES):
        xs = x[:, j * V7X_LANES:(j + 1) * V7X_LANES]
        hi = xs.astype(BF16)
        lo = (xs - hi.astype(F32)).astype(BF16)
        outs.append(_dot(hi, bd) + _dot(lo, bd))
    return jnp.concatenate(outs, axis=1)


def _layer_norm(z, g, b):
    mu = jnp.mean(z, axis=-1, keepdims=True)
    zc = z - mu
    var = jnp.mean(zc * zc, axis=-1, keepdims=True)
    return zc * lax.rsqrt(var + LN_EPS) * g + b


def _ada_kernel(c_ref, w_ref, b_ref, o_ref):
    cs = c_ref[...]
    s = cs * _sigmoid(cs)
    o_ref[0] = jnp.dot(s, w_ref[0], preferred_element_type=F32, precision=HIGHEST) + b_ref[0]


def _ada_mod(cvec, w_ada, b_ada):
    depth, d, n6 = w_ada.shape
    tn = n6 // 4
    return pl.pallas_call(
        _ada_kernel,
        grid=(depth, n6 // tn),
        in_specs=[pl.BlockSpec((8, d), lambda l, j: (0, 0)),
                  pl.BlockSpec((1, d, tn), lambda l, j: (l, 0, j)),
                  pl.BlockSpec((1, 1, tn), lambda l, j: (l, 0, j))],
        out_specs=pl.BlockSpec((1, 8, tn), lambda l, j: (l, 0, j)),
        out_shape=jax.ShapeDtypeStruct((depth, 8, n6), F32),
        name="ada_mod",
        compiler_params=_params("parallel", "parallel"),
    )(cvec, w_ada, b_ada.reshape(depth, 1, n6))


def _inproj_kernel(x_ref, mod_ref, w_ref, slab_ref, q_ref, k_ref, v_ref, kc_ref, vc_ref, gate_ref,
                   *, n_slab, nb_cache):
    sh = mod_ref[0, 0:1, :]
    sc = mod_ref[0, 1:2, :]
    h = (x_ref[...] * (1.0 + sc) + sh).astype(BF16)
    dn = D_BRANCH
    slab_ref[...] = _dot(h, w_ref[:, 0:n_slab])
    q_ref[...] = _dot(h, w_ref[:, n_slab:n_slab + dn]).astype(q_ref.dtype)
    k = _dot(h, w_ref[:, n_slab + dn:n_slab + 2 * dn])
    v = _dot(h, w_ref[:, n_slab + 2 * dn:n_slab + 3 * dn])
    k_ref[...] = k.astype(k_ref.dtype)
    v_ref[...] = v.astype(v_ref.dtype)

    @pl.when(pl.program_id(0) < nb_cache)
    def _():
        kc_ref[...] = k
        vc_ref[...] = v

    gate_ref[...] = _sigmoid(_dot(h, w_ref[:, n_slab + 3 * dn:]))


def _in_proj(x, modp, w_in, l, lay, n_slab):
    n, d = x.shape
    n_cols = w_in.shape[2]
    dn = D_BRANCH
    n_gate = n_cols - n_slab - 3 * dn
    n_cache = lay.passes[0][0] * lay.passes[0][1]
    nb_cache = n_cache // ROW_BLOCK
    row = lambda i: (i, 0)
    cache_row = lambda i: (jnp.minimum(i, nb_cache - 1), 0)
    return pl.pallas_call(
        functools.partial(_inproj_kernel, n_slab=n_slab, nb_cache=nb_cache),
        grid=(n // ROW_BLOCK,),
        in_specs=[pl.BlockSpec((ROW_BLOCK, d), row),
                  lay.mod_spec(d),
                  _layer_spec(w_in, l)],
        out_specs=[pl.BlockSpec((ROW_BLOCK, n_slab), row),
                   pl.BlockSpec((ROW_BLOCK, dn), row), pl.BlockSpec((ROW_BLOCK, dn), row),
                   pl.BlockSpec((ROW_BLOCK, dn), row),
                   pl.BlockSpec((ROW_BLOCK, dn), cache_row), pl.BlockSpec((ROW_BLOCK, dn), cache_row),
                   pl.BlockSpec((ROW_BLOCK, n_gate), row)],
        out_shape=[jax.ShapeDtypeStruct((n, n_slab), F32),
                   jax.ShapeDtypeStruct((n, dn), BF16), jax.ShapeDtypeStruct((n, dn), BF16),
                   jax.ShapeDtypeStruct((n, dn), BF16),
                   jax.ShapeDtypeStruct((n_cache, dn), F32), jax.ShapeDtypeStruct((n_cache, dn), F32),
                   jax.ShapeDtypeStruct((n, n_gate), F32)],
        name="in_proj",
        compiler_params=_params("arbitrary"),
    )(x, modp, w_in)


def _prep_kernel(cur_ref, prev_ref, next_ref, ts_ref, w0_ref, w2_ref, a0_ref, a2_ref, g2_ref,
                 kkw_ref, ka_ref, u_ref, bd_ref,
                 r_o, kk_o, lwf_o, lwb_o, bf_o, bb_o, ktf_o, ktb_o, v_o, bonus_o, g_o,
                 *, lay, lora_w, lora_a):
    i = pl.program_id(0)
    tb, n_slab = cur_ref.shape
    dr = D_BRANCH
    cur = cur_ref[...]
    first = lay.is_first(i)
    last = lay.is_last(i)
    prow = jnp.where(first, 0.0, prev_ref[7:8, :])
    nrow = jnp.where(last, 0.0, next_ref[0:1, :])
    rid = lax.broadcasted_iota(jnp.int32, (tb, n_slab), 0)
    xp = jnp.where(rid == 0, prow, pltpu.roll(cur, 1, 0))
    xn = jnp.where(rid == tb - 1, nrow, pltpu.roll(cur, tb - 1, 0))
    s = ts_ref[0:1, :] * xp + ts_ref[1:2, :] * cur + ts_ref[2:3, :] * xn

    r = s[:, 0:dr]
    k = s[:, dr:2 * dr]
    v = s[:, 2 * dr:3 * dr]
    o = 3 * dr
    wd = s[:, o:o + 2 * lora_w]
    ad = s[:, o + 2 * lora_w:o + 2 * lora_w + 2 * lora_a]
    gd = s[:, o + 2 * lora_w + 2 * lora_a:]

    wl = w0_ref[...] + _dot(jnp.tanh(wd).astype(BF16), w2_ref[...])
    logw = -float(np.exp(-0.5)) * _sigmoid(wl)
    a = _sigmoid(a0_ref[...] + _dot(ad.astype(BF16), a2_ref[...]))
    g = _dot(_sigmoid(gd).astype(BF16), g2_ref[...])

    bd = bd_ref[...]
    kk = k * kkw_ref[...]
    kk = kk * lax.rsqrt(_head_sum(kk * kk, bd) + 1e-12)
    ka = ka_ref[...]
    a_f = a[:, :dr]
    a_b = a[:, dr:]
    kt_f = k * (1.0 + (a_f - 1.0) * ka)
    kt_b = k * (1.0 + (a_b - 1.0) * ka)
    bonus = _head_sum(r * (0.5 * (kt_f + kt_b)) * u_ref[...], bd) * v

    r_o[...] = r
    kk_o[...] = kk
    lwf_o[...] = logw[:, :dr]
    lwb_o[...] = logw[:, dr:]
    bf_o[...] = kk * a_f
    bb_o[...] = kk * a_b
    ktf_o[...] = kt_f
    ktb_o[...] = kt_b
    v_o[...] = v
    bonus_o[...] = bonus
    g_o[...] = g


def _rwkv_prep(slab, lw, lay):
    n, n_slab = slab.shape
    dr = D_BRANCH
    sub = ROW_BLOCK // 8
    lora_w = lw["w2"].shape[0] // 2
    lora_a = lw["a2"].shape[0] // 2
    row = lambda i: (i, 0)
    consts = [lw["ts"], lw["w0"], lw["w2"], lw["a0"], lw["a2"], lw["g2"], lw["kkw"], lw["ka"], lw["u"], lw["bd"]]
    outs = pl.pallas_call(
        functools.partial(_prep_kernel, lay=lay, lora_w=lora_w, lora_a=lora_a),
        grid=(n // ROW_BLOCK,),
        in_specs=[pl.BlockSpec((ROW_BLOCK, n_slab), row),
                  pl.BlockSpec((8, n_slab), lambda i: (jnp.maximum(i * sub - 1, 0), 0)),
                  pl.BlockSpec((8, n_slab), lambda i: (jnp.minimum((i + 1) * sub, n // 8 - 1), 0))]
                 + [_const_spec(c.shape) for c in consts],
        out_specs=[pl.BlockSpec((ROW_BLOCK, dr), row)] * 11,
        out_shape=[jax.ShapeDtypeStruct((n, dr), F32)] * 11,
        name="rwkv_prep",
        compiler_params=_params("parallel"),
    )(slab, slab, slab, *consts)
    return outs


def _tri_masks(rev):
    c = SCAN_CHUNK
    ti = lax.broadcasted_iota(jnp.int32, (c, 2 * c), 0)
    ii = lax.broadcasted_iota(jnp.int32, (c, 2 * c), 1) % c
    if rev:
        ti, ii = c - 1 - ti, c - 1 - ii
    one = lambda m: jnp.where(m, 1.0, 0.0)
    levels = []
    s = 1
    while s < c:
        same = (ti // (2 * s)) == (ii // (2 * s))
        levels.append(one(same & ((ti % (2 * s)) >= s) & ((ii % (2 * s)) < s)))
        s *= 2
    return one(ii <= ti), one(ii < ti), levels


def _scan_kernel(fwd_tab, bwd_tab, seq_tab, first_tab, last_tab,
                 z0_ref, rf, kkf, lwf, bf, ktf, vf, rb, kkb, lwb, bb, ktb, vb, yf_o, yb_o, zfin_o, z_scr):
    g_step = pl.program_id(0)
    c = SCAN_CHUNK
    lanes = V7X_LANES
    npair = N_HEADS // 2
    assert c == HEAD_DIM and 2 * HEAD_DIM == lanes

    row_blk = lax.broadcasted_iota(jnp.int32, (lanes, lanes), 0) // HEAD_DIM
    col_blk = lax.broadcasted_iota(jnp.int32, (lanes, lanes), 1) // HEAD_DIM
    bd_mask = jnp.where(row_blk == col_blk, 1.0, 0.0)
    eye_full = jnp.where(lax.broadcasted_iota(jnp.int32, (lanes, lanes), 0)
                         == lax.broadcasted_iota(jnp.int32, (lanes, lanes), 1), 1.0, 0.0)
    lane_c = lax.broadcasted_iota(jnp.int32, (c, lanes), 1)
    lo = jnp.where(lane_c < HEAD_DIM, 1.0, 0.0)
    hi = 1.0 - lo
    eye_pair = jnp.where(lax.broadcasted_iota(jnp.int32, (c, lanes), 0) == lane_c % HEAD_DIM, 1.0, 0.0)

    def bd(x):
        return jnp.concatenate([x * lo, x * hi], axis=0)

    @pl.when(first_tab[g_step] == 1)
    def _():
        for d in range(2):
            for p in range(npair):
                za = jnp.concatenate([z0_ref[0, d, 2 * p], z0_ref[0, d, 2 * p + 1]], axis=1)
                z_scr[d * npair + p] = bd(za)

    refs = ((rf, kkf, lwf, bf, ktf, vf, yf_o), (rb, kkb, lwb, bb, ktb, vb, yb_o))
    masks = (_tri_masks(False), _tri_masks(True))
    cum_lhs = [m[0][:, 0:c].astype(BF16) for m in masks]
    last_row = (c - 1, 0)
    nck = rf.shape[0] // c
    units = [(d, p, ck) for d in range(2) for p in range(npair) for ck in range(nck)]

    def tile(d, p, ck):
        row0 = (nck - 1 - ck if d else ck) * c
        return slice(row0, row0 + c), slice(p * lanes, (p + 1) * lanes)

    def per(fn):
        return [fn(i, d, tile(d, p, ck)) for i, (d, p, ck) in enumerate(units)]

    lw = per(lambda i, d, sl: refs[d][2][sl])
    cum = per(lambda i, d, sl: _mm_mask_lhs(cum_lhs[d], lw[i]))
    tot = per(lambda i, d, sl: jnp.broadcast_to(cum[i][last_row[d]:last_row[d] + 1], (c, lanes)))
    r_s = per(lambda i, d, sl: refs[d][0][sl] * jnp.exp(cum[i]))
    k_s = per(lambda i, d, sl: refs[d][1][sl] * jnp.exp(cum[i] - lw[i]))
    e_neg = per(lambda i, d, sl: jnp.exp(-cum[i]))
    e_end = per(lambda i, d, sl: jnp.exp(tot[i] - cum[i]))
    b = per(lambda i, d, sl: refs[d][3][sl])
    kt = per(lambda i, d, sl: refs[d][4][sl])
    v = per(lambda i, d, sl: refs[d][5][sl])
    v_bd = per(lambda i, d, sl: bd(v[i]))
    kr = per(lambda i, d, sl: jnp.concatenate([k_s[i], r_s[i]], axis=0))
    ab = per(lambda i, d, sl: _mm1(kr[i], bd(b[i] * e_neg[i]), _dot_nt))
    ak = per(lambda i, d, sl: _mm1(kr[i], bd(kt[i] * e_neg[i]), _dot_nt))
    a_b = per(lambda i, d, sl: ab[i][0:c] * masks[d][1])
    b_b = per(lambda i, d, sl: ab[i][c:] * masks[d][0])
    a_kt = per(lambda i, d, sl: ak[i][0:c] * masks[d][1])
    b_kt = per(lambda i, d, sl: ak[i][c:] * masks[d][0])
    dm = per(lambda i, d, sl: eye_pair - a_b[i] * masks[d][2][0])
    for lvl in range(1, len(masks[0][2])):
        zd = per(lambda i, d, sl: _mm1(a_b[i] * masks[d][2][lvl], bd(dm[i])))
        dm = per(lambda i, d, sl: dm[i] - _mm1(dm[i], bd(zd[i])))
    akv = per(lambda i, d, sl: _mm1(jnp.concatenate([a_kt[i], b_kt[i]], axis=0), v_bd[i]))
    w1 = [x[0:c] for x in akv]
    g = per(lambda i, d, sl: _mm1(dm[i], jnp.concatenate([bd(k_s[i]), bd(w1[i])], axis=1)))
    bg = per(lambda i, d, sl: _mm1(b_b[i], jnp.concatenate([bd(g[i][:, 0:lanes]), bd(g[i][:, lanes:])], axis=1)))
    q = per(lambda i, d, sl: r_s[i] - bg[i][:, 0:lanes])
    y0 = per(lambda i, d, sl: akv[i][c:] - bg[i][:, lanes:])
    cg = per(lambda i, d, sl: _mm1(b[i] * e_end[i], g[i], _dot_tn))
    phi = per(lambda i, d, sl: eye_full * jnp.exp(jnp.concatenate([tot[i], tot[i]], axis=0))
              - cg[i][:, 0:lanes] * bd_mask)
    psi = per(lambda i, d, sl: (_mm1(kt[i] * e_end[i], v[i], _dot_tn) - cg[i][:, lanes:]) * bd_mask)
    z = [z_scr[s] for s in range(2 * npair)]
    for ck in range(nck):
        sel = [(i, d, p) for i, (d, p, k2) in enumerate(units) if k2 == ck]
        qz = [_mm1(jnp.concatenate([q[i], phi[i]], axis=0), z[d * npair + p]) for i, d, p in sel]
        y = [x[0:c] + y0[i] for x, (i, d, p) in zip(qz, sel)]
        z = [x[c:] + psi[i] for x, (i, d, p) in zip(qz, sel)]
        for yy, (i, d, p) in zip(y, sel):
            refs[d][6][tile(d, p, ck)] = yy
    for s in range(2 * npair):
        z_scr[s] = z[s]

    @pl.when(last_tab[g_step] == 1)
    def _():
        for d in range(2):
            for p in range(npair):
                zz = z_scr[d * npair + p]
                zfin_o[0, d, 2 * p] = zz[0:HEAD_DIM, 0:HEAD_DIM]
                zfin_o[0, d, 2 * p + 1] = zz[HEAD_DIM:, HEAD_DIM:]


def _scan_tables(lay):
    blk = SCAN_CHUNK * SCAN_CHUNKS_PER_STEP
    rows, blk0, seq0 = [], 0, 0
    for b, t in lay.passes:
        nj = t // blk
        for s in range(b):
            for j in range(nj):
                rows.append((blk0 + s * nj + j, blk0 + s * nj + nj - 1 - j, seq0 + s, int(j == 0), int(j == nj - 1)))
        blk0 += b * nj
        seq0 += b
    return [jnp.asarray(col, jnp.int32) for col in zip(*rows)]


def _rwkv_scan(z0, r, kk, lwf, lwb, bf, bb, ktf, ktb, v, lay):
    blk = SCAN_CHUNK * SCAN_CHUNKS_PER_STEP
    n, dr = r.shape
    tabs = _scan_tables(lay)
    fwd = pl.BlockSpec((blk, dr), lambda g, fw, bw, sq, fi, la: (fw[g], 0))
    bwd = pl.BlockSpec((blk, dr), lambda g, fw, bw, sq, fi, la: (bw[g], 0))
    st = pl.BlockSpec((1, 2, N_HEADS, HEAD_DIM, HEAD_DIM), lambda g, fw, bw, sq, fi, la: (sq[g], 0, 0, 0, 0))
    st_in = pl.BlockSpec((None, 1, 2, N_HEADS, HEAD_DIM, HEAD_DIM),
                         lambda g, fw, bw, sq, fi, la: (lay.layer, sq[g], 0, 0, 0, 0))
    grid_spec = pltpu.PrefetchScalarGridSpec(
        num_scalar_prefetch=len(tabs),
        grid=(int(tabs[0].shape[0]),),
        in_specs=[st_in] + [fwd] * 6 + [bwd] * 6,
        out_specs=[fwd, bwd, st],
        scratch_shapes=[pltpu.VMEM((N_HEADS, V7X_LANES, V7X_LANES), F32)],
    )
    return pl.pallas_call(
        _scan_kernel,
        grid_spec=grid_spec,
        out_shape=[jax.ShapeDtypeStruct((n, dr), F32),
                   jax.ShapeDtypeStruct((n, dr), F32),
                   jax.ShapeDtypeStruct((lay.n_seq, 2, N_HEADS, HEAD_DIM, HEAD_DIM), F32)],
        name="rwkv_scan",
        compiler_params=_params("arbitrary"),
    )(*tabs, z0, r, kk, lwf, bf, ktf, v, r, kk, lwb, bb, ktb, v)


def _softmax_rows(parts):
    m = parts[0].max(axis=-1, keepdims=True)
    for p in parts[1:]:
        m = jnp.maximum(m, p.max(axis=-1, keepdims=True))
    es = [jnp.exp(p - m) for p in parts]
    den = es[0].sum(axis=-1, keepdims=True)
    for e in es[1:]:
        den = den + e.sum(axis=-1, keepdims=True)
    inv = 1.0 / den
    return [e * inv for e in es]


def _ctx_attn_kernel(q_ref, k_ref, v_ref, o_ref):
    scale = HEAD_DIM ** -0.5
    npair = q_ref.shape[1] // V7X_LANES
    tile = lambda ref, p: ref[:, p * V7X_LANES:(p + 1) * V7X_LANES].astype(BF16)
    qb = [tile(q_ref, p) for p in range(npair)]
    k = [tile(k_ref, p) for p in range(npair)]
    v = [tile(v_ref, p) for p in range(npair)]
    first_head = lax.broadcasted_iota(jnp.int32, qb[0].shape, 1) < HEAD_DIM
    head_lanes = (first_head, jnp.logical_not(first_head))
    units = [(p, hh) for p in range(npair) for hh in range(2)]

    def per(fn):
        return [fn(u, p, hh) for u, (p, hh) in enumerate(units)]

    q = per(lambda u, p, hh: jnp.where(head_lanes[hh], qb[p], jnp.zeros_like(qb[p])))
    s = per(lambda u, p, hh: _dot_nt(q[u], k[p]) * scale)
    e = per(lambda u, p, hh: jnp.exp(s[u] - s[u].max(axis=-1, keepdims=True)))
    pr = per(lambda u, p, hh: (e[u] * (1.0 / e[u].sum(axis=-1, keepdims=True))).astype(BF16))
    out = per(lambda u, p, hh: _dot(pr[u], v[p]))
    for p in range(npair):
        o_ref[:, p * V7X_LANES:(p + 1) * V7X_LANES] = jnp.where(first_head, out[2 * p], out[2 * p + 1]).astype(o_ref.dtype)


def _ctx_attention(q, k, v, b_len, t_len):
    n = b_len * t_len
    seq = pl.BlockSpec((t_len, D_BRANCH), lambda b: (b, 0))
    return pl.pallas_call(
        _ctx_attn_kernel,
        grid=(b_len,),
        in_specs=[seq, seq, seq],
        out_specs=seq,
        out_shape=jax.ShapeDtypeStruct((n, D_BRANCH), BF16),
        name="ctx_attn",
        compiler_params=_params("parallel"),
    )(q, k, v)


def _nbr_attn_kernel(q_ref, k_ref, v_ref, kc_ref, vc_ref, bias_ref, o_ref, *, rows, kh):
    scale = HEAD_DIM ** -0.5
    win = kh * GRID_W

    group = 8 if rows % 8 == 0 else 1
    units = [(r, hh) for r in range(group) for hh in range(2)]
    first_head = lax.broadcasted_iota(jnp.int32, (GRID_W, V7X_LANES), 1) < HEAD_DIM
    head_lanes = (first_head, jnp.logical_not(first_head))

    def body(gi, carry):
        def per(fn):
            return [fn(u, r, hh) for u, (r, hh) in enumerate(units)]

        i = [gi * group + r for r in range(group)]
        rs = [jnp.clip(x - kh // 2, 0, rows - kh) for x in i]
        q0 = [pl.multiple_of(x * GRID_W, GRID_W) for x in i]
        k0 = [pl.multiple_of(x * GRID_W, GRID_W) for x in rs]
        qb = [q_ref[pl.ds(x, GRID_W), :].astype(BF16) for x in q0]
        kw = [k_ref[pl.ds(x, win), :].astype(BF16) for x in k0]
        vw = [v_ref[pl.ds(x, win), :].astype(BF16) for x in k0]
        kc = kc_ref[0].astype(BF16)
        vc = vc_ref[0].astype(BF16)
        q = per(lambda u, r, hh: jnp.where(head_lanes[hh], qb[r], jnp.zeros_like(qb[r])))
        s_loc = per(lambda u, r, hh: _dot_nt(q[u], kw[r]) * scale + bias_ref[hh, i[r] - rs[r]])
        s_ctx = per(lambda u, r, hh: _dot_nt(q[u], kc) * scale)
        m = per(lambda u, r, hh: jnp.maximum(s_loc[u].max(axis=-1, keepdims=True), s_ctx[u].max(axis=-1, keepdims=True)))
        e_loc = per(lambda u, r, hh: jnp.exp(s_loc[u] - m[u]))
        e_ctx = per(lambda u, r, hh: jnp.exp(s_ctx[u] - m[u]))
        inv = per(lambda u, r, hh: 1.0 / (e_loc[u].sum(axis=-1, keepdims=True) + e_ctx[u].sum(axis=-1, keepdims=True)))
        out = per(lambda u, r, hh: _dot((e_loc[u] * inv[u]).astype(BF16), vw[r])
                  + _dot((e_ctx[u] * inv[u]).astype(BF16), vc))
        for r in range(group):
            o_ref[pl.ds(q0[r], GRID_W), :] = jnp.where(first_head, out[2 * r], out[2 * r + 1]).astype(o_ref.dtype)
        return carry

    lax.fori_loop(0, rows // group, body, 0)


def _nbr_attention(q, k, v, k_ctx, v_ctx, bias, b_len, t_len, row0):
    n = b_len * t_len
    npair = N_HEADS // 2
    rows = t_len // GRID_W
    kh = min(KH_MAX, rows)
    c_len = k_ctx.shape[1]
    assert row0 % t_len == 0
    seq0 = row0 // t_len
    pair = pl.BlockSpec((t_len, V7X_LANES), lambda b, p: (seq0 + b, p))
    return pl.pallas_call(
        functools.partial(_nbr_attn_kernel, rows=rows, kh=kh),
        grid=(b_len, npair),
        in_specs=[pair, pair, pair,
                  pl.BlockSpec((1, c_len, V7X_LANES), lambda b, p: (b, 0, p)),
                  pl.BlockSpec((1, c_len, V7X_LANES), lambda b, p: (b, 0, p)),
                  pl.BlockSpec((2, kh, GRID_W, kh * GRID_W), lambda b, p: (p, 0, 0, 0))],
        out_specs=pl.BlockSpec((t_len, V7X_LANES), lambda b, p: (b, p)),
        out_shape=jax.ShapeDtypeStruct((n, D_BRANCH), BF16),
        name="nbr_attn",
        compiler_params=_params("parallel", "parallel"),
    )(q, k, v, k_ctx, v_ctx, bias)


def _nbr_bias_table(rpb, rows):
    kh = min(KH_MAX, rows)
    w = np.arange(GRID_W)
    col_start = np.clip(w - KW // 2, 0, GRID_W - KW)
    c = np.arange(GRID_W)
    inside = (c[None, :] >= col_start[:, None]) & (c[None, :] < col_start[:, None] + KW)
    rp = jnp.pad(rpb, ((0, 0), (0, 0), (GRID_W - KW, GRID_W - KW)))
    toep = jnp.stack([rp[:, :, GRID_W - 1 - wi:2 * GRID_W - 1 - wi] for wi in range(GRID_W)], axis=2)
    toep = jnp.where(inside[None, None], toep, NEG_BIG)
    tab = jnp.stack([toep[:, KH_MAX - 1 - dd:KH_MAX - 1 - dd + kh] for dd in range(kh)], axis=1)
    tab = jnp.transpose(tab, (0, 1, 3, 2, 4))
    return tab.reshape(rpb.shape[0], kh, GRID_W, kh * GRID_W)


def _top2_route(logits, n_exp):
    lane = lax.broadcasted_iota(jnp.int32, logits.shape, 1)
    lane_f = lane.astype(F32)
    lg = jnp.where(lane < n_exp, logits, -jnp.inf)
    m1 = lg.max(axis=-1, keepdims=True)
    i1 = jnp.where(lg == m1, lane_f, float(V7X_LANES)).min(axis=-1, keepdims=True)
    lg2 = jnp.where(lane_f == i1, -jnp.inf, lg)
    m2 = lg2.max(axis=-1, keepdims=True)
    i2 = jnp.where(lg2 == m2, lane_f, float(V7X_LANES)).min(axis=-1, keepdims=True)
    e = jnp.exp(m2 - m1)
    g1 = 1.0 / (1.0 + e)
    g2 = e * g1
    out = jnp.where(lane == 0, i1, 0.0)
    out = jnp.where(lane == 1, i2, out)
    out = jnp.where(lane == 2, g1, out)
    return jnp.where(lane == 3, g2, out)


def _mix_kernel(yf_ref, yb_ref, bonus_ref, g_ref, yn0_ref, yn1_ref, gate_ref, x_ref, mod_ref,
                gng_ref, gnb_ref, bd_ref, wur_ref, wun_ref, wo_ref, lng_ref, lnb_ref, *rest,
                alpha, n_exp, nb_first):
    if n_exp:
        rhi_ref, rlo_ref, x1_o, h2_o, route_o = rest
    else:
        x1_o, h2_o = rest
    d = x_ref.shape[1]
    yn = jnp.where(pl.program_id(0) < nb_first, yn0_ref[...], yn1_ref[...])
    bd = bd_ref[...]
    y = yf_ref[...] + yb_ref[...]
    inv_n = 1.0 / HEAD_DIM
    mu = _head_sum(y, bd) * inv_n
    yc = y - mu
    var = _head_sum(yc * yc, bd) * inv_n
    yr = yc * lax.rsqrt(var + GN_EPS) * gng_ref[...] + gnb_ref[...]
    yr = (yr + bonus_ref[...]) * g_ref[...]
    m = (gate_ref[:, 0:d] * _dot(yr.astype(BF16), wur_ref[...])
         + gate_ref[:, d:] * _dot(yn, wun_ref[...]))
    o = _dot(m.astype(BF16), wo_ref[...])
    g1 = mod_ref[0, 2:3, :]
    sh2 = mod_ref[0, 3:4, :]
    sc2 = mod_ref[0, 4:5, :]
    x1 = _layer_norm(alpha * x_ref[...] + g1 * o, lng_ref[...], lnb_ref[...])
    x1_o[...] = x1
    h2 = x1 * (1.0 + sc2) + sh2
    h2_o[...] = h2.astype(BF16)
    if n_exp:
        h_hi = h2.astype(BF16)
        h_lo = (h2 - h_hi.astype(F32)).astype(BF16)
        logits = _dot(h_hi, rhi_ref[...]) + (_dot(h_hi, rlo_ref[...]) + _dot(h_lo, rhi_ref[...]))
        route_o[...] = _top2_route(logits, n_exp)


def _mix_out(yf, yb, bonus, g, yn0, yn1, gates, x, modp, lw, lay, alpha, router):
    n, d = x.shape
    dr = D_BRANCH
    row = lambda i: (i, 0)
    nb_first = yn0.shape[0] // ROW_BLOCK
    yt_spec = pl.BlockSpec((ROW_BLOCK, dr), row)
    consts = [lw["gn_g"], lw["gn_b"], lw["bd"], lw["w_up_r"], lw["w_up_n"], lw["w_o"], lw["ln1_g"], lw["ln1_b"]]
    n_exp = 0
    out_specs = [pl.BlockSpec((ROW_BLOCK, d), row), pl.BlockSpec((ROW_BLOCK, d), row)]
    out_shape = [jax.ShapeDtypeStruct((n, d), F32), jax.ShapeDtypeStruct((n, d), BF16)]
    if router is not None:
        n_exp = router.shape[1]
        r_pad = jnp.pad(router, ((0, 0), (0, V7X_LANES - n_exp)))
        r_hi = r_pad.astype(BF16)
        consts += [r_hi, (r_pad - r_hi.astype(F32)).astype(BF16)]
        out_specs.append(pl.BlockSpec((ROW_BLOCK, V7X_LANES), row))
        out_shape.append(jax.ShapeDtypeStruct((n, V7X_LANES), F32))
    const_specs, const_ops = _param_specs(consts)
    return pl.pallas_call(
        functools.partial(_mix_kernel, alpha=alpha, n_exp=n_exp, nb_first=nb_first),
        grid=(n // ROW_BLOCK,),
        in_specs=[yt_spec, yt_spec,
                  pl.BlockSpec((ROW_BLOCK, dr), row), pl.BlockSpec((ROW_BLOCK, dr), row),
                  pl.BlockSpec((ROW_BLOCK, dr), lambda i: (jnp.minimum(i, nb_first - 1), 0)),
                  pl.BlockSpec((ROW_BLOCK, dr), lambda i: (jnp.maximum(i - nb_first, 0), 0)),
                  pl.BlockSpec((ROW_BLOCK, 2 * d), row),
                  pl.BlockSpec((ROW_BLOCK, d), row),
                  lay.mod_spec(d)]
                 + const_specs,
        out_specs=out_specs,
        out_shape=out_shape,
        name="mix_out",
        compiler_params=_params("parallel"),
    )(yf, yb, bonus, g, yn0, yn1, gates, x, modp, *const_ops)


def _swiglu(h, w1, w3, w2):
    u = _dot(h, w1)
    act = u * _sigmoid(u) * _dot(h, w3)
    return _dot(act.astype(BF16), w2)


def _ffn_kernel(h_ref, x1_ref, mod_ref, w1_ref, w3_ref, w2_ref, lng_ref, lnb_ref, o_ref, *, alpha):
    f = _swiglu(h_ref[...], w1_ref[...], w3_ref[...], w2_ref[...])
    g2 = mod_ref[0, 5:6, :]
    o_ref[...] = _layer_norm(alpha * x1_ref[...] + g2 * f, lng_ref[...], lnb_ref[...])


def _ffn_dense(h2, x1, modp, w1, w3, w2, ln_g, ln_b, lay, alpha):
    n, d = x1.shape
    row = lambda i: (i, 0)
    const_specs, const_ops = _param_specs([w1, w3, w2, ln_g, ln_b])
    return pl.pallas_call(
        functools.partial(_ffn_kernel, alpha=alpha),
        grid=(n // ROW_BLOCK,),
        in_specs=[pl.BlockSpec((ROW_BLOCK, d), row), pl.BlockSpec((ROW_BLOCK, d), row), lay.mod_spec(d)]
                 + const_specs,
        out_specs=pl.BlockSpec((ROW_BLOCK, d), row),
        out_shape=jax.ShapeDtypeStruct((n, d), F32),
        name="ffn_dense",
        compiler_params=_params("parallel"),
    )(h2, x1, modp, *const_ops)


def _moe_kernel(be_ref, nb_ref, x_ref, w1_ref, w3_ref, w2_ref, o_ref):
    i = pl.program_id(0)

    @pl.when(i < nb_ref[0])
    def _():
        o_ref[...] = _swiglu(x_ref[...], w1_ref[0], w3_ref[0], w2_ref[0])

    @pl.when(i >= nb_ref[0])
    def _():
        o_ref[...] = jnp.zeros(o_ref.shape, F32)


def _moe_experts(block_e, n_used, xb, w1, w3, w2, lm):
    p, d = xb.shape
    dff = w1.shape[3]
    nblk = p // MOE_ROWS
    one = pl.Buffered(1)
    expert = lambda i, be, nb: (lm, be[i], 0, 0)
    grid_spec = pltpu.PrefetchScalarGridSpec(
        num_scalar_prefetch=2,
        grid=(nblk,),
        in_specs=[pl.BlockSpec((MOE_ROWS, d), lambda i, be, nb: (i, 0)),
                  pl.BlockSpec((None, 1, d, dff), expert, pipeline_mode=one),
                  pl.BlockSpec((None, 1, d, dff), expert, pipeline_mode=one),
                  pl.BlockSpec((None, 1, dff, d), expert, pipeline_mode=one)],
        out_specs=pl.BlockSpec((MOE_ROWS, d), lambda i, be, nb: (i, 0)),
    )
    return pl.pallas_call(
        _moe_kernel,
        grid_spec=grid_spec,
        out_shape=jax.ShapeDtypeStruct((p, d), F32),
        name="moe_experts",
        compiler_params=_params("arbitrary"),
    )(block_e, n_used, xb, w1, w3, w2)


def _combine_kernel(ya_ref, yb_ref, route_ref, x1_ref, mod_ref, lng_ref, lnb_ref, o_ref, *, alpha):
    f = route_ref[:, 2:3] * ya_ref[...] + route_ref[:, 3:4] * yb_ref[...]
    g2 = mod_ref[0, 5:6, :]
    o_ref[...] = _layer_norm(alpha * x1_ref[...] + g2 * f, lng_ref[...], lnb_ref[...])


def _moe_combine(ya, yb, route, x1, modp, ln_g, ln_b, lay, alpha):
    n, d = x1.shape
    row = lambda i: (i, 0)
    return pl.pallas_call(
        functools.partial(_combine_kernel, alpha=alpha),
        grid=(n // ROW_BLOCK,),
        in_specs=[pl.BlockSpec((ROW_BLOCK, d), row), pl.BlockSpec((ROW_BLOCK, d), row),
                  pl.BlockSpec((ROW_BLOCK, V7X_LANES), row), pl.BlockSpec((ROW_BLOCK, d), row),
                  lay.mod_spec(d),
                  _const_spec(ln_g.shape), _const_spec(ln_b.shape)],
        out_specs=pl.BlockSpec((ROW_BLOCK, d), row),
        out_shape=jax.ShapeDtypeStruct((n, d), F32),
        name="moe_combine",
        compiler_params=_params("parallel"),
    )(ya, yb, route, x1, modp, ln_g, ln_b)


def _moe(h2, route, x1, modp, w1, w3, w2, lm, ln_g, ln_b, lay, alpha):
    n, d = x1.shape
    n_exp = w1.shape[1]
    top_e = route[:, 0:2].astype(jnp.int32)
    flat_e = top_e.reshape(-1)
    onehot = (flat_e[:, None] == jnp.arange(n_exp, dtype=jnp.int32)[None, :]).astype(jnp.int32)
    rank = jnp.sum((jnp.cumsum(onehot, axis=0) - onehot) * onehot, axis=1)
    counts = jnp.sum(onehot, axis=0)
    padded = (counts + MOE_ROWS - 1) // MOE_ROWS * MOE_ROWS
    pad_end = jnp.cumsum(padded)
    pad_start = pad_end - padded
    dest = pad_start[flat_e] + rank
    nblk = -(-(2 * n + n_exp * (MOE_ROWS - 1)) // MOE_ROWS)
    p = nblk * MOE_ROWS
    flat_tok = jnp.repeat(jnp.arange(n, dtype=jnp.int32), 2)
    buf_tok = jnp.full((p,), n, jnp.int32).at[dest].set(flat_tok)
    blk_start = jnp.arange(nblk, dtype=jnp.int32) * MOE_ROWS
    block_e = jnp.minimum(jnp.sum((pad_end[None, :] <= blk_start[:, None]).astype(jnp.int32), axis=1), n_exp - 1)
    n_used = (pad_end[-1] // MOE_ROWS).astype(jnp.int32).reshape(1)
    xb = jnp.concatenate([h2, jnp.zeros((1, d), h2.dtype)], axis=0)[buf_tok]
    yb = _moe_experts(block_e, n_used, xb, w1, w3, w2, lm)
    dest2 = dest.reshape(n, 2)
    return _moe_combine(yb[dest2[:, 0]], yb[dest2[:, 1]], route, x1, modp, ln_g, ln_b, lay, alpha)


def _layer(x, modp, lw, l, lay, alpha, z0, ctx_kv):
    (bc, tc_len), (bs, ts_len) = lay.passes
    n_ctx = bc * tc_len
    slab, qn, kn, vn, k_new, v_new, gates = _in_proj(x, modp, lw["w_in"], l, lay, lw["n_slab"])
    r, kk, lwf, lwb, bf, bb, ktf, ktb, v, bonus, g = _rwkv_prep(slab, lw, lay)
    yf, yb, z_fin = _rwkv_scan(z0, r, kk, lwf, lwb, bf, bb, ktf, ktb, v, lay)
    yn_ctx = _ctx_attention(qn, kn, vn, bc, tc_len)
    yn_lat = _nbr_attention(qn, kn, vn, ctx_kv[0], ctx_kv[1], lw["bias"], bs, ts_len, n_ctx)
    outs = _mix_out(yf, yb, bonus, g, yn_ctx, yn_lat, gates, x, modp, lw, lay, alpha, lw.get("router"))
    if "router" in lw:
        x1, h2, route = outs
        x2 = _moe(h2, route, x1, modp, lw["moe_w1"], lw["moe_w3"], lw["moe_w2"], l // 2, lw["ln2_g"], lw["ln2_b"],
                  lay, alpha)
    else:
        x1, h2 = outs
        x2 = _ffn_dense(h2, x1, modp, (lw["ffn_w1"], l // 2), (lw["ffn_w3"], l // 2), (lw["ffn_w2"], l // 2),
                        lw["ln2_g"], lw["ln2_b"], lay, alpha)
    return x2, z_fin, k_new, v_new


def kernel(x_prompt, x_sample, c, state_rwkv, cache_k, cache_v, c_ctx, w_ada, b_ada, w_in, ts_w, rw_w0, rw_w2, rw_a0, rw_a2, rw_g2, rw_kk, rw_ka, rw_u, rw_gn_g, rw_gn_b, na_rpb, w_up_r, w_up_n, w_o, ln1_g, ln1_b, ln2_g, ln2_b, ffn_w1, ffn_w3, ffn_w2, moe_router, moe_w1, moe_w3, moe_w2):
    depth = w_in.shape[0]
    bc, tc_len, d = x_prompt.shape
    bs, ts_len, _ = x_sample.shape
    dr = D_BRANCH
    alpha = float((2 * depth) ** 0.25)
    n_slab = ts_w.shape[2]
    lora_w = rw_w2.shape[2]
    lora_a = rw_a2.shape[2]

    cvec = jnp.zeros((8, d), F32).at[0].set(c_ctx).at[1:1 + bs].set(c)
    mod = _ada_mod(cvec, w_ada, b_ada)

    half = np.kron(np.eye(2, dtype=np.float32), np.ones((HEAD_DIM, HEAD_DIM), np.float32))
    bd = jnp.asarray(half, BF16)
    rows = ts_len // GRID_W

    passes = ((bc, tc_len), (bs, ts_len))
    n_ctx = bc * tc_len
    x = jnp.concatenate([x_prompt.reshape(n_ctx, d), x_sample.reshape(bs * ts_len, d)], axis=0)
    seq_row = np.concatenate([np.zeros(bc, np.int32), 1 + np.arange(bs, dtype=np.int32)])
    modp = mod[:, seq_row].reshape(depth, bc + bs, 6, d)
    z0_all = jnp.concatenate([jnp.zeros((depth, bc, 2, N_HEADS, HEAD_DIM, HEAD_DIM), F32),
                              jnp.swapaxes(jnp.swapaxes(state_rwkv, 0, 1), -1, -2)], axis=1)
    w_in_b, w_up_r_b, w_up_n_b, w_o_b = (w.astype(BF16) for w in (w_in, w_up_r, w_up_n, w_o))
    ffn_b = [w.astype(BF16) for w in (ffn_w1, ffn_w3, ffn_w2)]
    moe_b = [w.astype(BF16) for w in (moe_w1, moe_w3, moe_w2)]
    new_s, new_k, new_v = [], [], []
    for l in range(depth):
        w2blk = jnp.zeros((2 * lora_w, 2 * dr), F32).at[:lora_w, :dr].set(rw_w2[l, 0]).at[lora_w:, dr:].set(rw_w2[l, 1])
        a2blk = jnp.zeros((2 * lora_a, 2 * dr), F32).at[:lora_a, :dr].set(rw_a2[l, 0]).at[lora_a:, dr:].set(rw_a2[l, 1])
        lw = dict(
            n_slab=n_slab,
            w_in=w_in_b, ts=ts_w[l],
            w0=rw_w0[l].reshape(1, 2 * dr), w2=w2blk.astype(BF16),
            a0=rw_a0[l].reshape(1, 2 * dr), a2=a2blk.astype(BF16),
            g2=rw_g2[l].astype(BF16), kkw=rw_kk[l].reshape(1, dr), ka=rw_ka[l].reshape(1, dr),
            u=rw_u[l].reshape(1, dr), bd=bd,
            gn_g=rw_gn_g[l].reshape(1, dr), gn_b=rw_gn_b[l].reshape(1, dr),
            w_up_r=(w_up_r_b, l), w_up_n=(w_up_n_b, l), w_o=(w_o_b, l),
            ln1_g=ln1_g[l].reshape(1, d), ln1_b=ln1_b[l].reshape(1, d),
            ln2_g=ln2_g[l].reshape(1, d), ln2_b=ln2_b[l].reshape(1, d),
            bias=_nbr_bias_table(na_rpb[l], rows),
        )
        if l % 2 == 0:
            lw.update(ffn_w1=ffn_b[0], ffn_w3=ffn_b[1], ffn_w2=ffn_b[2])
        else:
            lw.update(router=moe_router[l // 2], moe_w1=moe_b[0], moe_w3=moe_b[1], moe_w2=moe_b[2])

        kv = (cache_k[:, l].reshape(bs, -1, dr), cache_v[:, l].reshape(bs, -1, dr))
        x, z_fin, k_c, v_c = _layer(x, modp, lw, l, _Layout(passes, l), alpha, z0_all, kv)
        new_s.append(jnp.swapaxes(z_fin[:bc], -1, -2))
        new_k.append(k_c.reshape(bc, tc_len, N_HEADS, HEAD_DIM))
        new_v.append(v_c.reshape(bc, tc_len, N_HEADS, HEAD_DIM))

    return (x[:n_ctx].reshape(bc, tc_len, d), x[n_ctx:].reshape(bs, ts_len, d),
            jnp.stack(new_s, axis=1), jnp.stack(new_k, axis=1), jnp.stack(new_v, axis=1))
```

```python
import functools

import numpy as np
import jax
import jax.numpy as jnp
from jax import lax
from jax.experimental import pallas as pl
from jax.experimental.pallas import tpu as pltpu

F32 = jnp.float32
BF16 = jnp.bfloat16
HIGHEST = lax.Precision.HIGHEST

N_HEADS = 8
HEAD_DIM = 64
D_BRANCH = N_HEADS * HEAD_DIM
V7X_LANES = 128
ROW_BLOCK = 256
SCAN_CHUNK = 64
SCAN_CHUNKS_PER_STEP = 2
MOE_ROWS = 256
KH_MAX = 8
KW = 16
GRID_W = 64
LN_EPS = 1e-5
GN_EPS = 64e-5
NEG_BIG = -1e30
VMEM_LIMIT = 56 * 1024 * 1024


def _params(*sem):
    return pltpu.CompilerParams(dimension_semantics=sem, vmem_limit_bytes=VMEM_LIMIT)


def _const_spec(shape):
    nd = len(shape)
    return pl.BlockSpec(shape, lambda *_: (0,) * nd, pipeline_mode=pl.Buffered(1))


def _layer_spec(stack, l):
    nd = stack.ndim - 1
    return pl.BlockSpec((None,) + stack.shape[1:], lambda *_: (l,) + (0,) * nd, pipeline_mode=pl.Buffered(1))


def _param_specs(params):
    specs = [_layer_spec(*p) if isinstance(p, tuple) else _const_spec(p.shape) for p in params]
    return specs, [p[0] if isinstance(p, tuple) else p for p in params]


class _Layout:
    def __init__(self, passes, layer=0):
        self.passes = passes
        self.layer = layer
        self.n_seq = sum(b for b, _ in passes)

    def _per_block(self, fn):
        def at(i):
            out, blk0, seq0 = None, 0, 0
            for b, t in self.passes:
                per = t // ROW_BLOCK
                val = fn(i - blk0, per, seq0)
                out = val if out is None else jnp.where(i >= blk0, val, out)
                blk0 += b * per
                seq0 += b
            return out
        return at

    def seq_of_block(self, i):
        return self._per_block(lambda j, per, s0: s0 + j // per)(i)

    def is_first(self, i):
        return self._per_block(lambda j, per, s0: (j % per) == 0)(i)

    def is_last(self, i):
        return self._per_block(lambda j, per, s0: (j % per) == per - 1)(i)

    def mod_spec(self, d):
        return pl.BlockSpec((None, 1, 6, d), lambda i: (self.layer, self.seq_of_block(i), 0, 0))


def _dot(a, b):
    return jnp.dot(a, b, preferred_element_type=F32)


def _dot_nt(a, b):
    return lax.dot_general(a, b, (((1,), (1,)), ((), ())), preferred_element_type=F32)


def _dot_tn(a, b):
    return lax.dot_general(a, b, (((0,), (0,)), ((), ())), preferred_element_type=F32)


def _mm1(a, b, f=_dot):
    return f(a.astype(BF16), b.astype(BF16))


def _mm_mask_lhs(m, x):
    hi = x.astype(BF16)
    lo = (x - hi.astype(F32)).astype(BF16)
    return _dot(m, hi) + _dot(m, lo)


def _sigmoid(x):
    return 1.0 / (1.0 + jnp.exp(-x))


def _head_sum(x, bd):
    outs = []
    for j in range(x.shape[1] // V7X_LANES):
        xs = x[:, j * V7X_LANES:(j + 1) * V7X_LANES]
        hi = xs.astype(BF16)
        lo = (xs - hi.astype(F32)).astype(BF16)
        outs.append(_dot(hi, bd) + _dot(lo, bd))
    return jnp.concatenate(outs, axis=1)


def _layer_norm(z, g, b):
    mu = jnp.mean(z, axis=-1, keepdims=True)
    zc = z - mu
    var = jnp.mean(zc * zc, axis=-1, keepdims=True)
    return zc * lax.rsqrt(var + LN_EPS) * g + b


def _ada_kernel(c_ref, w_ref, b_ref, o_ref):
    cs = c_ref[...]
    s = cs * _sigmoid(cs)
    o_ref[0] = jnp.dot(s, w_ref[0], preferred_element_type=F32, precision=HIGHEST) + b_ref[0]


def _ada_mod(cvec, w_ada, b_ada):
    depth, d, n6 = w_ada.shape
    tn = n6 // 4
    return pl.pallas_call(
        _ada_kernel,
        grid=(depth, n6 // tn),
        in_specs=[pl.BlockSpec((8, d), lambda l, j: (0, 0)),
                  pl.BlockSpec((1, d, tn), lambda l, j: (l, 0, j)),
                  pl.BlockSpec((1, 1, tn), lambda l, j: (l, 0, j))],
        out_specs=pl.BlockSpec((1, 8, tn), lambda l, j: (l, 0, j)),
        out_shape=jax.ShapeDtypeStruct((depth, 8, n6), F32),
        name="ada_mod",
        compiler_params=_params("parallel", "parallel"),
    )(cvec, w_ada, b_ada.reshape(depth, 1, n6))


def _inproj_kernel(x_ref, mod_ref, w_ref, kbuf_ref, vbuf_ref, slab_ref, q_ref, k_ref, v_ref, kc_ref, vc_ref, gate_ref,
                   *, n_slab, nb_cache):
    del kbuf_ref, vbuf_ref
    sh = mod_ref[0, 0:1, :]
    sc = mod_ref[0, 1:2, :]
    h = (x_ref[...] * (1.0 + sc) + sh).astype(BF16)
    dn = D_BRANCH
    slab_ref[...] = _dot(h, w_ref[:, 0:n_slab])
    q_ref[...] = _dot(h, w_ref[:, n_slab:n_slab + dn]).astype(q_ref.dtype)
    k = _dot(h, w_ref[:, n_slab + dn:n_slab + 2 * dn])
    v = _dot(h, w_ref[:, n_slab + 2 * dn:n_slab + 3 * dn])
    k_ref[...] = k.astype(k_ref.dtype)
    v_ref[...] = v.astype(v_ref.dtype)

    @pl.when(pl.program_id(0) < nb_cache)
    def _():
        kc_ref[0] = k.T
        vc_ref[0] = v.T

    gate_ref[...] = _sigmoid(_dot(h, w_ref[:, n_slab + 3 * dn:]))


def _in_proj(x, modp, w_in, l, lay, n_slab, cache_k, cache_v):
    n, d = x.shape
    n_cols = w_in.shape[2]
    dn = D_BRANCH
    n_gate = n_cols - n_slab - 3 * dn
    b_cache, t_cache = lay.passes[0]
    per_seq = t_cache // ROW_BLOCK
    nb_cache = b_cache * per_seq
    row = lambda i: (i, 0)

    def cache_block(i):
        ic = jnp.minimum(i, nb_cache - 1)
        return ic // per_seq, l, 0, ic % per_seq

    cache_spec = pl.BlockSpec((1, None, dn, ROW_BLOCK), cache_block)
    any_spec = pl.BlockSpec(memory_space=pl.ANY)
    return pl.pallas_call(
        functools.partial(_inproj_kernel, n_slab=n_slab, nb_cache=nb_cache),
        grid=(n // ROW_BLOCK,),
        in_specs=[pl.BlockSpec((ROW_BLOCK, d), row),
                  lay.mod_spec(d),
                  _layer_spec(w_in, l),
                  any_spec, any_spec],
        out_specs=[pl.BlockSpec((ROW_BLOCK, n_slab), row),
                   pl.BlockSpec((ROW_BLOCK, dn), row), pl.BlockSpec((ROW_BLOCK, dn), row),
                   pl.BlockSpec((ROW_BLOCK, dn), row),
                   cache_spec, cache_spec,
                   pl.BlockSpec((ROW_BLOCK, n_gate), row)],
        out_shape=[jax.ShapeDtypeStruct((n, n_slab), F32),
                   jax.ShapeDtypeStruct((n, dn), BF16), jax.ShapeDtypeStruct((n, dn), BF16),
                   jax.ShapeDtypeStruct((n, dn), BF16),
                   jax.ShapeDtypeStruct(cache_k.shape, F32), jax.ShapeDtypeStruct(cache_v.shape, F32),
                   jax.ShapeDtypeStruct((n, n_gate), F32)],
        input_output_aliases={3: 4, 4: 5},
        name="in_proj",
        compiler_params=_params("arbitrary"),
    )(x, modp, w_in, cache_k, cache_v)


def _prep_kernel(cur_ref, prev_ref, next_ref, ts_ref, w0_ref, w2_ref, a0_ref, a2_ref, g2_ref,
                 kkw_ref, ka_ref, u_ref, bd_ref,
                 r_o, kk_o, lwf_o, lwb_o, bf_o, bb_o, ktf_o, ktb_o, v_o, bonus_o, g_o,
                 *, lay, lora_w, lora_a):
    i = pl.program_id(0)
    tb, n_slab = cur_ref.shape
    dr = D_BRANCH
    cur = cur_ref[...]
    first = lay.is_first(i)
    last = lay.is_last(i)
    prow = jnp.where(first, 0.0, prev_ref[7:8, :])
    nrow = jnp.where(last, 0.0, next_ref[0:1, :])
    rid = lax.broadcasted_iota(jnp.int32, (tb, n_slab), 0)
    xp = jnp.where(rid == 0, prow, pltpu.roll(cur, 1, 0))
    xn = jnp.where(rid == tb - 1, nrow, pltpu.roll(cur, tb - 1, 0))
    s = ts_ref[0:1, :] * xp + ts_ref[1:2, :] * cur + ts_ref[2:3, :] * xn

    r = s[:, 0:dr]
    k = s[:, dr:2 * dr]
    v = s[:, 2 * dr:3 * dr]
    o = 3 * dr
    wd = s[:, o:o + 2 * lora_w]
    ad = s[:, o + 2 * lora_w:o + 2 * lora_w + 2 * lora_a]
    gd = s[:, o + 2 * lora_w + 2 * lora_a:]

    wl = w0_ref[...] + _dot(jnp.tanh(wd).astype(BF16), w2_ref[...])
    logw = -float(np.exp(-0.5)) * _sigmoid(wl)
    a = _sigmoid(a0_ref[...] + _dot(ad.astype(BF16), a2_ref[...]))
    g = _dot(_sigmoid(gd).astype(BF16), g2_ref[...])

    bd = bd_ref[...]
    kk = k * kkw_ref[...]
    kk = kk * lax.rsqrt(_head_sum(kk * kk, bd) + 1e-12)
    ka = ka_ref[...]
    a_f = a[:, :dr]
    a_b = a[:, dr:]
    kt_f = k * (1.0 + (a_f - 1.0) * ka)
    kt_b = k * (1.0 + (a_b - 1.0) * ka)
    bonus = _head_sum(r * (0.5 * (kt_f + kt_b)) * u_ref[...], bd) * v

    r_o[...] = r
    kk_o[...] = kk
    lwf_o[...] = logw[:, :dr]
    lwb_o[...] = logw[:, dr:]
    bf_o[...] = kk * a_f
    bb_o[...] = kk * a_b
    ktf_o[...] = kt_f
    ktb_o[...] = kt_b
    v_o[...] = v
    bonus_o[...] = bonus
    g_o[...] = g


def _rwkv_prep(slab, lw, lay):
    n, n_slab = slab.shape
    dr = D_BRANCH
    sub = ROW_BLOCK // 8
    lora_w = lw["w2"].shape[0] // 2
    lora_a = lw["a2"].shape[0] // 2
    row = lambda i: (i, 0)
    consts = [lw["ts"], lw["w0"], lw["w2"], lw["a0"], lw["a2"], lw["g2"], lw["kkw"], lw["ka"], lw["u"], lw["bd"]]
    outs = pl.pallas_call(
        functools.partial(_prep_kernel, lay=lay, lora_w=lora_w, lora_a=lora_a),
        grid=(n // ROW_BLOCK,),
        in_specs=[pl.BlockSpec((ROW_BLOCK, n_slab), row),
                  pl.BlockSpec((8, n_slab), lambda i: (jnp.maximum(i * sub - 1, 0), 0)),
                  pl.BlockSpec((8, n_slab), lambda i: (jnp.minimum((i + 1) * sub, n // 8 - 1), 0))]
                 + [_const_spec(c.shape) for c in consts],
        out_specs=[pl.BlockSpec((ROW_BLOCK, dr), row)] * 11,
        out_shape=[jax.ShapeDtypeStruct((n, dr), F32)] * 11,
        name="rwkv_prep",
        compiler_params=_params("parallel"),
    )(slab, slab, slab, *consts)
    return outs


def _tri_masks(rev):
    c = SCAN_CHUNK
    ti = lax.broadcasted_iota(jnp.int32, (c, 2 * c), 0)
    ii = lax.broadcasted_iota(jnp.int32, (c, 2 * c), 1) % c
    if rev:
        ti, ii = c - 1 - ti, c - 1 - ii
    one = lambda m: jnp.where(m, 1.0, 0.0)
    levels = []
    s = 1
    while s < c:
        same = (ti // (2 * s)) == (ii // (2 * s))
        levels.append(one(same & ((ti % (2 * s)) >= s) & ((ii % (2 * s)) < s)))
        s *= 2
    return one(ii <= ti), one(ii < ti), levels


def _scan_kernel(fwd_tab, bwd_tab, seq_tab, first_tab, last_tab,
                 z0_ref, rf, kkf, lwf, bf, ktf, vf, rb, kkb, lwb, bb, ktb, vb, sbuf_ref, yf_o, yb_o, sfin_o, z_scr,
                 *, n_state_out):
    del sbuf_ref
    g_step = pl.program_id(0)
    c = SCAN_CHUNK
    lanes = V7X_LANES
    npair = N_HEADS // 2
    assert c == HEAD_DIM and 2 * HEAD_DIM == lanes

    row_blk = lax.broadcasted_iota(jnp.int32, (lanes, lanes), 0) // HEAD_DIM
    col_blk = lax.broadcasted_iota(jnp.int32, (lanes, lanes), 1) // HEAD_DIM
    bd_mask = jnp.where(row_blk == col_blk, 1.0, 0.0)
    eye_full = jnp.where(lax.broadcasted_iota(jnp.int32, (lanes, lanes), 0)
                         == lax.broadcasted_iota(jnp.int32, (lanes, lanes), 1), 1.0, 0.0)
    lane_c = lax.broadcasted_iota(jnp.int32, (c, lanes), 1)
    lo = jnp.where(lane_c < HEAD_DIM, 1.0, 0.0)
    hi = 1.0 - lo
    eye_pair = jnp.where(lax.broadcasted_iota(jnp.int32, (c, lanes), 0) == lane_c % HEAD_DIM, 1.0, 0.0)

    def bd(x):
        return jnp.concatenate([x * lo, x * hi], axis=0)

    @pl.when(first_tab[g_step] == 1)
    def _():
        for d in range(2):
            for p in range(npair):
                za = jnp.concatenate([z0_ref[0, d, 2 * p], z0_ref[0, d, 2 * p + 1]], axis=1)
                z_scr[d * npair + p] = bd(za)

    refs = ((rf, kkf, lwf, bf, ktf, vf, yf_o), (rb, kkb, lwb, bb, ktb, vb, yb_o))
    masks = (_tri_masks(False), _tri_masks(True))
    cum_lhs = [m[0][:, 0:c].astype(BF16) for m in masks]
    last_row = (c - 1, 0)
    nck = rf.shape[0] // c
    units = [(d, p, ck) for d in range(2) for p in range(npair) for ck in range(nck)]

    def tile(d, p, ck):
        row0 = (nck - 1 - ck if d else ck) * c
        return slice(row0, row0 + c), slice(p * lanes, (p + 1) * lanes)

    def per(fn):
        return [fn(i, d, tile(d, p, ck)) for i, (d, p, ck) in enumerate(units)]

    lw = per(lambda i, d, sl: refs[d][2][sl])
    cum = per(lambda i, d, sl: _mm_mask_lhs(cum_lhs[d], lw[i]))
    tot = per(lambda i, d, sl: jnp.broadcast_to(cum[i][last_row[d]:last_row[d] + 1], (c, lanes)))
    r_s = per(lambda i, d, sl: refs[d][0][sl] * jnp.exp(cum[i]))
    k_s = per(lambda i, d, sl: refs[d][1][sl] * jnp.exp(cum[i] - lw[i]))
    e_neg = per(lambda i, d, sl: jnp.exp(-cum[i]))
    e_end = per(lambda i, d, sl: jnp.exp(tot[i] - cum[i]))
    b = per(lambda i, d, sl: refs[d][3][sl])
    kt = per(lambda i, d, sl: refs[d][4][sl])
    v = per(lambda i, d, sl: refs[d][5][sl])
    v_bd = per(lambda i, d, sl: bd(v[i]))
    kr = per(lambda i, d, sl: jnp.concatenate([k_s[i], r_s[i]], axis=0))
    ab = per(lambda i, d, sl: _mm1(kr[i], bd(b[i] * e_neg[i]), _dot_nt))
    ak = per(lambda i, d, sl: _mm1(kr[i], bd(kt[i] * e_neg[i]), _dot_nt))
    a_b = per(lambda i, d, sl: ab[i][0:c] * masks[d][1])
    b_b = per(lambda i, d, sl: ab[i][c:] * masks[d][0])
    a_kt = per(lambda i, d, sl: ak[i][0:c] * masks[d][1])
    b_kt = per(lambda i, d, sl: ak[i][c:] * masks[d][0])
    dm = per(lambda i, d, sl: eye_pair - a_b[i] * masks[d][2][0])
    for lvl in range(1, len(masks[0][2])):
        zd = per(lambda i, d, sl: _mm1(a_b[i] * masks[d][2][lvl], bd(dm[i])))
        dm = per(lambda i, d, sl: dm[i] - _mm1(dm[i], bd(zd[i])))
    akv = per(lambda i, d, sl: _mm1(jnp.concatenate([a_kt[i], b_kt[i]], axis=0), v_bd[i]))
    w1 = [x[0:c] for x in akv]
    g = per(lambda i, d, sl: _mm1(dm[i], jnp.concatenate([bd(k_s[i]), bd(w1[i])], axis=1)))
    bg = per(lambda i, d, sl: _mm1(b_b[i], jnp.concatenate([bd(g[i][:, 0:lanes]), bd(g[i][:, lanes:])], axis=1)))
    q = per(lambda i, d, sl: r_s[i] - bg[i][:, 0:lanes])
    y0 = per(lambda i, d, sl: akv[i][c:] - bg[i][:, lanes:])
    cg = per(lambda i, d, sl: _mm1(b[i] * e_end[i], g[i], _dot_tn))
    phi = per(lambda i, d, sl: eye_full * jnp.exp(jnp.concatenate([tot[i], tot[i]], axis=0))
              - cg[i][:, 0:lanes] * bd_mask)
    psi = per(lambda i, d, sl: (_mm1(kt[i] * e_end[i], v[i], _dot_tn) - cg[i][:, lanes:]) * bd_mask)
    z = [z_scr[s] for s in range(2 * npair)]
    for ck in range(nck):
        sel = [(i, d, p) for i, (d, p, k2) in enumerate(units) if k2 == ck]
        qz = [_mm1(jnp.concatenate([q[i], phi[i]], axis=0), z[d * npair + p]) for i, d, p in sel]
        y = [x[0:c] + y0[i] for x, (i, d, p) in zip(qz, sel)]
        z = [x[c:] + psi[i] for x, (i, d, p) in zip(qz, sel)]
        for yy, (i, d, p) in zip(y, sel):
            refs[d][6][tile(d, p, ck)] = yy
    for s in range(2 * npair):
        z_scr[s] = z[s]

    @pl.when(jnp.logical_and(last_tab[g_step] == 1, seq_tab[g_step] < n_state_out))
    def _():
        for d in range(2):
            for p in range(npair):
                zt = z_scr[d * npair + p].T
                sfin_o[0, d, 2 * p] = zt[0:HEAD_DIM, 0:HEAD_DIM]
                sfin_o[0, d, 2 * p + 1] = zt[HEAD_DIM:, HEAD_DIM:]


def _scan_tables(lay):
    blk = SCAN_CHUNK * SCAN_CHUNKS_PER_STEP
    rows, blk0, seq0 = [], 0, 0
    for b, t in lay.passes:
        nj = t // blk
        for s in range(b):
            for j in range(nj):
                rows.append((blk0 + s * nj + j, blk0 + s * nj + nj - 1 - j, seq0 + s, int(j == 0), int(j == nj - 1)))
        blk0 += b * nj
        seq0 += b
    return [jnp.asarray(col, jnp.int32) for col in zip(*rows)]


def _rwkv_scan(z0, r, kk, lwf, lwb, bf, bb, ktf, ktb, v, lay, new_state):
    blk = SCAN_CHUNK * SCAN_CHUNKS_PER_STEP
    n, dr = r.shape
    n_out = new_state.shape[0]
    assert n_out == lay.passes[0][0]
    tabs = _scan_tables(lay)
    fwd = pl.BlockSpec((blk, dr), lambda g, fw, bw, sq, fi, la: (fw[g], 0))
    bwd = pl.BlockSpec((blk, dr), lambda g, fw, bw, sq, fi, la: (bw[g], 0))
    st_out = pl.BlockSpec((1, None, 2, N_HEADS, HEAD_DIM, HEAD_DIM),
                          lambda g, fw, bw, sq, fi, la: (jnp.minimum(sq[g], n_out - 1), lay.layer, 0, 0, 0, 0))
    st_in = pl.BlockSpec((None, 1, 2, N_HEADS, HEAD_DIM, HEAD_DIM),
                         lambda g, fw, bw, sq, fi, la: (lay.layer, sq[g], 0, 0, 0, 0))
    grid_spec = pltpu.PrefetchScalarGridSpec(
        num_scalar_prefetch=len(tabs),
        grid=(int(tabs[0].shape[0]),),
        in_specs=[st_in] + [fwd] * 6 + [bwd] * 6 + [pl.BlockSpec(memory_space=pl.ANY)],
        out_specs=[fwd, bwd, st_out],
        scratch_shapes=[pltpu.VMEM((N_HEADS, V7X_LANES, V7X_LANES), F32)],
    )
    n_in = len(tabs) + 13
    return pl.pallas_call(
        functools.partial(_scan_kernel, n_state_out=n_out),
        grid_spec=grid_spec,
        out_shape=[jax.ShapeDtypeStruct((n, dr), F32),
                   jax.ShapeDtypeStruct((n, dr), F32),
                   jax.ShapeDtypeStruct(new_state.shape, F32)],
        input_output_aliases={n_in: 2},
        name="rwkv_scan",
        compiler_params=_params("arbitrary"),
    )(*tabs, z0, r, kk, lwf, bf, ktf, v, r, kk, lwb, bb, ktb, v, new_state)


def _softmax_rows(parts):
    m = parts[0].max(axis=-1, keepdims=True)
    for p in parts[1:]:
        m = jnp.maximum(m, p.max(axis=-1, keepdims=True))
    es = [jnp.exp(p - m) for p in parts]
    den = es[0].sum(axis=-1, keepdims=True)
    for e in es[1:]:
        den = den + e.sum(axis=-1, keepdims=True)
    inv = 1.0 / den
    return [e * inv for e in es]


def _ctx_attn_kernel(q_ref, k_ref, v_ref, o_ref):
    scale = HEAD_DIM ** -0.5
    npair = q_ref.shape[1] // V7X_LANES
    tile = lambda ref, p: ref[:, p * V7X_LANES:(p + 1) * V7X_LANES].astype(BF16)
    qb = [tile(q_ref, p) for p in range(npair)]
    k = [tile(k_ref, p) for p in range(npair)]
    v = [tile(v_ref, p) for p in range(npair)]
    first_head = lax.broadcasted_iota(jnp.int32, qb[0].shape, 1) < HEAD_DIM
    head_lanes = (first_head, jnp.logical_not(first_head))
    units = [(p, hh) for p in range(npair) for hh in range(2)]

    def per(fn):
        return [fn(u, p, hh) for u, (p, hh) in enumerate(units)]

    q = per(lambda u, p, hh: jnp.where(head_lanes[hh], qb[p], jnp.zeros_like(qb[p])))
    s = per(lambda u, p, hh: _dot_nt(q[u], k[p]) * scale)
    e = per(lambda u, p, hh: jnp.exp(s[u] - s[u].max(axis=-1, keepdims=True)))
    pr = per(lambda u, p, hh: (e[u] * (1.0 / e[u].sum(axis=-1, keepdims=True))).astype(BF16))
    out = per(lambda u, p, hh: _dot(pr[u], v[p]))
    for p in range(npair):
        o_ref[:, p * V7X_LANES:(p + 1) * V7X_LANES] = jnp.where(first_head, out[2 * p], out[2 * p + 1]).astype(o_ref.dtype)


def _ctx_attention(q, k, v, b_len, t_len):
    n = b_len * t_len
    seq = pl.BlockSpec((t_len, D_BRANCH), lambda b: (b, 0))
    return pl.pallas_call(
        _ctx_attn_kernel,
        grid=(b_len,),
        in_specs=[seq, seq, seq],
        out_specs=seq,
        out_shape=jax.ShapeDtypeStruct((n, D_BRANCH), BF16),
        name="ctx_attn",
        compiler_params=_params("parallel"),
    )(q, k, v)


def _nbr_attn_kernel(q_ref, k_ref, v_ref, kc_ref, vc_ref, bias_ref, o_ref, *, rows, kh):
    scale = HEAD_DIM ** -0.5
    win = kh * GRID_W

    group = 8 if rows % 8 == 0 else 1
    units = [(r, hh) for r in range(group) for hh in range(2)]
    first_head = lax.broadcasted_iota(jnp.int32, (GRID_W, V7X_LANES), 1) < HEAD_DIM
    head_lanes = (first_head, jnp.logical_not(first_head))

    def body(gi, carry):
        def per(fn):
            return [fn(u, r, hh) for u, (r, hh) in enumerate(units)]

        i = [gi * group + r for r in range(group)]
        rs = [jnp.clip(x - kh // 2, 0, rows - kh) for x in i]
        q0 = [pl.multiple_of(x * GRID_W, GRID_W) for x in i]
        k0 = [pl.multiple_of(x * GRID_W, GRID_W) for x in rs]
        qb = [q_ref[pl.ds(x, GRID_W), :].astype(BF16) for x in q0]
        kw = [k_ref[pl.ds(x, win), :].astype(BF16) for x in k0]
        vw = [v_ref[pl.ds(x, win), :].astype(BF16) for x in k0]
        kc = kc_ref[0].astype(BF16)
        vc = vc_ref[0].astype(BF16)
        q = per(lambda u, r, hh: jnp.where(head_lanes[hh], qb[r], jnp.zeros_like(qb[r])))
        s_loc = per(lambda u, r, hh: _dot_nt(q[u], kw[r]) * scale + bias_ref[hh, i[r] - rs[r]])
        s_ctx = per(lambda u, r, hh: _dot_nt(q[u], kc) * scale)
        m = per(lambda u, r, hh: jnp.maximum(s_loc[u].max(axis=-1, keepdims=True), s_ctx[u].max(axis=-1, keepdims=True)))
        e_loc = per(lambda u, r, hh: jnp.exp(s_loc[u] - m[u]))
        e_ctx = per(lambda u, r, hh: jnp.exp(s_ctx[u] - m[u]))
        inv = per(lambda u, r, hh: 1.0 / (e_loc[u].sum(axis=-1, keepdims=True) + e_ctx[u].sum(axis=-1, keepdims=True)))
        out = per(lambda u, r, hh: _dot((e_loc[u] * inv[u]).astype(BF16), vw[r])
                  + _dot((e_ctx[u] * inv[u]).astype(BF16), vc))
        for r in range(group):
            o_ref[pl.ds(q0[r], GRID_W), :] = jnp.where(first_head, out[2 * r], out[2 * r + 1]).astype(o_ref.dtype)
        return carry

    lax.fori_loop(0, rows // group, body, 0)


def _nbr_attention(q, k, v, k_ctx, v_ctx, bias, b_len, t_len, row0):
    n = b_len * t_len
    npair = N_HEADS // 2
    rows = t_len // GRID_W
    kh = min(KH_MAX, rows)
    c_len = k_ctx.shape[1]
    assert row0 % t_len == 0
    seq0 = row0 // t_len
    pair = pl.BlockSpec((t_len, V7X_LANES), lambda b, p: (seq0 + b, p))
    return pl.pallas_call(
        functools.partial(_nbr_attn_kernel, rows=rows, kh=kh),
        grid=(b_len, npair),
        in_specs=[pair, pair, pair,
                  pl.BlockSpec((1, c_len, V7X_LANES), lambda b, p: (b, 0, p)),
                  pl.BlockSpec((1, c_len, V7X_LANES), lambda b, p: (b, 0, p)),
                  pl.BlockSpec((2, kh, GRID_W, kh * GRID_W), lambda b, p: (p, 0, 0, 0))],
        out_specs=pl.BlockSpec((t_len, V7X_LANES), lambda b, p: (b, p)),
        out_shape=jax.ShapeDtypeStruct((n, D_BRANCH), BF16),
        name="nbr_attn",
        compiler_params=_params("parallel", "parallel"),
    )(q, k, v, k_ctx, v_ctx, bias)


def _nbr_bias_table(rpb, rows):
    kh = min(KH_MAX, rows)
    w = np.arange(GRID_W)
    col_start = np.clip(w - KW // 2, 0, GRID_W - KW)
    c = np.arange(GRID_W)
    inside = (c[None, :] >= col_start[:, None]) & (c[None, :] < col_start[:, None] + KW)
    rp = jnp.pad(rpb, ((0, 0), (0, 0), (GRID_W - KW, GRID_W - KW)))
    toep = jnp.stack([rp[:, :, GRID_W - 1 - wi:2 * GRID_W - 1 - wi] for wi in range(GRID_W)], axis=2)
    toep = jnp.where(inside[None, None], toep, NEG_BIG)
    tab = jnp.stack([toep[:, KH_MAX - 1 - dd:KH_MAX - 1 - dd + kh] for dd in range(kh)], axis=1)
    tab = jnp.transpose(tab, (0, 1, 3, 2, 4))
    return tab.reshape(rpb.shape[0], kh, GRID_W, kh * GRID_W)


def _top2_route(logits, n_exp):
    lane = lax.broadcasted_iota(jnp.int32, logits.shape, 1)
    lane_f = lane.astype(F32)
    lg = jnp.where(lane < n_exp, logits, -jnp.inf)
    m1 = lg.max(axis=-1, keepdims=True)
    i1 = jnp.where(lg == m1, lane_f, float(V7X_LANES)).min(axis=-1, keepdims=True)
    lg2 = jnp.where(lane_f == i1, -jnp.inf, lg)
    m2 = lg2.max(axis=-1, keepdims=True)
    i2 = jnp.where(lg2 == m2, lane_f, float(V7X_LANES)).min(axis=-1, keepdims=True)
    e = jnp.exp(m2 - m1)
    g1 = 1.0 / (1.0 + e)
    g2 = e * g1
    out = jnp.where(lane == 0, i1, 0.0)
    out = jnp.where(lane == 1, i2, out)
    out = jnp.where(lane == 2, g1, out)
    return jnp.where(lane == 3, g2, out)


def _mix_kernel(yf_ref, yb_ref, bonus_ref, g_ref, yn0_ref, yn1_ref, gate_ref, x_ref, mod_ref,
                gng_ref, gnb_ref, bd_ref, wur_ref, wun_ref, wo_ref, lng_ref, lnb_ref, *rest,
                alpha, n_exp, nb_first):
    if n_exp:
        rhi_ref, rlo_ref, x1_o, h2_o, route_o = rest
    else:
        x1_o, h2_o = rest
    d = x_ref.shape[1]
    yn = jnp.where(pl.program_id(0) < nb_first, yn0_ref[...], yn1_ref[...])
    bd = bd_ref[...]
    y = yf_ref[...] + yb_ref[...]
    inv_n = 1.0 / HEAD_DIM
    mu = _head_sum(y, bd) * inv_n
    yc = y - mu
    var = _head_sum(yc * yc, bd) * inv_n
    yr = yc * lax.rsqrt(var + GN_EPS) * gng_ref[...] + gnb_ref[...]
    yr = (yr + bonus_ref[...]) * g_ref[...]
    m = (gate_ref[:, 0:d] * _dot(yr.astype(BF16), wur_ref[...])
         + gate_ref[:, d:] * _dot(yn, wun_ref[...]))
    o = _dot(m.astype(BF16), wo_ref[...])
    g1 = mod_ref[0, 2:3, :]
    sh2 = mod_ref[0, 3:4, :]
    sc2 = mod_ref[0, 4:5, :]
    x1 = _layer_norm(alpha * x_ref[...] + g1 * o, lng_ref[...], lnb_ref[...])
    x1_o[...] = x1
    h2 = x1 * (1.0 + sc2) + sh2
    h2_o[...] = h2.astype(BF16)
    if n_exp:
        h_hi = h2.astype(BF16)
        h_lo = (h2 - h_hi.astype(F32)).astype(BF16)
        logits = _dot(h_hi, rhi_ref[...]) + (_dot(h_hi, rlo_ref[...]) + _dot(h_lo, rhi_ref[...]))
        route_o[...] = _top2_route(logits, n_exp)


def _mix_out(yf, yb, bonus, g, yn0, yn1, gates, x, modp, lw, lay, alpha, router):
    n, d = x.shape
    dr = D_BRANCH
    row = lambda i: (i, 0)
    nb_first = yn0.shape[0] // ROW_BLOCK
    yt_spec = pl.BlockSpec((ROW_BLOCK, dr), row)
    consts = [lw["gn_g"], lw["gn_b"], lw["bd"], lw["w_up_r"], lw["w_up_n"], lw["w_o"], lw["ln1_g"], lw["ln1_b"]]
    n_exp = 0
    out_specs = [pl.BlockSpec((ROW_BLOCK, d), row), pl.BlockSpec((ROW_BLOCK, d), row)]
    out_shape = [jax.ShapeDtypeStruct((n, d), F32), jax.ShapeDtypeStruct((n, d), BF16)]
    if router is not None:
        n_exp = router.shape[1]
        r_pad = jnp.pad(router, ((0, 0), (0, V7X_LANES - n_exp)))
        r_hi = r_pad.astype(BF16)
        consts += [r_hi, (r_pad - r_hi.astype(F32)).astype(BF16)]
        out_specs.append(pl.BlockSpec((ROW_BLOCK, V7X_LANES), row))
        out_shape.append(jax.ShapeDtypeStruct((n, V7X_LANES), F32))
    const_specs, const_ops = _param_specs(consts)
    return pl.pallas_call(
        functools.partial(_mix_kernel, alpha=alpha, n_exp=n_exp, nb_first=nb_first),
        grid=(n // ROW_BLOCK,),
        in_specs=[yt_spec, yt_spec,
                  pl.BlockSpec((ROW_BLOCK, dr), row), pl.BlockSpec((ROW_BLOCK, dr), row),
                  pl.BlockSpec((ROW_BLOCK, dr), lambda i: (jnp.minimum(i, nb_first - 1), 0)),
                  pl.BlockSpec((ROW_BLOCK, dr), lambda i: (jnp.maximum(i - nb_first, 0), 0)),
                  pl.BlockSpec((ROW_BLOCK, 2 * d), row),
                  pl.BlockSpec((ROW_BLOCK, d), row),
                  lay.mod_spec(d)]
                 + const_specs,
        out_specs=out_specs,
        out_shape=out_shape,
        name="mix_out",
        compiler_params=_params("parallel"),
    )(yf, yb, bonus, g, yn0, yn1, gates, x, modp, *const_ops)


def _swiglu(h, w1, w3, w2):
    u = _dot(h, w1)
    act = u * _sigmoid(u) * _dot(h, w3)
    return _dot(act.astype(BF16), w2)


def _ffn_kernel(h_ref, x1_ref, mod_ref, w1_ref, w3_ref, w2_ref, lng_ref, lnb_ref, o_ref, *, alpha):
    f = _swiglu(h_ref[...], w1_ref[...], w3_ref[...], w2_ref[...])
    g2 = mod_ref[0, 5:6, :]
    o_ref[...] = _layer_norm(alpha * x1_ref[...] + g2 * f, lng_ref[...], lnb_ref[...])


def _ffn_dense(h2, x1, modp, w1, w3, w2, ln_g, ln_b, lay, alpha):
    n, d = x1.shape
    row = lambda i: (i, 0)
    const_specs, const_ops = _param_specs([w1, w3, w2, ln_g, ln_b])
    return pl.pallas_call(
        functools.partial(_ffn_kernel, alpha=alpha),
        grid=(n // ROW_BLOCK,),
        in_specs=[pl.BlockSpec((ROW_BLOCK, d), row), pl.BlockSpec((ROW_BLOCK, d), row), lay.mod_spec(d)]
                 + const_specs,
        out_specs=pl.BlockSpec((ROW_BLOCK, d), row),
        out_shape=jax.ShapeDtypeStruct((n, d), F32),
        name="ffn_dense",
        compiler_params=_params("parallel"),
    )(h2, x1, modp, *const_ops)


def _moe_kernel(be_ref, nb_ref, x_ref, w1_ref, w3_ref, w2_ref, o_ref):
    i = pl.program_id(0)

    @pl.when(i < nb_ref[0])
    def _():
        o_ref[...] = _swiglu(x_ref[...], w1_ref[0], w3_ref[0], w2_ref[0])

    @pl.when(i >= nb_ref[0])
    def _():
        o_ref[...] = jnp.zeros(o_ref.shape, F32)


def _moe_experts(block_e, n_used, xb, w1, w3, w2, lm):
    p, d = xb.shape
    dff = w1.shape[3]
    nblk = p // MOE_ROWS
    one = pl.Buffered(1)
    expert = lambda i, be, nb: (lm, be[i], 0, 0)
    grid_spec = pltpu.PrefetchScalarGridSpec(
        num_scalar_prefetch=2,
        grid=(nblk,),
        in_specs=[pl.BlockSpec((MOE_ROWS, d), lambda i, be, nb: (i, 0)),
                  pl.BlockSpec((None, 1, d, dff), expert, pipeline_mode=one),
                  pl.BlockSpec((None, 1, d, dff), expert, pipeline_mode=one),
                  pl.BlockSpec((None, 1, dff, d), expert, pipeline_mode=one)],
        out_specs=pl.BlockSpec((MOE_ROWS, d), lambda i, be, nb: (i, 0)),
    )
    return pl.pallas_call(
        _moe_kernel,
        grid_spec=grid_spec,
        out_shape=jax.ShapeDtypeStruct((p, d), F32),
        name="moe_experts",
        compiler_params=_params("arbitrary"),
    )(block_e, n_used, xb, w1, w3, w2)


def _combine_kernel(ya_ref, yb_ref, route_ref, x1_ref, mod_ref, lng_ref, lnb_ref, o_ref, *, alpha):
    f = route_ref[:, 2:3] * ya_ref[...] + route_ref[:, 3:4] * yb_ref[...]
    g2 = mod_ref[0, 5:6, :]
    o_ref[...] = _layer_norm(alpha * x1_ref[...] + g2 * f, lng_ref[...], lnb_ref[...])


def _moe_combine(ya, yb, route, x1, modp, ln_g, ln_b, lay, alpha):
    n, d = x1.shape
    row = lambda i: (i, 0)
    return pl.pallas_call(
        functools.partial(_combine_kernel, alpha=alpha),
        grid=(n // ROW_BLOCK,),
        in_specs=[pl.BlockSpec((ROW_BLOCK, d), row), pl.BlockSpec((ROW_BLOCK, d), row),
                  pl.BlockSpec((ROW_BLOCK, V7X_LANES), row), pl.BlockSpec((ROW_BLOCK, d), row),
                  lay.mod_spec(d),
                  _const_spec(ln_g.shape), _const_spec(ln_b.shape)],
        out_specs=pl.BlockSpec((ROW_BLOCK, d), row),
        out_shape=jax.ShapeDtypeStruct((n, d), F32),
        name="moe_combine",
        compiler_params=_params("parallel"),
    )(ya, yb, route, x1, modp, ln_g, ln_b)


def _moe(h2, route, x1, modp, w1, w3, w2, lm, ln_g, ln_b, lay, alpha):
    n, d = x1.shape
    n_exp = w1.shape[1]
    top_e = route[:, 0:2].astype(jnp.int32)
    flat_e = top_e.reshape(-1)
    onehot = (flat_e[:, None] == jnp.arange(n_exp, dtype=jnp.int32)[None, :]).astype(jnp.int32)
    rank = jnp.sum((jnp.cumsum(onehot, axis=0) - onehot) * onehot, axis=1)
    counts = jnp.sum(onehot, axis=0)
    padded = (counts + MOE_ROWS - 1) // MOE_ROWS * MOE_ROWS
    pad_end = jnp.cumsum(padded)
    pad_start = pad_end - padded
    dest = pad_start[flat_e] + rank
    nblk = -(-(2 * n + n_exp * (MOE_ROWS - 1)) // MOE_ROWS)
    p = nblk * MOE_ROWS
    flat_tok = jnp.repeat(jnp.arange(n, dtype=jnp.int32), 2)
    buf_tok = jnp.full((p,), n, jnp.int32).at[dest].set(flat_tok)
    blk_start = jnp.arange(nblk, dtype=jnp.int32) * MOE_ROWS
    block_e = jnp.minimum(jnp.sum((pad_end[None, :] <= blk_start[:, None]).astype(jnp.int32), axis=1), n_exp - 1)
    n_used = (pad_end[-1] // MOE_ROWS).astype(jnp.int32).reshape(1)
    xb = jnp.concatenate([h2, jnp.zeros((1, d), h2.dtype)], axis=0)[buf_tok]
    yb = _moe_experts(block_e, n_used, xb, w1, w3, w2, lm)
    dest2 = dest.reshape(n, 2)
    return _moe_combine(yb[dest2[:, 0]], yb[dest2[:, 1]], route, x1, modp, ln_g, ln_b, lay, alpha)


def _layer(x, modp, lw, l, lay, alpha, z0, ctx_kv, new_skv):
    (bc, tc_len), (bs, ts_len) = lay.passes
    n_ctx = bc * tc_len
    slab, qn, kn, vn, k_new, v_new, gates = _in_proj(x, modp, lw["w_in"], l, lay, lw["n_slab"], *new_skv[1:])
    r, kk, lwf, lwb, bf, bb, ktf, ktb, v, bonus, g = _rwkv_prep(slab, lw, lay)
    yf, yb, s_new = _rwkv_scan(z0, r, kk, lwf, lwb, bf, bb, ktf, ktb, v, lay, new_skv[0])
    yn_ctx = _ctx_attention(qn, kn, vn, bc, tc_len)
    yn_lat = _nbr_attention(qn, kn, vn, ctx_kv[0], ctx_kv[1], lw["bias"], bs, ts_len, n_ctx)
    outs = _mix_out(yf, yb, bonus, g, yn_ctx, yn_lat, gates, x, modp, lw, lay, alpha, lw.get("router"))
    if "router" in lw:
        x1, h2, route = outs
        x2 = _moe(h2, route, x1, modp, lw["moe_w1"], lw["moe_w3"], lw["moe_w2"], l // 2, lw["ln2_g"], lw["ln2_b"],
                  lay, alpha)
    else:
        x1, h2 = outs
        x2 = _ffn_dense(h2, x1, modp, (lw["ffn_w1"], l // 2), (lw["ffn_w3"], l // 2), (lw["ffn_w2"], l // 2),
                        lw["ln2_g"], lw["ln2_b"], lay, alpha)
    return x2, s_new, k_new, v_new


def kernel(x_prompt, x_sample, c, state_rwkv, cache_k, cache_v, c_ctx, w_ada, b_ada, w_in, ts_w, rw_w0, rw_w2, rw_a0, rw_a2, rw_g2, rw_kk, rw_ka, rw_u, rw_gn_g, rw_gn_b, na_rpb, w_up_r, w_up_n, w_o, ln1_g, ln1_b, ln2_g, ln2_b, ffn_w1, ffn_w3, ffn_w2, moe_router, moe_w1, moe_w3, moe_w2):
    depth = w_in.shape[0]
    bc, tc_len, d = x_prompt.shape
    bs, ts_len, _ = x_sample.shape
    dr = D_BRANCH
    alpha = float((2 * depth) ** 0.25)
    n_slab = ts_w.shape[2]
    lora_w = rw_w2.shape[2]
    lora_a = rw_a2.shape[2]

    cvec = jnp.zeros((8, d), F32).at[0].set(c_ctx).at[1:1 + bs].set(c)
    mod = _ada_mod(cvec, w_ada, b_ada)

    half = np.kron(np.eye(2, dtype=np.float32), np.ones((HEAD_DIM, HEAD_DIM), np.float32))
    bd = jnp.asarray(half, BF16)
    rows = ts_len // GRID_W

    passes = ((bc, tc_len), (bs, ts_len))
    n_ctx = bc * tc_len
    x = jnp.concatenate([x_prompt.reshape(n_ctx, d), x_sample.reshape(bs * ts_len, d)], axis=0)
    seq_row = np.concatenate([np.zeros(bc, np.int32), 1 + np.arange(bs, dtype=np.int32)])
    modp = mod[:, seq_row].reshape(depth, bc + bs, 6, d)
    z0_all = jnp.concatenate([jnp.zeros((depth, bc, 2, N_HEADS, HEAD_DIM, HEAD_DIM), F32),
                              jnp.swapaxes(jnp.swapaxes(state_rwkv, 0, 1), -1, -2)], axis=1)
    w_in_b, w_up_r_b, w_up_n_b, w_o_b = (w.astype(BF16) for w in (w_in, w_up_r, w_up_n, w_o))
    ffn_b = [w.astype(BF16) for w in (ffn_w1, ffn_w3, ffn_w2)]
    moe_b = [w.astype(BF16) for w in (moe_w1, moe_w3, moe_w2)]
    new_skv = [jnp.zeros((bc, depth, 2, N_HEADS, HEAD_DIM, HEAD_DIM), F32),
               jnp.zeros((bc, depth, dr, tc_len), F32), jnp.zeros((bc, depth, dr, tc_len), F32)]
    for l in range(depth):
        w2blk = jnp.zeros((2 * lora_w, 2 * dr), F32).at[:lora_w, :dr].set(rw_w2[l, 0]).at[lora_w:, dr:].set(rw_w2[l, 1])
        a2blk = jnp.zeros((2 * lora_a, 2 * dr), F32).at[:lora_a, :dr].set(rw_a2[l, 0]).at[lora_a:, dr:].set(rw_a2[l, 1])
        lw = dict(
            n_slab=n_slab,
            w_in=w_in_b, ts=ts_w[l],
            w0=rw_w0[l].reshape(1, 2 * dr), w2=w2blk.astype(BF16),
            a0=rw_a0[l].reshape(1, 2 * dr), a2=a2blk.astype(BF16),
            g2=rw_g2[l].astype(BF16), kkw=rw_kk[l].reshape(1, dr), ka=rw_ka[l].reshape(1, dr),
            u=rw_u[l].reshape(1, dr), bd=bd,
            gn_g=rw_gn_g[l].reshape(1, dr), gn_b=rw_gn_b[l].reshape(1, dr),
            w_up_r=(w_up_r_b, l), w_up_n=(w_up_n_b, l), w_o=(w_o_b, l),
            ln1_g=ln1_g[l].reshape(1, d), ln1_b=ln1_b[l].reshape(1, d),
            ln2_g=ln2_g[l].reshape(1, d), ln2_b=ln2_b[l].reshape(1, d),
            bias=_nbr_bias_table(na_rpb[l], rows),
        )
        if l % 2 == 0:
            lw.update(ffn_w1=ffn_b[0], ffn_w3=ffn_b[1], ffn_w2=ffn_b[2])
        else:
            lw.update(router=moe_router[l // 2], moe_w1=moe_b[0], moe_w3=moe_b[1], moe_w2=moe_b[2])

        kv = (cache_k[:, l].reshape(bs, -1, dr), cache_v[:, l].reshape(bs, -1, dr))
        x, *new_skv = _layer(x, modp, lw, l, _Layout(passes, l), alpha, z0_all, kv, new_skv)

    cache_out = lambda a: jnp.transpose(a.reshape(bc, depth, N_HEADS, HEAD_DIM, tc_len), (0, 1, 4, 2, 3))
    return (x[:n_ctx].reshape(bc, tc_len, d), x[n_ctx:].reshape(bs, ts_len, d),
            new_skv[0], cache_out(new_skv[1]), cache_out(new_skv[2]))
```

```python
import functools

import numpy as np
import jax
import jax.numpy as jnp
from jax import lax
from jax.experimental import pallas as pl
from jax.experimental.pallas import tpu as pltpu

F32 = jnp.float32
BF16 = jnp.bfloat16
HIGHEST = lax.Precision.HIGHEST

N_HEADS = 8
HEAD_DIM = 64
D_BRANCH = N_HEADS * HEAD_DIM
V7X_LANES = 128
ROW_BLOCK = 256
SCAN_CHUNK = 64
SCAN_CHUNKS_PER_STEP = 2
MOE_ROWS = 256
KH_MAX = 8
KW = 16
GRID_W = 64
LN_EPS = 1e-5
GN_EPS = 64e-5
NEG_BIG = -1e30
VMEM_LIMIT = 56 * 1024 * 1024


def _params(*sem):
    return pltpu.CompilerParams(dimension_semantics=sem, vmem_limit_bytes=VMEM_LIMIT)


def _const_spec(shape):
    nd = len(shape)
    return pl.BlockSpec(shape, lambda *_: (0,) * nd, pipeline_mode=pl.Buffered(1))


def _layer_spec(stack, l):
    nd = stack.ndim - 1
    return pl.BlockSpec((None,) + stack.shape[1:], lambda *_: (l,) + (0,) * nd, pipeline_mode=pl.Buffered(1))


def _param_specs(params):
    specs = [_layer_spec(*p) if isinstance(p, tuple) else _const_spec(p.shape) for p in params]
    return specs, [p[0] if isinstance(p, tuple) else p for p in params]


class _Layout:
    def __init__(self, passes, layer=0):
        self.passes = passes
        self.layer = layer
        self.n_seq = sum(b for b, _ in passes)

    def _per_block(self, fn):
        def at(i):
            out, blk0, seq0 = None, 0, 0
            for b, t in self.passes:
                per = t // ROW_BLOCK
                val = fn(i - blk0, per, seq0)
                out = val if out is None else jnp.where(i >= blk0, val, out)
                blk0 += b * per
                seq0 += b
            return out
        return at

    def seq_of_block(self, i):
        return self._per_block(lambda j, per, s0: s0 + j // per)(i)

    def is_first(self, i):
        return self._per_block(lambda j, per, s0: (j % per) == 0)(i)

    def is_last(self, i):
        return self._per_block(lambda j, per, s0: (j % per) == per - 1)(i)

    def mod_spec(self, d):
        return pl.BlockSpec((None, 1, 6, d), lambda i: (self.layer, self.seq_of_block(i), 0, 0))


def _dot(a, b):
    return jnp.dot(a, b, preferred_element_type=F32)


def _dot_nt(a, b):
    return lax.dot_general(a, b, (((1,), (1,)), ((), ())), preferred_element_type=F32)


def _dot_tn(a, b):
    return lax.dot_general(a, b, (((0,), (0,)), ((), ())), preferred_element_type=F32)


def _mm1(a, b, f=_dot):
    return f(a.astype(BF16), b.astype(BF16))


def _mm_mask_lhs(m, x):
    hi = x.astype(BF16)
    lo = (x - hi.astype(F32)).astype(BF16)
    return _dot(m, hi) + _dot(m, lo)


def _sigmoid(x):
    return 1.0 / (1.0 + jnp.exp(-x))


def _head_sum(x, bd):
    outs = []
    for j in range(x.shape[1] // V7X_LANES):
        xs = x[:, j * V7X_LANES:(j + 1) * V7X_LANES]
        hi = xs.astype(BF16)
        lo = (xs - hi.astype(F32)).astype(BF16)
        outs.append(_dot(hi, bd) + _dot(lo, bd))
    return jnp.concatenate(outs, axis=1)


def _layer_norm(z, g, b):
    mu = jnp.mean(z, axis=-1, keepdims=True)
    zc = z - mu
    var = jnp.mean(zc * zc, axis=-1, keepdims=True)
    return zc * lax.rsqrt(var + LN_EPS) * g + b


def _ada_kernel(c_ref, w_ref, b_ref, o_ref):
    cs = c_ref[...]
    s = cs * _sigmoid(cs)
    o_ref[0] = jnp.dot(s, w_ref[0], preferred_element_type=F32, precision=HIGHEST) + b_ref[0]


def _ada_mod(cvec, w_ada, b_ada):
    depth, d, n6 = w_ada.shape
    tn = n6 // 4
    return pl.pallas_call(
        _ada_kernel,
        grid=(depth, n6 // tn),
        in_specs=[pl.BlockSpec((8, d), lambda l, j: (0, 0)),
                  pl.BlockSpec((1, d, tn), lambda l, j: (l, 0, j)),
                  pl.BlockSpec((1, 1, tn), lambda l, j: (l, 0, j))],
        out_specs=pl.BlockSpec((1, 8, tn), lambda l, j: (l, 0, j)),
        out_shape=jax.ShapeDtypeStruct((depth, 8, n6), F32),
        name="ada_mod",
        compiler_params=_params("parallel", "parallel"),
    )(cvec, w_ada, b_ada.reshape(depth, 1, n6))


def _inproj_kernel(x_ref, mod_ref, w_ref, kbuf_ref, vbuf_ref, slab_ref, q_ref, k_ref, v_ref, kc_ref, vc_ref, gate_ref,
                   *, n_slab, nb_cache):
    del kbuf_ref, vbuf_ref
    sh = mod_ref[0, 0:1, :]
    sc = mod_ref[0, 1:2, :]
    h = (x_ref[...] * (1.0 + sc) + sh).astype(BF16)
    dn = D_BRANCH
    slab_ref[...] = _dot(h, w_ref[:, 0:n_slab])
    q_ref[...] = _dot(h, w_ref[:, n_slab:n_slab + dn]).astype(q_ref.dtype)
    k = _dot(h, w_ref[:, n_slab + dn:n_slab + 2 * dn])
    v = _dot(h, w_ref[:, n_slab + 2 * dn:n_slab + 3 * dn])
    k_ref[...] = k.astype(k_ref.dtype)
    v_ref[...] = v.astype(v_ref.dtype)

    @pl.when(pl.program_id(0) < nb_cache)
    def _():
        kc_ref[0] = k.T
        vc_ref[0] = v.T

    gate_ref[...] = _sigmoid(_dot(h, w_ref[:, n_slab + 3 * dn:]))


def _in_proj(x, modp, w_in, l, lay, n_slab, cache_k, cache_v):
    n, d = x.shape
    n_cols = w_in.shape[2]
    dn = D_BRANCH
    n_gate = n_cols - n_slab - 3 * dn
    b_cache, t_cache = lay.passes[0]
    per_seq = t_cache // ROW_BLOCK
    nb_cache = b_cache * per_seq
    row = lambda i: (i, 0)

    def cache_block(i):
        ic = jnp.minimum(i, nb_cache - 1)
        return ic // per_seq, l, 0, ic % per_seq

    cache_spec = pl.BlockSpec((1, None, dn, ROW_BLOCK), cache_block)
    any_spec = pl.BlockSpec(memory_space=pl.ANY)
    return pl.pallas_call(
        functools.partial(_inproj_kernel, n_slab=n_slab, nb_cache=nb_cache),
        grid=(n // ROW_BLOCK,),
        in_specs=[pl.BlockSpec((ROW_BLOCK, d), row),
                  lay.mod_spec(d),
                  _layer_spec(w_in, l),
                  any_spec, any_spec],
        out_specs=[pl.BlockSpec((ROW_BLOCK, n_slab), row),
                   pl.BlockSpec((ROW_BLOCK, dn), row), pl.BlockSpec((ROW_BLOCK, dn), row),
                   pl.BlockSpec((ROW_BLOCK, dn), row),
                   cache_spec, cache_spec,
                   pl.BlockSpec((ROW_BLOCK, n_gate), row)],
        out_shape=[jax.ShapeDtypeStruct((n, n_slab), F32),
                   jax.ShapeDtypeStruct((n, dn), BF16), jax.ShapeDtypeStruct((n, dn), BF16),
                   jax.ShapeDtypeStruct((n, dn), BF16),
                   jax.ShapeDtypeStruct(cache_k.shape, F32), jax.ShapeDtypeStruct(cache_v.shape, F32),
                   jax.ShapeDtypeStruct((n, n_gate), F32)],
        input_output_aliases={3: 4, 4: 5},
        name="in_proj",
        compiler_params=_params("arbitrary"),
    )(x, modp, w_in, cache_k, cache_v)


def _prep_kernel(cur_ref, prev_ref, next_ref, ts_ref, w0_ref, w2_ref, a0_ref, a2_ref, g2_ref,
                 kkw_ref, ka_ref, u_ref, bd_ref,
                 r_o, kk_o, lwf_o, lwb_o, bf_o, bb_o, ktf_o, ktb_o, v_o, bonus_o, g_o,
                 *, lay, lora_w, lora_a):
    i = pl.program_id(0)
    tb, n_slab = cur_ref.shape
    dr = D_BRANCH
    cur = cur_ref[...]
    first = lay.is_first(i)
    last = lay.is_last(i)
    prow = jnp.where(first, 0.0, prev_ref[7:8, :])
    nrow = jnp.where(last, 0.0, next_ref[0:1, :])
    rid = lax.broadcasted_iota(jnp.int32, (tb, n_slab), 0)
    xp = jnp.where(rid == 0, prow, pltpu.roll(cur, 1, 0))
    xn = jnp.where(rid == tb - 1, nrow, pltpu.roll(cur, tb - 1, 0))
    s = ts_ref[0:1, :] * xp + ts_ref[1:2, :] * cur + ts_ref[2:3, :] * xn

    r = s[:, 0:dr]
    k = s[:, dr:2 * dr]
    v = s[:, 2 * dr:3 * dr]
    o = 3 * dr
    wd = s[:, o:o + 2 * lora_w]
    ad = s[:, o + 2 * lora_w:o + 2 * lora_w + 2 * lora_a]
    gd = s[:, o + 2 * lora_w + 2 * lora_a:]

    wl = w0_ref[...] + _dot(jnp.tanh(wd).astype(BF16), w2_ref[...])
    logw = -float(np.exp(-0.5)) * _sigmoid(wl)
    a = _sigmoid(a0_ref[...] + _dot(ad.astype(BF16), a2_ref[...]))
    g = _dot(_sigmoid(gd).astype(BF16), g2_ref[...])

    bd = bd_ref[...]
    kk = k * kkw_ref[...]
    kk = kk * lax.rsqrt(_head_sum(kk * kk, bd) + 1e-12)
    ka = ka_ref[...]
    a_f = a[:, :dr]
    a_b = a[:, dr:]
    kt_f = k * (1.0 + (a_f - 1.0) * ka)
    kt_b = k * (1.0 + (a_b - 1.0) * ka)
    bonus = _head_sum(r * (0.5 * (kt_f + kt_b)) * u_ref[...], bd) * v

    r_o[...] = r
    kk_o[...] = kk
    lwf_o[...] = logw[:, :dr]
    lwb_o[...] = logw[:, dr:]
    bf_o[...] = kk * a_f
    bb_o[...] = kk * a_b
    ktf_o[...] = kt_f
    ktb_o[...] = kt_b
    v_o[...] = v
    bonus_o[...] = bonus
    g_o[...] = g


def _rwkv_prep(slab, lw, lay):
    n, n_slab = slab.shape
    dr = D_BRANCH
    sub = ROW_BLOCK // 8
    lora_w = lw["w2"].shape[0] // 2
    lora_a = lw["a2"].shape[0] // 2
    row = lambda i: (i, 0)
    consts = [lw["ts"], lw["w0"], lw["w2"], lw["a0"], lw["a2"], lw["g2"], lw["kkw"], lw["ka"], lw["u"], lw["bd"]]
    outs = pl.pallas_call(
        functools.partial(_prep_kernel, lay=lay, lora_w=lora_w, lora_a=lora_a),
        grid=(n // ROW_BLOCK,),
        in_specs=[pl.BlockSpec((ROW_BLOCK, n_slab), row),
                  pl.BlockSpec((8, n_slab), lambda i: (jnp.maximum(i * sub - 1, 0), 0)),
                  pl.BlockSpec((8, n_slab), lambda i: (jnp.minimum((i + 1) * sub, n // 8 - 1), 0))]
                 + [_const_spec(c.shape) for c in consts],
        out_specs=[pl.BlockSpec((ROW_BLOCK, dr), row)] * 11,
        out_shape=[jax.ShapeDtypeStruct((n, dr), F32)] * 11,
        name="rwkv_prep",
        compiler_params=_params("parallel"),
    )(slab, slab, slab, *consts)
    return outs


def _tri_masks(rev):
    c = SCAN_CHUNK
    ti = lax.broadcasted_iota(jnp.int32, (c, 2 * c), 0)
    ii = lax.broadcasted_iota(jnp.int32, (c, 2 * c), 1) % c
    if rev:
        ti, ii = c - 1 - ti, c - 1 - ii
    one = lambda m: jnp.where(m, 1.0, 0.0)
    levels = []
    s = 1
    while s < c:
        same = (ti // (2 * s)) == (ii // (2 * s))
        levels.append(one(same & ((ti % (2 * s)) >= s) & ((ii % (2 * s)) < s)))
        s *= 2
    return one(ii <= ti), one(ii < ti), levels


def _scan_kernel(fwd_tab, bwd_tab, seq_tab, first_tab, last_tab,
                 z0_ref, rf, kkf, lwf, bf, ktf, vf, rb, kkb, lwb, bb, ktb, vb, sbuf_ref, yf_o, yb_o, sfin_o, z_scr,
                 *, n_state_out):
    del sbuf_ref
    g_step = pl.program_id(0)
    c = SCAN_CHUNK
    lanes = V7X_LANES
    npair = N_HEADS // 2
    assert c == HEAD_DIM and 2 * HEAD_DIM == lanes

    row_blk = lax.broadcasted_iota(jnp.int32, (lanes, lanes), 0) // HEAD_DIM
    col_blk = lax.broadcasted_iota(jnp.int32, (lanes, lanes), 1) // HEAD_DIM
    bd_mask = jnp.where(row_blk == col_blk, 1.0, 0.0)
    eye_full = jnp.where(lax.broadcasted_iota(jnp.int32, (lanes, lanes), 0)
                         == lax.broadcasted_iota(jnp.int32, (lanes, lanes), 1), 1.0, 0.0)
    lane_c = lax.broadcasted_iota(jnp.int32, (c, lanes), 1)
    lo = jnp.where(lane_c < HEAD_DIM, 1.0, 0.0)
    hi = 1.0 - lo
    eye_pair = jnp.where(lax.broadcasted_iota(jnp.int32, (c, lanes), 0) == lane_c % HEAD_DIM, 1.0, 0.0)

    def bd(x):
        return jnp.concatenate([x * lo, x * hi], axis=0)

    @pl.when(first_tab[g_step] == 1)
    def _():
        for d in range(2):
            for p in range(npair):
                za = jnp.concatenate([z0_ref[0, d, 2 * p], z0_ref[0, d, 2 * p + 1]], axis=1)
                z_scr[d * npair + p] = bd(za)

    refs = ((rf, kkf, lwf, bf, ktf, vf, yf_o), (rb, kkb, lwb, bb, ktb, vb, yb_o))
    masks = (_tri_masks(False), _tri_masks(True))
    cum_lhs = [m[0][:, 0:c].astype(BF16) for m in masks]
    last_row = (c - 1, 0)
    nck = rf.shape[0] // c
    units = [(d, p, ck) for d in range(2) for p in range(npair) for ck in range(nck)]

    def tile(d, p, ck):
        row0 = (nck - 1 - ck if d else ck) * c
        return slice(row0, row0 + c), slice(p * lanes, (p + 1) * lanes)

    def per(fn):
        return [fn(i, d, tile(d, p, ck)) for i, (d, p, ck) in enumerate(units)]

    lw = per(lambda i, d, sl: refs[d][2][sl])
    cum = per(lambda i, d, sl: _mm_mask_lhs(cum_lhs[d], lw[i]))
    tot = per(lambda i, d, sl: jnp.broadcast_to(cum[i][last_row[d]:last_row[d] + 1], (c, lanes)))
    r_s = per(lambda i, d, sl: refs[d][0][sl] * jnp.exp(cum[i]))
    k_s = per(lambda i, d, sl: refs[d][1][sl] * jnp.exp(cum[i] - lw[i]))
    e_neg = per(lambda i, d, sl: jnp.exp(-cum[i]))
    e_end = per(lambda i, d, sl: jnp.exp(tot[i] - cum[i]))
    b = per(lambda i, d, sl: refs[d][3][sl])
    kt = per(lambda i, d, sl: refs[d][4][sl])
    v = per(lambda i, d, sl: refs[d][5][sl])
    v_bd = per(lambda i, d, sl: bd(v[i]))
    kr = per(lambda i, d, sl: jnp.concatenate([k_s[i], r_s[i]], axis=0))
    ab = per(lambda i, d, sl: _mm1(kr[i], bd(b[i] * e_neg[i]), _dot_nt))
    ak = per(lambda i, d, sl: _mm1(kr[i], bd(kt[i] * e_neg[i]), _dot_nt))
    a_b = per(lambda i, d, sl: ab[i][0:c] * masks[d][1])
    b_b = per(lambda i, d, sl: ab[i][c:] * masks[d][0])
    a_kt = per(lambda i, d, sl: ak[i][0:c] * masks[d][1])
    b_kt = per(lambda i, d, sl: ak[i][c:] * masks[d][0])
    dm = per(lambda i, d, sl: eye_pair - a_b[i] * masks[d][2][0])
    for lvl in range(1, len(masks[0][2])):
        zd = per(lambda i, d, sl: _mm1(a_b[i] * masks[d][2][lvl], bd(dm[i])))
        dm = per(lambda i, d, sl: dm[i] - _mm1(dm[i], bd(zd[i])))
    akv = per(lambda i, d, sl: _mm1(jnp.concatenate([a_kt[i], b_kt[i]], axis=0), v_bd[i]))
    w1 = [x[0:c] for x in akv]
    g = per(lambda i, d, sl: _mm1(dm[i], jnp.concatenate([bd(k_s[i]), bd(w1[i])], axis=1)))
    bg = per(lambda i, d, sl: _mm1(b_b[i], jnp.concatenate([bd(g[i][:, 0:lanes]), bd(g[i][:, lanes:])], axis=1)))
    q = per(lambda i, d, sl: r_s[i] - bg[i][:, 0:lanes])
    y0 = per(lambda i, d, sl: akv[i][c:] - bg[i][:, lanes:])
    cg = per(lambda i, d, sl: _mm1(b[i] * e_end[i], g[i], _dot_tn))
    phi = per(lambda i, d, sl: eye_full * jnp.exp(jnp.concatenate([tot[i], tot[i]], axis=0))
              - cg[i][:, 0:lanes] * bd_mask)
    psi = per(lambda i, d, sl: (_mm1(kt[i] * e_end[i], v[i], _dot_tn) - cg[i][:, lanes:]) * bd_mask)
    z = [z_scr[s] for s in range(2 * npair)]
    for ck in range(nck):
        sel = [(i, d, p) for i, (d, p, k2) in enumerate(units) if k2 == ck]
        qz = [_mm1(jnp.concatenate([q[i], phi[i]], axis=0), z[d * npair + p]) for i, d, p in sel]
        y = [x[0:c] + y0[i] for x, (i, d, p) in zip(qz, sel)]
        z = [x[c:] + psi[i] for x, (i, d, p) in zip(qz, sel)]
        for yy, (i, d, p) in zip(y, sel):
            refs[d][6][tile(d, p, ck)] = yy
    for s in range(2 * npair):
        z_scr[s] = z[s]

    @pl.when(jnp.logical_and(last_tab[g_step] == 1, seq_tab[g_step] < n_state_out))
    def _():
        for d in range(2):
            for p in range(npair):
                zt = z_scr[d * npair + p].T
                sfin_o[0, d, 2 * p] = zt[0:HEAD_DIM, 0:HEAD_DIM]
                sfin_o[0, d, 2 * p + 1] = zt[HEAD_DIM:, HEAD_DIM:]


def _scan_tables(lay):
    blk = SCAN_CHUNK * SCAN_CHUNKS_PER_STEP
    rows, blk0, seq0 = [], 0, 0
    for b, t in lay.passes:
        nj = t // blk
        for s in range(b):
            for j in range(nj):
                rows.append((blk0 + s * nj + j, blk0 + s * nj + nj - 1 - j, seq0 + s, int(j == 0), int(j == nj - 1)))
        blk0 += b * nj
        seq0 += b
    return [jnp.asarray(col, jnp.int32) for col in zip(*rows)]


def _rwkv_scan(z0, r, kk, lwf, lwb, bf, bb, ktf, ktb, v, lay, new_state):
    blk = SCAN_CHUNK * SCAN_CHUNKS_PER_STEP
    n, dr = r.shape
    n_out = new_state.shape[0]
    assert n_out == lay.passes[0][0]
    tabs = _scan_tables(lay)
    fwd = pl.BlockSpec((blk, dr), lambda g, fw, bw, sq, fi, la: (fw[g], 0))
    bwd = pl.BlockSpec((blk, dr), lambda g, fw, bw, sq, fi, la: (bw[g], 0))
    st_out = pl.BlockSpec((1, None, 2, N_HEADS, HEAD_DIM, HEAD_DIM),
                          lambda g, fw, bw, sq, fi, la: (jnp.minimum(sq[g], n_out - 1), lay.layer, 0, 0, 0, 0))
    st_in = pl.BlockSpec((None, 1, 2, N_HEADS, HEAD_DIM, HEAD_DIM),
                         lambda g, fw, bw, sq, fi, la: (lay.layer, sq[g], 0, 0, 0, 0))
    grid_spec = pltpu.PrefetchScalarGridSpec(
        num_scalar_prefetch=len(tabs),
        grid=(int(tabs[0].shape[0]),),
        in_specs=[st_in] + [fwd] * 6 + [bwd] * 6 + [pl.BlockSpec(memory_space=pl.ANY)],
        out_specs=[fwd, bwd, st_out],
        scratch_shapes=[pltpu.VMEM((N_HEADS, V7X_LANES, V7X_LANES), F32)],
    )
    n_in = len(tabs) + 13
    return pl.pallas_call(
        functools.partial(_scan_kernel, n_state_out=n_out),
        grid_spec=grid_spec,
        out_shape=[jax.ShapeDtypeStruct((n, dr), F32),
                   jax.ShapeDtypeStruct((n, dr), F32),
                   jax.ShapeDtypeStruct(new_state.shape, F32)],
        input_output_aliases={n_in: 2},
        name="rwkv_scan",
        compiler_params=_params("arbitrary"),
    )(*tabs, z0, r, kk, lwf, bf, ktf, v, r, kk, lwb, bb, ktb, v, new_state)


def _softmax_rows(parts):
    m = parts[0].max(axis=-1, keepdims=True)
    for p in parts[1:]:
        m = jnp.maximum(m, p.max(axis=-1, keepdims=True))
    es = [jnp.exp(p - m) for p in parts]
    den = es[0].sum(axis=-1, keepdims=True)
    for e in es[1:]:
        den = den + e.sum(axis=-1, keepdims=True)
    inv = 1.0 / den
    return [e * inv for e in es]


def _ctx_attn_kernel(q_ref, k_ref, v_ref, o_ref):
    scale = HEAD_DIM ** -0.5
    npair = q_ref.shape[1] // V7X_LANES
    tile = lambda ref, p: ref[:, p * V7X_LANES:(p + 1) * V7X_LANES].astype(BF16)
    qb = [tile(q_ref, p) for p in range(npair)]
    k = [tile(k_ref, p) for p in range(npair)]
    v = [tile(v_ref, p) for p in range(npair)]
    first_head = lax.broadcasted_iota(jnp.int32, qb[0].shape, 1) < HEAD_DIM
    head_lanes = (first_head, jnp.logical_not(first_head))
    units = [(p, hh) for p in range(npair) for hh in range(2)]

    def per(fn):
        return [fn(u, p, hh) for u, (p, hh) in enumerate(units)]

    q = per(lambda u, p, hh: jnp.where(head_lanes[hh], qb[p], jnp.zeros_like(qb[p])))
    s = per(lambda u, p, hh: _dot_nt(q[u], k[p]) * scale)
    e = per(lambda u, p, hh: jnp.exp(s[u] - s[u].max(axis=-1, keepdims=True)))
    pr = per(lambda u, p, hh: (e[u] * (1.0 / e[u].sum(axis=-1, keepdims=True))).astype(BF16))
    out = per(lambda u, p, hh: _dot(pr[u], v[p]))
    for p in range(npair):
        o_ref[:, p * V7X_LANES:(p + 1) * V7X_LANES] = jnp.where(first_head, out[2 * p], out[2 * p + 1]).astype(o_ref.dtype)


def _ctx_attention(q, k, v, b_len, t_len):
    n = b_len * t_len
    seq = pl.BlockSpec((t_len, D_BRANCH), lambda b: (b, 0))
    return pl.pallas_call(
        _ctx_attn_kernel,
        grid=(b_len,),
        in_specs=[seq, seq, seq],
        out_specs=seq,
        out_shape=jax.ShapeDtypeStruct((n, D_BRANCH), BF16),
        name="ctx_attn",
        compiler_params=_params("parallel"),
    )(q, k, v)


def _nbr_attn_kernel(q_ref, k_ref, v_ref, kc_ref, vc_ref, bias_ref, o_ref, *, rows, kh):
    scale = HEAD_DIM ** -0.5
    win = kh * GRID_W

    group = 8 if rows % 8 == 0 else 1
    units = [(r, hh) for r in range(group) for hh in range(2)]
    first_head = lax.broadcasted_iota(jnp.int32, (GRID_W, V7X_LANES), 1) < HEAD_DIM
    head_lanes = (first_head, jnp.logical_not(first_head))

    def body(gi, carry):
        def per(fn):
            return [fn(u, r, hh) for u, (r, hh) in enumerate(units)]

        i = [gi * group + r for r in range(group)]
        rs = [jnp.clip(x - kh // 2, 0, rows - kh) for x in i]
        q0 = [pl.multiple_of(x * GRID_W, GRID_W) for x in i]
        k0 = [pl.multiple_of(x * GRID_W, GRID_W) for x in rs]
        qb = [q_ref[pl.ds(x, GRID_W), :].astype(BF16) for x in q0]
        kw = [k_ref[pl.ds(x, win), :].astype(BF16) for x in k0]
        vw = [v_ref[pl.ds(x, win), :].astype(BF16) for x in k0]
        kc = kc_ref[0].astype(BF16)
        vc = vc_ref[0].astype(BF16)
        q = per(lambda u, r, hh: jnp.where(head_lanes[hh], qb[r], jnp.zeros_like(qb[r])))
        s_loc = per(lambda u, r, hh: _dot_nt(q[u], kw[r]) * scale + bias_ref[hh, i[r] - rs[r]])
        s_ctx = per(lambda u, r, hh: _dot_nt(q[u], kc) * scale)
        m = per(lambda u, r, hh: jnp.maximum(s_loc[u].max(axis=-1, keepdims=True), s_ctx[u].max(axis=-1, keepdims=True)))
        e_loc = per(lambda u, r, hh: jnp.exp(s_loc[u] - m[u]))
        e_ctx = per(lambda u, r, hh: jnp.exp(s_ctx[u] - m[u]))
        inv = per(lambda u, r, hh: 1.0 / (e_loc[u].sum(axis=-1, keepdims=True) + e_ctx[u].sum(axis=-1, keepdims=True)))
        out = per(lambda u, r, hh: _dot((e_loc[u] * inv[u]).astype(BF16), vw[r])
                  + _dot((e_ctx[u] * inv[u]).astype(BF16), vc))
        for r in range(group):
            o_ref[pl.ds(q0[r], GRID_W), :] = jnp.where(first_head, out[2 * r], out[2 * r + 1]).astype(o_ref.dtype)
        return carry

    lax.fori_loop(0, rows // group, body, 0)


def _nbr_attention(q, k, v, k_ctx, v_ctx, bias, l, b_len, t_len, row0):
    n = b_len * t_len
    npair = N_HEADS // 2
    rows = t_len // GRID_W
    kh = min(KH_MAX, rows)
    c_len = k_ctx.shape[1]
    assert row0 % t_len == 0
    seq0 = row0 // t_len
    pair = pl.BlockSpec((t_len, V7X_LANES), lambda b, p: (seq0 + b, p))
    return pl.pallas_call(
        functools.partial(_nbr_attn_kernel, rows=rows, kh=kh),
        grid=(b_len, npair),
        in_specs=[pair, pair, pair,
                  pl.BlockSpec((1, c_len, V7X_LANES), lambda b, p: (b, 0, p)),
                  pl.BlockSpec((1, c_len, V7X_LANES), lambda b, p: (b, 0, p)),
                  pl.BlockSpec((None, 2, kh, GRID_W, kh * GRID_W), lambda b, p: (l, p, 0, 0, 0))],
        out_specs=pl.BlockSpec((t_len, V7X_LANES), lambda b, p: (b, p)),
        out_shape=jax.ShapeDtypeStruct((n, D_BRANCH), BF16),
        name="nbr_attn",
        compiler_params=_params("parallel", "parallel"),
    )(q, k, v, k_ctx, v_ctx, bias)


def _nbr_bias_tables(rpb, rows):
    kh = min(KH_MAX, rows)
    w = np.arange(GRID_W)
    c = np.arange(GRID_W)
    col_start = np.clip(w - KW // 2, 0, GRID_W - KW)
    inside = (c[None, :] >= col_start[:, None]) & (c[None, :] < col_start[:, None] + KW)
    col_off = c[None, :] - w[:, None] + (KW - 1)
    sel_col = ((np.arange(2 * KW - 1)[:, None, None] == col_off[None]) & inside[None]).astype(np.float32)
    row_off = (KH_MAX - 1) - np.arange(kh)[:, None] + np.arange(kh)[None, :]
    sel_row = (np.arange(2 * KH_MAX - 1)[None, None, :] == row_off[:, :, None]).astype(np.float32)
    toep = jnp.einsum("lhrj,jwc->lhrwc", rpb, sel_col, precision=HIGHEST)
    tab = jnp.einsum("dar,lhrwc->lhdwac", sel_row, toep, precision=HIGHEST)
    tab = jnp.where(inside[None, None, None, :, None, :], tab, NEG_BIG)
    return tab.reshape(rpb.shape[0], rpb.shape[1], kh, GRID_W, kh * GRID_W)


def _top2_route(logits, n_exp):
    lane = lax.broadcasted_iota(jnp.int32, logits.shape, 1)
    lane_f = lane.astype(F32)
    lg = jnp.where(lane < n_exp, logits, -jnp.inf)
    m1 = lg.max(axis=-1, keepdims=True)
    i1 = jnp.where(lg == m1, lane_f, float(V7X_LANES)).min(axis=-1, keepdims=True)
    lg2 = jnp.where(lane_f == i1, -jnp.inf, lg)
    m2 = lg2.max(axis=-1, keepdims=True)
    i2 = jnp.where(lg2 == m2, lane_f, float(V7X_LANES)).min(axis=-1, keepdims=True)
    e = jnp.exp(m2 - m1)
    g1 = 1.0 / (1.0 + e)
    g2 = e * g1
    out = jnp.where(lane == 0, i1, 0.0)
    out = jnp.where(lane == 1, i2, out)
    out = jnp.where(lane == 2, g1, out)
    return jnp.where(lane == 3, g2, out)


def _mix_kernel(yf_ref, yb_ref, bonus_ref, g_ref, yn0_ref, yn1_ref, gate_ref, x_ref, mod_ref,
                gng_ref, gnb_ref, bd_ref, wur_ref, wun_ref, wo_ref, lng_ref, lnb_ref, *rest,
                alpha, n_exp, nb_first):
    if n_exp:
        rhi_ref, rlo_ref, x1_o, h2_o, route_o = rest
    else:
        x1_o, h2_o = rest
    d = x_ref.shape[1]
    yn = jnp.where(pl.program_id(0) < nb_first, yn0_ref[...], yn1_ref[...])
    bd = bd_ref[...]
    y = yf_ref[...] + yb_ref[...]
    inv_n = 1.0 / HEAD_DIM
    mu = _head_sum(y, bd) * inv_n
    yc = y - mu
    var = _head_sum(yc * yc, bd) * inv_n
    yr = yc * lax.rsqrt(var + GN_EPS) * gng_ref[...] + gnb_ref[...]
    yr = (yr + bonus_ref[...]) * g_ref[...]
    m = (gate_ref[:, 0:d] * _dot(yr.astype(BF16), wur_ref[...])
         + gate_ref[:, d:] * _dot(yn, wun_ref[...]))
    o = _dot(m.astype(BF16), wo_ref[...])
    g1 = mod_ref[0, 2:3, :]
    sh2 = mod_ref[0, 3:4, :]
    sc2 = mod_ref[0, 4:5, :]
    x1 = _layer_norm(alpha * x_ref[...] + g1 * o, lng_ref[...], lnb_ref[...])
    x1_o[...] = x1
    h2 = x1 * (1.0 + sc2) + sh2
    h2_o[...] = h2.astype(BF16)
    if n_exp:
        h_hi = h2.astype(BF16)
        h_lo = (h2 - h_hi.astype(F32)).astype(BF16)
        logits = _dot(h_hi, rhi_ref[...]) + (_dot(h_hi, rlo_ref[...]) + _dot(h_lo, rhi_ref[...]))
        route_o[...] = _top2_route(logits, n_exp)


def _mix_out(yf, yb, bonus, g, yn0, yn1, gates, x, modp, lw, lay, alpha, router):
    n, d = x.shape
    dr = D_BRANCH
    row = lambda i: (i, 0)
    nb_first = yn0.shape[0] // ROW_BLOCK
    yt_spec = pl.BlockSpec((ROW_BLOCK, dr), row)
    consts = [lw["gn_g"], lw["gn_b"], lw["bd"], lw["w_up_r"], lw["w_up_n"], lw["w_o"], lw["ln1_g"], lw["ln1_b"]]
    n_exp = 0
    out_specs = [pl.BlockSpec((ROW_BLOCK, d), row), pl.BlockSpec((ROW_BLOCK, d), row)]
    out_shape = [jax.ShapeDtypeStruct((n, d), F32), jax.ShapeDtypeStruct((n, d), BF16)]
    if router is not None:
        n_exp = router.shape[1]
        r_pad = jnp.pad(router, ((0, 0), (0, V7X_LANES - n_exp)))
        r_hi = r_pad.astype(BF16)
        consts += [r_hi, (r_pad - r_hi.astype(F32)).astype(BF16)]
        out_specs.append(pl.BlockSpec((ROW_BLOCK, V7X_LANES), row))
        out_shape.append(jax.ShapeDtypeStruct((n, V7X_LANES), F32))
    const_specs, const_ops = _param_specs(consts)
    return pl.pallas_call(
        functools.partial(_mix_kernel, alpha=alpha, n_exp=n_exp, nb_first=nb_first),
        grid=(n // ROW_BLOCK,),
        in_specs=[yt_spec, yt_spec,
                  pl.BlockSpec((ROW_BLOCK, dr), row), pl.BlockSpec((ROW_BLOCK, dr), row),
                  pl.BlockSpec((ROW_BLOCK, dr), lambda i: (jnp.minimum(i, nb_first - 1), 0)),
                  pl.BlockSpec((ROW_BLOCK, dr), lambda i: (jnp.maximum(i - nb_first, 0), 0)),
                  pl.BlockSpec((ROW_BLOCK, 2 * d), row),
                  pl.BlockSpec((ROW_BLOCK, d), row),
                  lay.mod_spec(d)]
                 + const_specs,
        out_specs=out_specs,
        out_shape=out_shape,
        name="mix_out",
        compiler_params=_params("parallel"),
    )(yf, yb, bonus, g, yn0, yn1, gates, x, modp, *const_ops)


def _swiglu(h, w1, w3, w2):
    u = _dot(h, w1)
    act = u * _sigmoid(u) * _dot(h, w3)
    return _dot(act.astype(BF16), w2)


def _ffn_kernel(h_ref, x1_ref, mod_ref, w1_ref, w3_ref, w2_ref, lng_ref, lnb_ref, o_ref, *, alpha):
    f = _swiglu(h_ref[...], w1_ref[...], w3_ref[...], w2_ref[...])
    g2 = mod_ref[0, 5:6, :]
    o_ref[...] = _layer_norm(alpha * x1_ref[...] + g2 * f, lng_ref[...], lnb_ref[...])


def _ffn_dense(h2, x1, modp, w1, w3, w2, ln_g, ln_b, lay, alpha):
    n, d = x1.shape
    row = lambda i: (i, 0)
    const_specs, const_ops = _param_specs([w1, w3, w2, ln_g, ln_b])
    return pl.pallas_call(
        functools.partial(_ffn_kernel, alpha=alpha),
        grid=(n // ROW_BLOCK,),
        in_specs=[pl.BlockSpec((ROW_BLOCK, d), row), pl.BlockSpec((ROW_BLOCK, d), row), lay.mod_spec(d)]
                 + const_specs,
        out_specs=pl.BlockSpec((ROW_BLOCK, d), row),
        out_shape=jax.ShapeDtypeStruct((n, d), F32),
        name="ffn_dense",
        compiler_params=_params("parallel"),
    )(h2, x1, modp, *const_ops)


def _moe_kernel(be_ref, nb_ref, x_ref, w1_ref, w3_ref, w2_ref, o_ref):
    i = pl.program_id(0)

    @pl.when(i < nb_ref[0])
    def _():
        o_ref[...] = _swiglu(x_ref[...], w1_ref[0], w3_ref[0], w2_ref[0])

    @pl.when(i >= nb_ref[0])
    def _():
        o_ref[...] = jnp.zeros(o_ref.shape, F32)


def _moe_experts(block_e, n_used, xb, w1, w3, w2, lm):
    p, d = xb.shape
    dff = w1.shape[3]
    nblk = p // MOE_ROWS
    one = pl.Buffered(1)
    expert = lambda i, be, nb: (lm, be[i], 0, 0)
    grid_spec = pltpu.PrefetchScalarGridSpec(
        num_scalar_prefetch=2,
        grid=(nblk,),
        in_specs=[pl.BlockSpec((MOE_ROWS, d), lambda i, be, nb: (i, 0)),
                  pl.BlockSpec((None, 1, d, dff), expert, pipeline_mode=one),
                  pl.BlockSpec((None, 1, d, dff), expert, pipeline_mode=one),
                  pl.BlockSpec((None, 1, dff, d), expert, pipeline_mode=one)],
        out_specs=pl.BlockSpec((MOE_ROWS, d), lambda i, be, nb: (i, 0)),
    )
    return pl.pallas_call(
        _moe_kernel,
        grid_spec=grid_spec,
        out_shape=jax.ShapeDtypeStruct((p, d), F32),
        name="moe_experts",
        compiler_params=_params("arbitrary"),
    )(block_e, n_used, xb, w1, w3, w2)


def _combine_kernel(ya_ref, yb_ref, route_ref, x1_ref, mod_ref, lng_ref, lnb_ref, o_ref, *, alpha):
    f = route_ref[:, 2:3] * ya_ref[...] + route_ref[:, 3:4] * yb_ref[...]
    g2 = mod_ref[0, 5:6, :]
    o_ref[...] = _layer_norm(alpha * x1_ref[...] + g2 * f, lng_ref[...], lnb_ref[...])


def _moe_combine(ya, yb, route, x1, modp, ln_g, ln_b, lay, alpha):
    n, d = x1.shape
    row = lambda i: (i, 0)
    return pl.pallas_call(
        functools.partial(_combine_kernel, alpha=alpha),
        grid=(n // ROW_BLOCK,),
        in_specs=[pl.BlockSpec((ROW_BLOCK, d), row), pl.BlockSpec((ROW_BLOCK, d), row),
                  pl.BlockSpec((ROW_BLOCK, V7X_LANES), row), pl.BlockSpec((ROW_BLOCK, d), row),
                  lay.mod_spec(d),
                  _const_spec(ln_g.shape), _const_spec(ln_b.shape)],
        out_specs=pl.BlockSpec((ROW_BLOCK, d), row),
        out_shape=jax.ShapeDtypeStruct((n, d), F32),
        name="moe_combine",
        compiler_params=_params("parallel"),
    )(ya, yb, route, x1, modp, ln_g, ln_b)


def _moe(h2, route, x1, modp, w1, w3, w2, lm, ln_g, ln_b, lay, alpha):
    n, d = x1.shape
    n_exp = w1.shape[1]
    top_e = route[:, 0:2].astype(jnp.int32)
    flat_e = top_e.reshape(-1)
    onehot = (flat_e[:, None] == jnp.arange(n_exp, dtype=jnp.int32)[None, :]).astype(jnp.int32)
    rank = jnp.sum((jnp.cumsum(onehot, axis=0) - onehot) * onehot, axis=1)
    counts = jnp.sum(onehot, axis=0)
    padded = (counts + MOE_ROWS - 1) // MOE_ROWS * MOE_ROWS
    pad_end = jnp.cumsum(padded)
    pad_start = pad_end - padded
    dest = pad_start[flat_e] + rank
    nblk = -(-(2 * n + n_exp * (MOE_ROWS - 1)) // MOE_ROWS)
    p = nblk * MOE_ROWS
    flat_tok = jnp.repeat(jnp.arange(n, dtype=jnp.int32), 2)
    buf_tok = jnp.full((p,), n, jnp.int32).at[dest].set(flat_tok)
    blk_start = jnp.arange(nblk, dtype=jnp.int32) * MOE_ROWS
    block_e = jnp.minimum(jnp.sum((pad_end[None, :] <= blk_start[:, None]).astype(jnp.int32), axis=1), n_exp - 1)
    n_used = (pad_end[-1] // MOE_ROWS).astype(jnp.int32).reshape(1)
    xb = jnp.concatenate([h2, jnp.zeros((1, d), h2.dtype)], axis=0)[buf_tok]
    yb = _moe_experts(block_e, n_used, xb, w1, w3, w2, lm)
    dest2 = dest.reshape(n, 2)
    return _moe_combine(yb[dest2[:, 0]], yb[dest2[:, 1]], route, x1, modp, ln_g, ln_b, lay, alpha)


def _layer(x, modp, lw, l, lay, alpha, z0, ctx_kv, new_skv):
    (bc, tc_len), (bs, ts_len) = lay.passes
    n_ctx = bc * tc_len
    slab, qn, kn, vn, k_new, v_new, gates = _in_proj(x, modp, lw["w_in"], l, lay, lw["n_slab"], *new_skv[1:])
    r, kk, lwf, lwb, bf, bb, ktf, ktb, v, bonus, g = _rwkv_prep(slab, lw, lay)
    yf, yb, s_new = _rwkv_scan(z0, r, kk, lwf, lwb, bf, bb, ktf, ktb, v, lay, new_skv[0])
    yn_ctx = _ctx_attention(qn, kn, vn, bc, tc_len)
    yn_lat = _nbr_attention(qn, kn, vn, ctx_kv[0], ctx_kv[1], lw["bias"], l, bs, ts_len, n_ctx)
    outs = _mix_out(yf, yb, bonus, g, yn_ctx, yn_lat, gates, x, modp, lw, lay, alpha, lw.get("router"))
    if "router" in lw:
        x1, h2, route = outs
        x2 = _moe(h2, route, x1, modp, lw["moe_w1"], lw["moe_w3"], lw["moe_w2"], l // 2, lw["ln2_g"], lw["ln2_b"],
                  lay, alpha)
    else:
        x1, h2 = outs
        x2 = _ffn_dense(h2, x1, modp, (lw["ffn_w1"], l // 2), (lw["ffn_w3"], l // 2), (lw["ffn_w2"], l // 2),
                        lw["ln2_g"], lw["ln2_b"], lay, alpha)
    return x2, s_new, k_new, v_new


def kernel(x_prompt, x_sample, c, state_rwkv, cache_k, cache_v, c_ctx, w_ada, b_ada, w_in, ts_w, rw_w0, rw_w2, rw_a0, rw_a2, rw_g2, rw_kk, rw_ka, rw_u, rw_gn_g, rw_gn_b, na_rpb, w_up_r, w_up_n, w_o, ln1_g, ln1_b, ln2_g, ln2_b, ffn_w1, ffn_w3, ffn_w2, moe_router, moe_w1, moe_w3, moe_w2):
    depth = w_in.shape[0]
    bc, tc_len, d = x_prompt.shape
    bs, ts_len, _ = x_sample.shape
    dr = D_BRANCH
    alpha = float((2 * depth) ** 0.25)
    n_slab = ts_w.shape[2]
    lora_w = rw_w2.shape[2]
    lora_a = rw_a2.shape[2]

    cvec = jnp.zeros((8, d), F32).at[0].set(c_ctx).at[1:1 + bs].set(c)
    mod = _ada_mod(cvec, w_ada, b_ada)

    half = np.kron(np.eye(2, dtype=np.float32), np.ones((HEAD_DIM, HEAD_DIM), np.float32))
    bd = jnp.asarray(half, BF16)
    rows = ts_len // GRID_W

    passes = ((bc, tc_len), (bs, ts_len))
    n_ctx = bc * tc_len
    bias_tables = _nbr_bias_tables(na_rpb, rows)
    x = jnp.concatenate([x_prompt.reshape(n_ctx, d), x_sample.reshape(bs * ts_len, d)], axis=0)
    seq_row = np.concatenate([np.zeros(bc, np.int32), 1 + np.arange(bs, dtype=np.int32)])
    modp = mod[:, seq_row].reshape(depth, bc + bs, 6, d)
    z0_all = jnp.concatenate([jnp.zeros((depth, bc, 2, N_HEADS, HEAD_DIM, HEAD_DIM), F32),
                              jnp.swapaxes(jnp.swapaxes(state_rwkv, 0, 1), -1, -2)], axis=1)
    w_in_b, w_up_r_b, w_up_n_b, w_o_b = (w.astype(BF16) for w in (w_in, w_up_r, w_up_n, w_o))
    ffn_b = [w.astype(BF16) for w in (ffn_w1, ffn_w3, ffn_w2)]
    moe_b = [w.astype(BF16) for w in (moe_w1, moe_w3, moe_w2)]
    new_skv = [jnp.zeros((bc, depth, 2, N_HEADS, HEAD_DIM, HEAD_DIM), F32),
               jnp.zeros((bc, depth, dr, tc_len), F32), jnp.zeros((bc, depth, dr, tc_len), F32)]
    for l in range(depth):
        w2blk = jnp.zeros((2 * lora_w, 2 * dr), F32).at[:lora_w, :dr].set(rw_w2[l, 0]).at[lora_w:, dr:].set(rw_w2[l, 1])
        a2blk = jnp.zeros((2 * lora_a, 2 * dr), F32).at[:lora_a, :dr].set(rw_a2[l, 0]).at[lora_a:, dr:].set(rw_a2[l, 1])
        lw = dict(
            n_slab=n_slab,
            w_in=w_in_b, ts=ts_w[l],
            w0=rw_w0[l].reshape(1, 2 * dr), w2=w2blk.astype(BF16),
            a0=rw_a0[l].reshape(1, 2 * dr), a2=a2blk.astype(BF16),
            g2=rw_g2[l].astype(BF16), kkw=rw_kk[l].reshape(1, dr), ka=rw_ka[l].reshape(1, dr),
            u=rw_u[l].reshape(1, dr), bd=bd,
            gn_g=rw_gn_g[l].reshape(1, dr), gn_b=rw_gn_b[l].reshape(1, dr),
            w_up_r=(w_up_r_b, l), w_up_n=(w_up_n_b, l), w_o=(w_o_b, l),
            ln1_g=ln1_g[l].reshape(1, d), ln1_b=ln1_b[l].reshape(1, d),
            ln2_g=ln2_g[l].reshape(1, d), ln2_b=ln2_b[l].reshape(1, d),
            bias=bias_tables,
        )
        if l % 2 == 0:
            lw.update(ffn_w1=ffn_b[0], ffn_w3=ffn_b[1], ffn_w2=ffn_b[2])
        else:
            lw.update(router=moe_router[l // 2], moe_w1=moe_b[0], moe_w3=moe_b[1], moe_w2=moe_b[2])

        kv = (cache_k[:, l].reshape(bs, -1, dr), cache_v[:, l].reshape(bs, -1, dr))
        x, *new_skv = _layer(x, modp, lw, l, _Layout(passes, l), alpha, z0_all, kv, new_skv)

    cache_out = lambda a: jnp.transpose(a.reshape(bc, depth, N_HEADS, HEAD_DIM, tc_len), (0, 1, 4, 2, 3))
    return (x[:n_ctx].reshape(bc, tc_len, d), x[n_ctx:].reshape(bs, ts_len, d),
            new_skv[0], cache_out(new_skv[1]), cache_out(new_skv[2]))
```

```python
import functools

import numpy as np
import jax
import jax.numpy as jnp
from jax import lax
from jax.experimental import pallas as pl
from jax.experimental.pallas import tpu as pltpu

F32 = jnp.float32
BF16 = jnp.bfloat16
HIGHEST = lax.Precision.HIGHEST

N_HEADS = 8
HEAD_DIM = 64
D_BRANCH = N_HEADS * HEAD_DIM
V7X_LANES = 128
ROW_BLOCK = 256
SCAN_CHUNK = 64
SCAN_CHUNKS_PER_STEP = 2
MOE_ROWS = 256
KH_MAX = 8
KW = 16
GRID_W = 64
LN_EPS = 1e-5
GN_EPS = 64e-5
NEG_BIG = -1e30
VMEM_LIMIT = 56 * 1024 * 1024


def _params(*sem):
    return pltpu.CompilerParams(dimension_semantics=sem, vmem_limit_bytes=VMEM_LIMIT)


def _const_spec(shape):
    nd = len(shape)
    return pl.BlockSpec(shape, lambda *_: (0,) * nd, pipeline_mode=pl.Buffered(1))


def _layer_spec(stack, l):
    nd = stack.ndim - 1
    return pl.BlockSpec((None,) + stack.shape[1:], lambda *_: (l,) + (0,) * nd, pipeline_mode=pl.Buffered(1))


def _param_specs(params):
    specs = [_layer_spec(*p) if isinstance(p, tuple) else _const_spec(p.shape) for p in params]
    return specs, [p[0] if isinstance(p, tuple) else p for p in params]


class _Layout:
    def __init__(self, passes, layer=0):
        self.passes = passes
        self.layer = layer
        self.n_seq = sum(b for b, _ in passes)

    def _per_block(self, fn):
        def at(i):
            out, blk0, seq0 = None, 0, 0
            for b, t in self.passes:
                per = t // ROW_BLOCK
                val = fn(i - blk0, per, seq0)
                out = val if out is None else jnp.where(i >= blk0, val, out)
                blk0 += b * per
                seq0 += b
            return out
        return at

    def seq_of_block(self, i):
        return self._per_block(lambda j, per, s0: s0 + j // per)(i)

    def is_first(self, i):
        return self._per_block(lambda j, per, s0: (j % per) == 0)(i)

    def is_last(self, i):
        return self._per_block(lambda j, per, s0: (j % per) == per - 1)(i)

    def mod_spec(self, d):
        return pl.BlockSpec((None, 1, 6, d), lambda i: (self.layer, self.seq_of_block(i), 0, 0))


def _dot(a, b):
    return jnp.dot(a, b, preferred_element_type=F32)


def _dot_nt(a, b):
    return lax.dot_general(a, b, (((1,), (1,)), ((), ())), preferred_element_type=F32)


def _dot_tn(a, b):
    return lax.dot_general(a, b, (((0,), (0,)), ((), ())), preferred_element_type=F32)


def _mm1(a, b, f=_dot):
    return f(a.astype(BF16), b.astype(BF16))


def _mm_mask_lhs(m, x):
    hi = x.astype(BF16)
    lo = (x - hi.astype(F32)).astype(BF16)
    return _dot(m, hi) + _dot(m, lo)


def _sigmoid(x):
    return 1.0 / (1.0 + jnp.exp(-x))


def _head_sum(x, bd):
    outs = []
    for j in range(x.shape[1] // V7X_LANES):
        xs = x[:, j * V7X_LANES:(j + 1) * V7X_LANES]
        hi = xs.astype(BF16)
        lo = (xs - hi.astype(F32)).astype(BF16)
        outs.append(_dot(hi, bd) + _dot(lo, bd))
    return jnp.concatenate(outs, axis=1)


def _layer_norm(z, g, b):
    mu = jnp.mean(z, axis=-1, keepdims=True)
    zc = z - mu
    var = jnp.mean(zc * zc, axis=-1, keepdims=True)
    return zc * lax.rsqrt(var + LN_EPS) * g + b


def _ada_kernel(c_ref, w_ref, b_ref, o_ref):
    cs = c_ref[...]
    s = cs * _sigmoid(cs)
    o_ref[0] = jnp.dot(s, w_ref[0], preferred_element_type=F32, precision=HIGHEST) + b_ref[0]


def _ada_mod(cvec, w_ada, b_ada):
    depth, d, n6 = w_ada.shape
    tn = n6 // 4
    return pl.pallas_call(
        _ada_kernel,
        grid=(depth, n6 // tn),
        in_specs=[pl.BlockSpec((8, d), lambda l, j: (0, 0)),
                  pl.BlockSpec((1, d, tn), lambda l, j: (l, 0, j)),
                  pl.BlockSpec((1, 1, tn), lambda l, j: (l, 0, j))],
        out_specs=pl.BlockSpec((1, 8, tn), lambda l, j: (l, 0, j)),
        out_shape=jax.ShapeDtypeStruct((depth, 8, n6), F32),
        name="ada_mod",
        compiler_params=_params("parallel", "parallel"),
    )(cvec, w_ada, b_ada.reshape(depth, 1, n6))


def _inproj_kernel(x_ref, mod_ref, w_ref, kbuf_ref, vbuf_ref, slab_ref, q_ref, k_ref, v_ref, kc_ref, vc_ref, gate_ref,
                   *, n_slab, nb_cache):
    del kbuf_ref, vbuf_ref
    sh = mod_ref[0, 0:1, :]
    sc = mod_ref[0, 1:2, :]
    h = (x_ref[...] * (1.0 + sc) + sh).astype(BF16)
    dn = D_BRANCH
    slab_ref[...] = _dot(h, w_ref[:, 0:n_slab])
    q_ref[...] = _dot(h, w_ref[:, n_slab:n_slab + dn]).astype(q_ref.dtype)
    k = _dot(h, w_ref[:, n_slab + dn:n_slab + 2 * dn])
    v = _dot(h, w_ref[:, n_slab + 2 * dn:n_slab + 3 * dn])
    k_ref[...] = k.astype(k_ref.dtype)
    v_ref[...] = v.astype(v_ref.dtype)

    @pl.when(pl.program_id(0) < nb_cache)
    def _():
        kc_ref[0] = k.T
        vc_ref[0] = v.T

    gate_ref[...] = _sigmoid(_dot(h, w_ref[:, n_slab + 3 * dn:]))


def _in_proj(x, modp, w_in, l, lay, n_slab, cache_k, cache_v):
    n, d = x.shape
    n_cols = w_in.shape[2]
    dn = D_BRANCH
    n_gate = n_cols - n_slab - 3 * dn
    b_cache, t_cache = lay.passes[0]
    per_seq = t_cache // ROW_BLOCK
    nb_cache = b_cache * per_seq
    row = lambda i: (i, 0)

    def cache_block(i):
        ic = jnp.minimum(i, nb_cache - 1)
        return ic // per_seq, l, 0, ic % per_seq

    cache_spec = pl.BlockSpec((1, None, dn, ROW_BLOCK), cache_block)
    any_spec = pl.BlockSpec(memory_space=pl.ANY)
    return pl.pallas_call(
        functools.partial(_inproj_kernel, n_slab=n_slab, nb_cache=nb_cache),
        grid=(n // ROW_BLOCK,),
        in_specs=[pl.BlockSpec((ROW_BLOCK, d), row),
                  lay.mod_spec(d),
                  _layer_spec(w_in, l),
                  any_spec, any_spec],
        out_specs=[pl.BlockSpec((ROW_BLOCK, n_slab), row),
                   pl.BlockSpec((ROW_BLOCK, dn), row), pl.BlockSpec((ROW_BLOCK, dn), row),
                   pl.BlockSpec((ROW_BLOCK, dn), row),
                   cache_spec, cache_spec,
                   pl.BlockSpec((ROW_BLOCK, n_gate), row)],
        out_shape=[jax.ShapeDtypeStruct((n, n_slab), F32),
                   jax.ShapeDtypeStruct((n, dn), BF16), jax.ShapeDtypeStruct((n, dn), BF16),
                   jax.ShapeDtypeStruct((n, dn), BF16),
                   jax.ShapeDtypeStruct(cache_k.shape, F32), jax.ShapeDtypeStruct(cache_v.shape, F32),
                   jax.ShapeDtypeStruct((n, n_gate), F32)],
        input_output_aliases={3: 4, 4: 5},
        name="in_proj",
        compiler_params=_params("arbitrary"),
    )(x, modp, w_in, cache_k, cache_v)


def _prep_kernel(cur_ref, prev_ref, next_ref, ts_ref, w0_ref, w2_ref, a0_ref, a2_ref, g2_ref,
                 kkw_ref, ka_ref, u_ref, bd_ref,
                 r_o, kk_o, lwf_o, lwb_o, bf_o, bb_o, ktf_o, ktb_o, v_o, bonus_o, g_o,
                 *, lay, lora_w, lora_a):
    i = pl.program_id(0)
    tb, n_slab = cur_ref.shape
    dr = D_BRANCH
    cur = cur_ref[...]
    first = lay.is_first(i)
    last = lay.is_last(i)
    prow = jnp.where(first, 0.0, prev_ref[7:8, :])
    nrow = jnp.where(last, 0.0, next_ref[0:1, :])
    rid = lax.broadcasted_iota(jnp.int32, (tb, n_slab), 0)
    xp = jnp.where(rid == 0, prow, pltpu.roll(cur, 1, 0))
    xn = jnp.where(rid == tb - 1, nrow, pltpu.roll(cur, tb - 1, 0))
    s = ts_ref[0:1, :] * xp + ts_ref[1:2, :] * cur + ts_ref[2:3, :] * xn

    r = s[:, 0:dr]
    k = s[:, dr:2 * dr]
    v = s[:, 2 * dr:3 * dr]
    o = 3 * dr
    wd = s[:, o:o + 2 * lora_w]
    ad = s[:, o + 2 * lora_w:o + 2 * lora_w + 2 * lora_a]
    gd = s[:, o + 2 * lora_w + 2 * lora_a:]

    wl = w0_ref[...] + _dot(jnp.tanh(wd).astype(BF16), w2_ref[...])
    logw = -float(np.exp(-0.5)) * _sigmoid(wl)
    a = _sigmoid(a0_ref[...] + _dot(ad.astype(BF16), a2_ref[...]))
    g = _dot(_sigmoid(gd).astype(BF16), g2_ref[...])

    bd = bd_ref[...]
    kk = k * kkw_ref[...]
    kk = kk * lax.rsqrt(_head_sum(kk * kk, bd) + 1e-12)
    ka = ka_ref[...]
    a_f = a[:, :dr]
    a_b = a[:, dr:]
    kt_f = k * (1.0 + (a_f - 1.0) * ka)
    kt_b = k * (1.0 + (a_b - 1.0) * ka)
    bonus = _head_sum(r * (0.5 * (kt_f + kt_b)) * u_ref[...], bd) * v

    r_o[...] = r
    kk_o[...] = kk
    lwf_o[...] = logw[:, :dr]
    lwb_o[...] = logw[:, dr:]
    bf_o[...] = kk * a_f
    bb_o[...] = kk * a_b
    ktf_o[...] = kt_f
    ktb_o[...] = kt_b
    v_o[...] = v
    bonus_o[...] = bonus
    g_o[...] = g


def _rwkv_prep(slab, lw, lay):
    n, n_slab = slab.shape
    dr = D_BRANCH
    sub = ROW_BLOCK // 8
    lora_w = lw["w2"].shape[0] // 2
    lora_a = lw["a2"].shape[0] // 2
    row = lambda i: (i, 0)
    consts = [lw["ts"], lw["w0"], lw["w2"], lw["a0"], lw["a2"], lw["g2"], lw["kkw"], lw["ka"], lw["u"], lw["bd"]]
    outs = pl.pallas_call(
        functools.partial(_prep_kernel, lay=lay, lora_w=lora_w, lora_a=lora_a),
        grid=(n // ROW_BLOCK,),
        in_specs=[pl.BlockSpec((ROW_BLOCK, n_slab), row),
                  pl.BlockSpec((8, n_slab), lambda i: (jnp.maximum(i * sub - 1, 0), 0)),
                  pl.BlockSpec((8, n_slab), lambda i: (jnp.minimum((i + 1) * sub, n // 8 - 1), 0))]
                 + [_const_spec(c.shape) for c in consts],
        out_specs=[pl.BlockSpec((ROW_BLOCK, dr), row)] * 11,
        out_shape=[jax.ShapeDtypeStruct((n, dr), F32)] * 11,
        name="rwkv_prep",
        compiler_params=_params("parallel"),
    )(slab, slab, slab, *consts)
    return outs


def _tri_masks(rev):
    c = SCAN_CHUNK
    ti = lax.broadcasted_iota(jnp.int32, (c, 2 * c), 0)
    ii = lax.broadcasted_iota(jnp.int32, (c, 2 * c), 1) % c
    if rev:
        ti, ii = c - 1 - ti, c - 1 - ii
    one = lambda m: jnp.where(m, 1.0, 0.0)
    levels = []
    s = 1
    while s < c:
        same = (ti // (2 * s)) == (ii // (2 * s))
        levels.append(one(same & ((ti % (2 * s)) >= s) & ((ii % (2 * s)) < s)))
        s *= 2
    return one(ii <= ti), one(ii < ti), levels


def _scan_kernel(fwd_tab, bwd_tab, seq_tab, first_tab, last_tab,
                 z0_ref, rf, kkf, lwf, bf, ktf, vf, rb, kkb, lwb, bb, ktb, vb, sbuf_ref, yf_o, yb_o, sfin_o, z_scr,
                 *, n_state_out):
    del sbuf_ref
    g_step = pl.program_id(0)
    c = SCAN_CHUNK
    lanes = V7X_LANES
    npair = N_HEADS // 2
    assert c == HEAD_DIM and 2 * HEAD_DIM == lanes

    row_blk = lax.broadcasted_iota(jnp.int32, (lanes, lanes), 0) // HEAD_DIM
    col_blk = lax.broadcasted_iota(jnp.int32, (lanes, lanes), 1) // HEAD_DIM
    bd_mask = jnp.where(row_blk == col_blk, 1.0, 0.0)
    eye_full = jnp.where(lax.broadcasted_iota(jnp.int32, (lanes, lanes), 0)
                         == lax.broadcasted_iota(jnp.int32, (lanes, lanes), 1), 1.0, 0.0)
    lane_c = lax.broadcasted_iota(jnp.int32, (c, lanes), 1)
    lo = jnp.where(lane_c < HEAD_DIM, 1.0, 0.0)
    hi = 1.0 - lo
    eye_pair = jnp.where(lax.broadcasted_iota(jnp.int32, (c, lanes), 0) == lane_c % HEAD_DIM, 1.0, 0.0)

    def bd(x):
        return jnp.concatenate([x * lo, x * hi], axis=0)

    @pl.when(first_tab[g_step] == 1)
    def _():
        for d in range(2):
            for p in range(npair):
                za = jnp.concatenate([z0_ref[0, d, 2 * p], z0_ref[0, d, 2 * p + 1]], axis=1)
                z_scr[d * npair + p] = bd(za)

    refs = ((rf, kkf, lwf, bf, ktf, vf, yf_o), (rb, kkb, lwb, bb, ktb, vb, yb_o))
    masks = (_tri_masks(False), _tri_masks(True))
    cum_lhs = [m[0][:, 0:c].astype(BF16) for m in masks]
    last_row = (c - 1, 0)
    nck = rf.shape[0] // c
    units = [(d, p, ck) for d in range(2) for p in range(npair) for ck in range(nck)]

    def tile(d, p, ck):
        row0 = (nck - 1 - ck if d else ck) * c
        return slice(row0, row0 + c), slice(p * lanes, (p + 1) * lanes)

    def per(fn):
        return [fn(i, d, tile(d, p, ck)) for i, (d, p, ck) in enumerate(units)]

    lw = per(lambda i, d, sl: refs[d][2][sl])
    cum = per(lambda i, d, sl: _mm_mask_lhs(cum_lhs[d], lw[i]))
    tot = per(lambda i, d, sl: jnp.broadcast_to(cum[i][last_row[d]:last_row[d] + 1], (c, lanes)))
    r_s = per(lambda i, d, sl: refs[d][0][sl] * jnp.exp(cum[i]))
    k_s = per(lambda i, d, sl: refs[d][1][sl] * jnp.exp(cum[i] - lw[i]))
    e_neg = per(lambda i, d, sl: jnp.exp(-cum[i]))
    e_end = per(lambda i, d, sl: jnp.exp(tot[i] - cum[i]))
    b = per(lambda i, d, sl: refs[d][3][sl])
    kt = per(lambda i, d, sl: refs[d][4][sl])
    v = per(lambda i, d, sl: refs[d][5][sl])
    v_bd = per(lambda i, d, sl: bd(v[i]))
    kr = per(lambda i, d, sl: jnp.concatenate([k_s[i], r_s[i]], axis=0))
    ab = per(lambda i, d, sl: _mm1(kr[i], bd(b[i] * e_neg[i]), _dot_nt))
    ak = per(lambda i, d, sl: _mm1(kr[i], bd(kt[i] * e_neg[i]), _dot_nt))
    a_b = per(lambda i, d, sl: ab[i][0:c] * masks[d][1])
    b_b = per(lambda i, d, sl: ab[i][c:] * masks[d][0])
    a_kt = per(lambda i, d, sl: ak[i][0:c] * masks[d][1])
    b_kt = per(lambda i, d, sl: ak[i][c:] * masks[d][0])
    dm = per(lambda i, d, sl: eye_pair - a_b[i] * masks[d][2][0])
    for lvl in range(1, len(masks[0][2])):
        zd = per(lambda i, d, sl: _mm1(a_b[i] * masks[d][2][lvl], bd(dm[i])))
        dm = per(lambda i, d, sl: dm[i] - _mm1(dm[i], bd(zd[i])))
    akv = per(lambda i, d, sl: _mm1(jnp.concatenate([a_kt[i], b_kt[i]], axis=0), v_bd[i]))
    w1 = [x[0:c] for x in akv]
    g = per(lambda i, d, sl: _mm1(dm[i], jnp.concatenate([bd(k_s[i]), bd(w1[i])], axis=1)))
    bg = per(lambda i, d, sl: _mm1(b_b[i], jnp.concatenate([bd(g[i][:, 0:lanes]), bd(g[i][:, lanes:])], axis=1)))
    q = per(lambda i, d, sl: r_s[i] - bg[i][:, 0:lanes])
    y0 = per(lambda i, d, sl: akv[i][c:] - bg[i][:, lanes:])
    cg = per(lambda i, d, sl: _mm1(b[i] * e_end[i], g[i], _dot_tn))
    phi = per(lambda i, d, sl: eye_full * jnp.exp(jnp.concatenate([tot[i], tot[i]], axis=0))
              - cg[i][:, 0:lanes] * bd_mask)
    psi = per(lambda i, d, sl: (_mm1(kt[i] * e_end[i], v[i], _dot_tn) - cg[i][:, lanes:]) * bd_mask)
    z = [z_scr[s] for s in range(2 * npair)]
    for ck in range(nck):
        sel = [(i, d, p) for i, (d, p, k2) in enumerate(units) if k2 == ck]
        qz = [_mm1(jnp.concatenate([q[i], phi[i]], axis=0), z[d * npair + p]) for i, d, p in sel]
        y = [x[0:c] + y0[i] for x, (i, d, p) in zip(qz, sel)]
        z = [x[c:] + psi[i] for x, (i, d, p) in zip(qz, sel)]
        for yy, (i, d, p) in zip(y, sel):
            refs[d][6][tile(d, p, ck)] = yy
    for s in range(2 * npair):
        z_scr[s] = z[s]

    @pl.when(jnp.logical_and(last_tab[g_step] == 1, seq_tab[g_step] < n_state_out))
    def _():
        for d in range(2):
            for p in range(npair):
                zt = z_scr[d * npair + p].T
                sfin_o[0, d, 2 * p] = zt[0:HEAD_DIM, 0:HEAD_DIM]
                sfin_o[0, d, 2 * p + 1] = zt[HEAD_DIM:, HEAD_DIM:]


def _scan_tables(lay):
    blk = SCAN_CHUNK * SCAN_CHUNKS_PER_STEP
    rows, blk0, seq0 = [], 0, 0
    for b, t in lay.passes:
        nj = t // blk
        for s in range(b):
            for j in range(nj):
                rows.append((blk0 + s * nj + j, blk0 + s * nj + nj - 1 - j, seq0 + s, int(j == 0), int(j == nj - 1)))
        blk0 += b * nj
        seq0 += b
    return [jnp.asarray(col, jnp.int32) for col in zip(*rows)]


def _rwkv_scan(z0, r, kk, lwf, lwb, bf, bb, ktf, ktb, v, lay, new_state):
    blk = SCAN_CHUNK * SCAN_CHUNKS_PER_STEP
    n, dr = r.shape
    n_out = new_state.shape[0]
    assert n_out == lay.passes[0][0]
    tabs = _scan_tables(lay)
    fwd = pl.BlockSpec((blk, dr), lambda g, fw, bw, sq, fi, la: (fw[g], 0))
    bwd = pl.BlockSpec((blk, dr), lambda g, fw, bw, sq, fi, la: (bw[g], 0))
    st_out = pl.BlockSpec((1, None, 2, N_HEADS, HEAD_DIM, HEAD_DIM),
                          lambda g, fw, bw, sq, fi, la: (jnp.minimum(sq[g], n_out - 1), lay.layer, 0, 0, 0, 0))
    st_in = pl.BlockSpec((None, 1, 2, N_HEADS, HEAD_DIM, HEAD_DIM),
                         lambda g, fw, bw, sq, fi, la: (lay.layer, sq[g], 0, 0, 0, 0))
    grid_spec = pltpu.PrefetchScalarGridSpec(
        num_scalar_prefetch=len(tabs),
        grid=(int(tabs[0].shape[0]),),
        in_specs=[st_in] + [fwd] * 6 + [bwd] * 6 + [pl.BlockSpec(memory_space=pl.ANY)],
        out_specs=[fwd, bwd, st_out],
        scratch_shapes=[pltpu.VMEM((N_HEADS, V7X_LANES, V7X_LANES), F32)],
    )
    n_in = len(tabs) + 13
    return pl.pallas_call(
        functools.partial(_scan_kernel, n_state_out=n_out),
        grid_spec=grid_spec,
        out_shape=[jax.ShapeDtypeStruct((n, dr), F32),
                   jax.ShapeDtypeStruct((n, dr), F32),
                   jax.ShapeDtypeStruct(new_state.shape, F32)],
        input_output_aliases={n_in: 2},
        name="rwkv_scan",
        compiler_params=_params("arbitrary"),
    )(*tabs, z0, r, kk, lwf, bf, ktf, v, r, kk, lwb, bb, ktb, v, new_state)


def _softmax_rows(parts):
    m = parts[0].max(axis=-1, keepdims=True)
    for p in parts[1:]:
        m = jnp.maximum(m, p.max(axis=-1, keepdims=True))
    es = [jnp.exp(p - m) for p in parts]
    den = es[0].sum(axis=-1, keepdims=True)
    for e in es[1:]:
        den = den + e.sum(axis=-1, keepdims=True)
    inv = 1.0 / den
    return [e * inv for e in es]


def _ctx_attn_kernel(q_ref, k_ref, v_ref, o_ref):
    scale = HEAD_DIM ** -0.5
    npair = q_ref.shape[1] // V7X_LANES
    tile = lambda ref, p: ref[:, p * V7X_LANES:(p + 1) * V7X_LANES].astype(BF16)
    qb = [tile(q_ref, p) for p in range(npair)]
    k = [tile(k_ref, p) for p in range(npair)]
    v = [tile(v_ref, p) for p in range(npair)]
    first_head = lax.broadcasted_iota(jnp.int32, qb[0].shape, 1) < HEAD_DIM
    head_lanes = (first_head, jnp.logical_not(first_head))
    units = [(p, hh) for p in range(npair) for hh in range(2)]

    def per(fn):
        return [fn(u, p, hh) for u, (p, hh) in enumerate(units)]

    q = per(lambda u, p, hh: jnp.where(head_lanes[hh], qb[p], jnp.zeros_like(qb[p])))
    s = per(lambda u, p, hh: _dot_nt(q[u], k[p]) * scale)
    e = per(lambda u, p, hh: jnp.exp(s[u] - s[u].max(axis=-1, keepdims=True)))
    pr = per(lambda u, p, hh: (e[u] * (1.0 / e[u].sum(axis=-1, keepdims=True))).astype(BF16))
    out = per(lambda u, p, hh: _dot(pr[u], v[p]))
    for p in range(npair):
        o_ref[:, p * V7X_LANES:(p + 1) * V7X_LANES] = jnp.where(first_head, out[2 * p], out[2 * p + 1]).astype(o_ref.dtype)


def _ctx_attention(q, k, v, b_len, t_len):
    n = b_len * t_len
    seq = pl.BlockSpec((t_len, D_BRANCH), lambda b: (b, 0))
    return pl.pallas_call(
        _ctx_attn_kernel,
        grid=(b_len,),
        in_specs=[seq, seq, seq],
        out_specs=seq,
        out_shape=jax.ShapeDtypeStruct((n, D_BRANCH), BF16),
        name="ctx_attn",
        compiler_params=_params("parallel"),
    )(q, k, v)


def _nbr_attn_kernel(q_ref, k_ref, v_ref, kc_ref, vc_ref, bias_ref, o_ref, *, rows, kh):
    scale = HEAD_DIM ** -0.5
    win = kh * GRID_W

    group = 8 if rows % 8 == 0 else 1
    units = [(r, hh) for r in range(group) for hh in range(2)]
    first_head = lax.broadcasted_iota(jnp.int32, (GRID_W, V7X_LANES), 1) < HEAD_DIM
    head_lanes = (first_head, jnp.logical_not(first_head))

    def body(gi, carry):
        def per(fn):
            return [fn(u, r, hh) for u, (r, hh) in enumerate(units)]

        i = [gi * group + r for r in range(group)]
        rs = [jnp.clip(x - kh // 2, 0, rows - kh) for x in i]
        q0 = [pl.multiple_of(x * GRID_W, GRID_W) for x in i]
        k0 = [pl.multiple_of(x * GRID_W, GRID_W) for x in rs]
        qb = [q_ref[pl.ds(x, GRID_W), :].astype(BF16) for x in q0]
        kw = [k_ref[pl.ds(x, win), :].astype(BF16) for x in k0]
        vw = [v_ref[pl.ds(x, win), :].astype(BF16) for x in k0]
        kc = kc_ref[0].astype(BF16)
        vc = vc_ref[0].astype(BF16)
        q = per(lambda u, r, hh: jnp.where(head_lanes[hh], qb[r], jnp.zeros_like(qb[r])))
        off0 = [(KH_MAX - 1) - (i[r] - rs[r]) for r in range(group)]
        bias = per(lambda u, r, hh: jnp.concatenate([bias_ref[hh, off0[r] + 2 * a2] for a2 in range(kh // 2)], axis=1))
        s_loc = per(lambda u, r, hh: _dot_nt(q[u], kw[r]) * scale + bias[u])
        s_ctx = per(lambda u, r, hh: _dot_nt(q[u], kc) * scale)
        m = per(lambda u, r, hh: jnp.maximum(s_loc[u].max(axis=-1, keepdims=True), s_ctx[u].max(axis=-1, keepdims=True)))
        e_loc = per(lambda u, r, hh: jnp.exp(s_loc[u] - m[u]))
        e_ctx = per(lambda u, r, hh: jnp.exp(s_ctx[u] - m[u]))
        inv = per(lambda u, r, hh: 1.0 / (e_loc[u].sum(axis=-1, keepdims=True) + e_ctx[u].sum(axis=-1, keepdims=True)))
        out = per(lambda u, r, hh: _dot((e_loc[u] * inv[u]).astype(BF16), vw[r])
                  + _dot((e_ctx[u] * inv[u]).astype(BF16), vc))
        for r in range(group):
            o_ref[pl.ds(q0[r], GRID_W), :] = jnp.where(first_head, out[2 * r], out[2 * r + 1]).astype(o_ref.dtype)
        return carry

    lax.fori_loop(0, rows // group, body, 0)


def _nbr_attention(q, k, v, k_ctx, v_ctx, bias, l, b_len, t_len, row0):
    n = b_len * t_len
    npair = N_HEADS // 2
    rows = t_len // GRID_W
    kh = min(KH_MAX, rows)
    c_len = k_ctx.shape[1]
    assert row0 % t_len == 0 and kh % 2 == 0
    seq0 = row0 // t_len
    pair = pl.BlockSpec((t_len, V7X_LANES), lambda b, p: (seq0 + b, p))
    return pl.pallas_call(
        functools.partial(_nbr_attn_kernel, rows=rows, kh=kh),
        grid=(b_len, npair),
        in_specs=[pair, pair, pair,
                  pl.BlockSpec((1, c_len, V7X_LANES), lambda b, p: (b, 0, p)),
                  pl.BlockSpec((1, c_len, V7X_LANES), lambda b, p: (b, 0, p)),
                  pl.BlockSpec((None, 2) + bias.shape[2:], lambda b, p: (l, p, 0, 0, 0))],
        out_specs=pl.BlockSpec((t_len, V7X_LANES), lambda b, p: (b, p)),
        out_shape=jax.ShapeDtypeStruct((n, D_BRANCH), BF16),
        name="nbr_attn",
        compiler_params=_params("parallel", "parallel"),
    )(q, k, v, k_ctx, v_ctx, bias)


def _nbr_bias_tables(rpb):
    w = np.arange(GRID_W)
    c = np.arange(GRID_W)
    col_start = np.clip(w - KW // 2, 0, GRID_W - KW)
    inside = (c[None, :] >= col_start[:, None]) & (c[None, :] < col_start[:, None] + KW)
    col_off = c[None, :] - w[:, None] + (KW - 1)
    sel_col = ((np.arange(2 * KW - 1)[:, None, None] == col_off[None]) & inside[None]).astype(np.float32)
    toep = jnp.einsum("lhrj,jwc->lhrwc", rpb, sel_col, precision=HIGHEST)
    toep = jnp.where(inside[None, None, None], toep, NEG_BIG)
    return jnp.concatenate([toep[:, :, :-1], toep[:, :, 1:]], axis=-1)


def _top2_route(logits, n_exp):
    lane = lax.broadcasted_iota(jnp.int32, logits.shape, 1)
    lane_f = lane.astype(F32)
    lg = jnp.where(lane < n_exp, logits, -jnp.inf)
    m1 = lg.max(axis=-1, keepdims=True)
    i1 = jnp.where(lg == m1, lane_f, float(V7X_LANES)).min(axis=-1, keepdims=True)
    lg2 = jnp.where(lane_f == i1, -jnp.inf, lg)
    m2 = lg2.max(axis=-1, keepdims=True)
    i2 = jnp.where(lg2 == m2, lane_f, float(V7X_LANES)).min(axis=-1, keepdims=True)
    e = jnp.exp(m2 - m1)
    g1 = 1.0 / (1.0 + e)
    g2 = e * g1
    out = jnp.where(lane == 0, i1, 0.0)
    out = jnp.where(lane == 1, i2, out)
    out = jnp.where(lane == 2, g1, out)
    return jnp.where(lane == 3, g2, out)


def _mix_kernel(yf_ref, yb_ref, bonus_ref, g_ref, yn0_ref, yn1_ref, gate_ref, x_ref, mod_ref,
                gng_ref, gnb_ref, bd_ref, wur_ref, wun_ref, wo_ref, lng_ref, lnb_ref, *rest,
                alpha, n_exp, nb_first):
    if n_exp:
        rhi_ref, rlo_ref, x1_o, h2_o, route_o = rest
    else:
        x1_o, h2_o = rest
    d = x_ref.shape[1]
    yn = jnp.where(pl.program_id(0) < nb_first, yn0_ref[...], yn1_ref[...])
    bd = bd_ref[...]
    y = yf_ref[...] + yb_ref[...]
    inv_n = 1.0 / HEAD_DIM
    mu = _head_sum(y, bd) * inv_n
    yc = y - mu
    var = _head_sum(yc * yc, bd) * inv_n
    yr = yc * lax.rsqrt(var + GN_EPS) * gng_ref[...] + gnb_ref[...]
    yr = (yr + bonus_ref[...]) * g_ref[...]
    m = (gate_ref[:, 0:d] * _dot(yr.astype(BF16), wur_ref[...])
         + gate_ref[:, d:] * _dot(yn, wun_ref[...]))
    o = _dot(m.astype(BF16), wo_ref[...])
    g1 = mod_ref[0, 2:3, :]
    sh2 = mod_ref[0, 3:4, :]
    sc2 = mod_ref[0, 4:5, :]
    x1 = _layer_norm(alpha * x_ref[...] + g1 * o, lng_ref[...], lnb_ref[...])
    x1_o[...] = x1
    h2 = x1 * (1.0 + sc2) + sh2
    h2_o[...] = h2.astype(BF16)
    if n_exp:
        h_hi = h2.astype(BF16)
        h_lo = (h2 - h_hi.astype(F32)).astype(BF16)
        logits = _dot(h_hi, rhi_ref[...]) + (_dot(h_hi, rlo_ref[...]) + _dot(h_lo, rhi_ref[...]))
        route_o[...] = _top2_route(logits, n_exp)


def _mix_out(yf, yb, bonus, g, yn0, yn1, gates, x, modp, lw, lay, alpha, router):
    n, d = x.shape
    dr = D_BRANCH
    row = lambda i: (i, 0)
    nb_first = yn0.shape[0] // ROW_BLOCK
    yt_spec = pl.BlockSpec((ROW_BLOCK, dr), row)
    consts = [lw["gn_g"], lw["gn_b"], lw["bd"], lw["w_up_r"], lw["w_up_n"], lw["w_o"], lw["ln1_g"], lw["ln1_b"]]
    n_exp = 0
    out_specs = [pl.BlockSpec((ROW_BLOCK, d), row), pl.BlockSpec((ROW_BLOCK, d), row)]
    out_shape = [jax.ShapeDtypeStruct((n, d), F32), jax.ShapeDtypeStruct((n, d), BF16)]
    if router is not None:
        n_exp = router.shape[1]
        r_pad = jnp.pad(router, ((0, 0), (0, V7X_LANES - n_exp)))
        r_hi = r_pad.astype(BF16)
        consts += [r_hi, (r_pad - r_hi.astype(F32)).astype(BF16)]
        out_specs.append(pl.BlockSpec((ROW_BLOCK, V7X_LANES), row))
        out_shape.append(jax.ShapeDtypeStruct((n, V7X_LANES), F32))
    const_specs, const_ops = _param_specs(consts)
    return pl.pallas_call(
        functools.partial(_mix_kernel, alpha=alpha, n_exp=n_exp, nb_first=nb_first),
        grid=(n // ROW_BLOCK,),
        in_specs=[yt_spec, yt_spec,
                  pl.BlockSpec((ROW_BLOCK, dr), row), pl.BlockSpec((ROW_BLOCK, dr), row),
                  pl.BlockSpec((ROW_BLOCK, dr), lambda i: (jnp.minimum(i, nb_first - 1), 0)),
                  pl.BlockSpec((ROW_BLOCK, dr), lambda i: (jnp.maximum(i - nb_first, 0), 0)),
                  pl.BlockSpec((ROW_BLOCK, 2 * d), row),
                  pl.BlockSpec((ROW_BLOCK, d), row),
                  lay.mod_spec(d)]
                 + const_specs,
        out_specs=out_specs,
        out_shape=out_shape,
        name="mix_out",
        compiler_params=_params("parallel"),
    )(yf, yb, bonus, g, yn0, yn1, gates, x, modp, *const_ops)


def _swiglu(h, w1, w3, w2):
    u = _dot(h, w1)
    act = u * _sigmoid(u) * _dot(h, w3)
    return _dot(act.astype(BF16), w2)


def _ffn_kernel(h_ref, x1_ref, mod_ref, w1_ref, w3_ref, w2_ref, lng_ref, lnb_ref, o_ref, *, alpha):
    f = _swiglu(h_ref[...], w1_ref[...], w3_ref[...], w2_ref[...])
    g2 = mod_ref[0, 5:6, :]
    o_ref[...] = _layer_norm(alpha * x1_ref[...] + g2 * f, lng_ref[...], lnb_ref[...])


def _ffn_dense(h2, x1, modp, w1, w3, w2, ln_g, ln_b, lay, alpha):
    n, d = x1.shape
    row = lambda i: (i, 0)
    const_specs, const_ops = _param_specs([w1, w3, w2, ln_g, ln_b])
    return pl.pallas_call(
        functools.partial(_ffn_kernel, alpha=alpha),
        grid=(n // ROW_BLOCK,),
        in_specs=[pl.BlockSpec((ROW_BLOCK, d), row), pl.BlockSpec((ROW_BLOCK, d), row), lay.mod_spec(d)]
                 + const_specs,
        out_specs=pl.BlockSpec((ROW_BLOCK, d), row),
        out_shape=jax.ShapeDtypeStruct((n, d), F32),
        name="ffn_dense",
        compiler_params=_params("parallel"),
    )(h2, x1, modp, *const_ops)


def _moe_kernel(be_ref, nb_ref, x_ref, w1_ref, w3_ref, w2_ref, o_ref):
    i = pl.program_id(0)

    @pl.when(i < nb_ref[0])
    def _():
        o_ref[...] = _swiglu(x_ref[...], w1_ref[0], w3_ref[0], w2_ref[0])

    @pl.when(i >= nb_ref[0])
    def _():
        o_ref[...] = jnp.zeros(o_ref.shape, F32)


def _moe_experts(block_e, n_used, xb, w1, w3, w2, lm):
    p, d = xb.shape
    dff = w1.shape[3]
    nblk = p // MOE_ROWS
    one = pl.Buffered(1)
    expert = lambda i, be, nb: (lm, be[i], 0, 0)
    grid_spec = pltpu.PrefetchScalarGridSpec(
        num_scalar_prefetch=2,
        grid=(nblk,),
        in_specs=[pl.BlockSpec((MOE_ROWS, d), lambda i, be, nb: (i, 0)),
                  pl.BlockSpec((None, 1, d, dff), expert, pipeline_mode=one),
                  pl.BlockSpec((None, 1, d, dff), expert, pipeline_mode=one),
                  pl.BlockSpec((None, 1, dff, d), expert, pipeline_mode=one)],
        out_specs=pl.BlockSpec((MOE_ROWS, d), lambda i, be, nb: (i, 0)),
    )
    return pl.pallas_call(
        _moe_kernel,
        grid_spec=grid_spec,
        out_shape=jax.ShapeDtypeStruct((p, d), F32),
        name="moe_experts",
        compiler_params=_params("arbitrary"),
    )(block_e, n_used, xb, w1, w3, w2)


def _combine_kernel(ya_ref, yb_ref, route_ref, x1_ref, mod_ref, lng_ref, lnb_ref, o_ref, *, alpha):
    f = route_ref[:, 2:3] * ya_ref[...] + route_ref[:, 3:4] * yb_ref[...]
    g2 = mod_ref[0, 5:6, :]
    o_ref[...] = _layer_norm(alpha * x1_ref[...] + g2 * f, lng_ref[...], lnb_ref[...])


def _moe_combine(ya, yb, route, x1, modp, ln_g, ln_b, lay, alpha):
    n, d = x1.shape
    row = lambda i: (i, 0)
    return pl.pallas_call(
        functools.partial(_combine_kernel, alpha=alpha),
        grid=(n // ROW_BLOCK,),
        in_specs=[pl.BlockSpec((ROW_BLOCK, d), row), pl.BlockSpec((ROW_BLOCK, d), row),
                  pl.BlockSpec((ROW_BLOCK, V7X_LANES), row), pl.BlockSpec((ROW_BLOCK, d), row),
                  lay.mod_spec(d),
                  _const_spec(ln_g.shape), _const_spec(ln_b.shape)],
        out_specs=pl.BlockSpec((ROW_BLOCK, d), row),
        out_shape=jax.ShapeDtypeStruct((n, d), F32),
        name="moe_combine",
        compiler_params=_params("parallel"),
    )(ya, yb, route, x1, modp, ln_g, ln_b)


def _moe(h2, route, x1, modp, w1, w3, w2, lm, ln_g, ln_b, lay, alpha):
    n, d = x1.shape
    n_exp = w1.shape[1]
    top_e = route[:, 0:2].astype(jnp.int32)
    flat_e = top_e.reshape(-1)
    onehot = (flat_e[:, None] == jnp.arange(n_exp, dtype=jnp.int32)[None, :]).astype(jnp.int32)
    rank = jnp.sum((jnp.cumsum(onehot, axis=0) - onehot) * onehot, axis=1)
    counts = jnp.sum(onehot, axis=0)
    padded = (counts + MOE_ROWS - 1) // MOE_ROWS * MOE_ROWS
    pad_end = jnp.cumsum(padded)
    pad_start = pad_end - padded
    dest = pad_start[flat_e] + rank
    nblk = -(-(2 * n + n_exp * (MOE_ROWS - 1)) // MOE_ROWS)
    p = nblk * MOE_ROWS
    flat_tok = jnp.repeat(jnp.arange(n, dtype=jnp.int32), 2)
    buf_tok = jnp.full((p,), n, jnp.int32).at[dest].set(flat_tok)
    blk_start = jnp.arange(nblk, dtype=jnp.int32) * MOE_ROWS
    block_e = jnp.minimum(jnp.sum((pad_end[None, :] <= blk_start[:, None]).astype(jnp.int32), axis=1), n_exp - 1)
    n_used = (pad_end[-1] // MOE_ROWS).astype(jnp.int32).reshape(1)
    xb = jnp.concatenate([h2, jnp.zeros((1, d), h2.dtype)], axis=0)[buf_tok]
    yb = _moe_experts(block_e, n_used, xb, w1, w3, w2, lm)
    dest2 = dest.reshape(n, 2)
    return _moe_combine(yb[dest2[:, 0]], yb[dest2[:, 1]], route, x1, modp, ln_g, ln_b, lay, alpha)


def _layer(x, modp, lw, l, lay, alpha, z0, ctx_kv, new_skv):
    (bc, tc_len), (bs, ts_len) = lay.passes
    n_ctx = bc * tc_len
    slab, qn, kn, vn, k_new, v_new, gates = _in_proj(x, modp, lw["w_in"], l, lay, lw["n_slab"], *new_skv[1:])
    r, kk, lwf, lwb, bf, bb, ktf, ktb, v, bonus, g = _rwkv_prep(slab, lw, lay)
    yf, yb, s_new = _rwkv_scan(z0, r, kk, lwf, lwb, bf, bb, ktf, ktb, v, lay, new_skv[0])
    yn_ctx = _ctx_attention(qn, kn, vn, bc, tc_len)
    yn_lat = _nbr_attention(qn, kn, vn, ctx_kv[0], ctx_kv[1], lw["bias"], l, bs, ts_len, n_ctx)
    outs = _mix_out(yf, yb, bonus, g, yn_ctx, yn_lat, gates, x, modp, lw, lay, alpha, lw.get("router"))
    if "router" in lw:
        x1, h2, route = outs
        x2 = _moe(h2, route, x1, modp, lw["moe_w1"], lw["moe_w3"], lw["moe_w2"], l // 2, lw["ln2_g"], lw["ln2_b"],
                  lay, alpha)
    else:
        x1, h2 = outs
        x2 = _ffn_dense(h2, x1, modp, (lw["ffn_w1"], l // 2), (lw["ffn_w3"], l // 2), (lw["ffn_w2"], l // 2),
                        lw["ln2_g"], lw["ln2_b"], lay, alpha)
    return x2, s_new, k_new, v_new


def kernel(x_prompt, x_sample, c, state_rwkv, cache_k, cache_v, c_ctx, w_ada, b_ada, w_in, ts_w, rw_w0, rw_w2, rw_a0, rw_a2, rw_g2, rw_kk, rw_ka, rw_u, rw_gn_g, rw_gn_b, na_rpb, w_up_r, w_up_n, w_o, ln1_g, ln1_b, ln2_g, ln2_b, ffn_w1, ffn_w3, ffn_w2, moe_router, moe_w1, moe_w3, moe_w2):
    depth = w_in.shape[0]
    bc, tc_len, d = x_prompt.shape
    bs, ts_len, _ = x_sample.shape
    dr = D_BRANCH
    alpha = float((2 * depth) ** 0.25)
    n_slab = ts_w.shape[2]
    lora_w = rw_w2.shape[2]
    lora_a = rw_a2.shape[2]

    cvec = jnp.zeros((8, d), F32).at[0].set(c_ctx).at[1:1 + bs].set(c)
    mod = _ada_mod(cvec, w_ada, b_ada)

    half = np.kron(np.eye(2, dtype=np.float32), np.ones((HEAD_DIM, HEAD_DIM), np.float32))
    bd = jnp.asarray(half, BF16)
    rows = ts_len // GRID_W

    passes = ((bc, tc_len), (bs, ts_len))
    n_ctx = bc * tc_len
    bias_tables = _nbr_bias_tables(na_rpb)
    x = jnp.concatenate([x_prompt.reshape(n_ctx, d), x_sample.reshape(bs * ts_len, d)], axis=0)
    seq_row = np.concatenate([np.zeros(bc, np.int32), 1 + np.arange(bs, dtype=np.int32)])
    modp = mod[:, seq_row].reshape(depth, bc + bs, 6, d)
    z0_all = jnp.concatenate([jnp.zeros((depth, bc, 2, N_HEADS, HEAD_DIM, HEAD_DIM), F32),
                              jnp.swapaxes(jnp.swapaxes(state_rwkv, 0, 1), -1, -2)], axis=1)
    w_in_b, w_up_r_b, w_up_n_b, w_o_b = (w.astype(BF16) for w in (w_in, w_up_r, w_up_n, w_o))
    ffn_b = [w.astype(BF16) for w in (ffn_w1, ffn_w3, ffn_w2)]
    moe_b = [w.astype(BF16) for w in (moe_w1, moe_w3, moe_w2)]
    new_skv = [jnp.zeros((bc, depth, 2, N_HEADS, HEAD_DIM, HEAD_DIM), F32),
               jnp.zeros((bc, depth, dr, tc_len), F32), jnp.zeros((bc, depth, dr, tc_len), F32)]
    for l in range(depth):
        w2blk = jnp.zeros((2 * lora_w, 2 * dr), F32).at[:lora_w, :dr].set(rw_w2[l, 0]).at[lora_w:, dr:].set(rw_w2[l, 1])
        a2blk = jnp.zeros((2 * lora_a, 2 * dr), F32).at[:lora_a, :dr].set(rw_a2[l, 0]).at[lora_a:, dr:].set(rw_a2[l, 1])
        lw = dict(
            n_slab=n_slab,
            w_in=w_in_b, ts=ts_w[l],
            w0=rw_w0[l].reshape(1, 2 * dr), w2=w2blk.astype(BF16),
            a0=rw_a0[l].reshape(1, 2 * dr), a2=a2blk.astype(BF16),
            g2=rw_g2[l].astype(BF16), kkw=rw_kk[l].reshape(1, dr), ka=rw_ka[l].reshape(1, dr),
            u=rw_u[l].reshape(1, dr), bd=bd,
            gn_g=rw_gn_g[l].reshape(1, dr), gn_b=rw_gn_b[l].reshape(1, dr),
            w_up_r=(w_up_r_b, l), w_up_n=(w_up_n_b, l), w_o=(w_o_b, l),
            ln1_g=ln1_g[l].reshape(1, d), ln1_b=ln1_b[l].reshape(1, d),
            ln2_g=ln2_g[l].reshape(1, d), ln2_b=ln2_b[l].reshape(1, d),
            bias=bias_tables,
        )
        if l % 2 == 0:
            lw.update(ffn_w1=ffn_b[0], ffn_w3=ffn_b[1], ffn_w2=ffn_b[2])
        else:
            lw.update(router=moe_router[l // 2], moe_w1=moe_b[0], moe_w3=moe_b[1], moe_w2=moe_b[2])

        kv = (cache_k[:, l].reshape(bs, -1, dr), cache_v[:, l].reshape(bs, -1, dr))
        x, *new_skv = _layer(x, modp, lw, l, _Layout(passes, l), alpha, z0_all, kv, new_skv)

    cache_out = lambda a: jnp.transpose(a.reshape(bc, depth, N_HEADS, HEAD_DIM, tc_len), (0, 1, 4, 2, 3))
    return (x[:n_ctx].reshape(bc, tc_len, d), x[n_ctx:].reshape(bs, ts_len, d),
            new_skv[0], cache_out(new_skv[1]), cache_out(new_skv[2]))
```

```python
import functools

import numpy as np
import jax
import jax.numpy as jnp
from jax import lax
from jax.experimental import pallas as pl
from jax.experimental.pallas import tpu as pltpu

F32 = jnp.float32
BF16 = jnp.bfloat16
HIGHEST = lax.Precision.HIGHEST

N_HEADS = 8
HEAD_DIM = 64
D_BRANCH = N_HEADS * HEAD_DIM
V7X_LANES = 128
ROW_BLOCK = 256
SCAN_CHUNK = 64
SCAN_CHUNKS_PER_STEP = 2
MOE_ROWS = 256
KH_MAX = 8
KW = 16
GRID_W = 64
LN_EPS = 1e-5
GN_EPS = 64e-5
NEG_BIG = -1e30
VMEM_LIMIT = 56 * 1024 * 1024


def _params(*sem):
    return pltpu.CompilerParams(dimension_semantics=sem, vmem_limit_bytes=VMEM_LIMIT)


def _const_spec(shape):
    nd = len(shape)
    return pl.BlockSpec(shape, lambda *_: (0,) * nd, pipeline_mode=pl.Buffered(1))


def _layer_spec(stack, l):
    nd = stack.ndim - 1
    return pl.BlockSpec((None,) + stack.shape[1:], lambda *_: (l,) + (0,) * nd, pipeline_mode=pl.Buffered(1))


def _param_specs(params):
    specs = [_layer_spec(*p) if isinstance(p, tuple) else _const_spec(p.shape) for p in params]
    return specs, [p[0] if isinstance(p, tuple) else p for p in params]


class _Layout:
    def __init__(self, passes, layer=0):
        self.passes = passes
        self.layer = layer
        self.n_seq = sum(b for b, _ in passes)

    def _per_block(self, fn):
        def at(i):
            out, blk0, seq0 = None, 0, 0
            for b, t in self.passes:
                per = t // ROW_BLOCK
                val = fn(i - blk0, per, seq0)
                out = val if out is None else jnp.where(i >= blk0, val, out)
                blk0 += b * per
                seq0 += b
            return out
        return at

    def seq_of_block(self, i):
        return self._per_block(lambda j, per, s0: s0 + j // per)(i)

    def is_first(self, i):
        return self._per_block(lambda j, per, s0: (j % per) == 0)(i)

    def is_last(self, i):
        return self._per_block(lambda j, per, s0: (j % per) == per - 1)(i)

    def mod_spec(self, d):
        return pl.BlockSpec((None, 1, 6, d), lambda i: (self.layer, self.seq_of_block(i), 0, 0))


def _dot(a, b):
    return jnp.dot(a, b, preferred_element_type=F32)


def _dot_nt(a, b):
    return lax.dot_general(a, b, (((1,), (1,)), ((), ())), preferred_element_type=F32)


def _dot_tn(a, b):
    return lax.dot_general(a, b, (((0,), (0,)), ((), ())), preferred_element_type=F32)


def _mm1(a, b, f=_dot):
    return f(a.astype(BF16), b.astype(BF16))


def _mm_mask_lhs(m, x):
    hi = x.astype(BF16)
    lo = (x - hi.astype(F32)).astype(BF16)
    return _dot(m, hi) + _dot(m, lo)


def _sigmoid(x):
    return 1.0 / (1.0 + jnp.exp(-x))


def _head_sum(x, bd):
    outs = []
    for j in range(x.shape[1] // V7X_LANES):
        xs = x[:, j * V7X_LANES:(j + 1) * V7X_LANES]
        hi = xs.astype(BF16)
        lo = (xs - hi.astype(F32)).astype(BF16)
        outs.append(_dot(hi, bd) + _dot(lo, bd))
    return jnp.concatenate(outs, axis=1)


def _layer_norm(z, g, b):
    mu = jnp.mean(z, axis=-1, keepdims=True)
    zc = z - mu
    var = jnp.mean(zc * zc, axis=-1, keepdims=True)
    return zc * lax.rsqrt(var + LN_EPS) * g + b


def _ada_kernel(c_ref, w_ref, b_ref, o_ref):
    cs = c_ref[...]
    s = cs * _sigmoid(cs)
    o_ref[0] = jnp.dot(s, w_ref[0], preferred_element_type=F32, precision=HIGHEST) + b_ref[0]


def _ada_mod(cvec, w_ada, b_ada):
    depth, d, n6 = w_ada.shape
    tn = n6 // 4
    return pl.pallas_call(
        _ada_kernel,
        grid=(depth, n6 // tn),
        in_specs=[pl.BlockSpec((8, d), lambda l, j: (0, 0)),
                  pl.BlockSpec((1, d, tn), lambda l, j: (l, 0, j)),
                  pl.BlockSpec((1, 1, tn), lambda l, j: (l, 0, j))],
        out_specs=pl.BlockSpec((1, 8, tn), lambda l, j: (l, 0, j)),
        out_shape=jax.ShapeDtypeStruct((depth, 8, n6), F32),
        name="ada_mod",
        compiler_params=_params("parallel", "parallel"),
    )(cvec, w_ada, b_ada.reshape(depth, 1, n6))


def _inproj_kernel(x_ref, mod_ref, w_ref, kbuf_ref, vbuf_ref, slab_ref, q_ref, k_ref, v_ref, kc_ref, vc_ref, gate_ref,
                   *, n_slab, nb_cache):
    del kbuf_ref, vbuf_ref
    sh = mod_ref[0, 0:1, :]
    sc = mod_ref[0, 1:2, :]
    h = (x_ref[...] * (1.0 + sc) + sh).astype(BF16)
    dn = D_BRANCH
    slab_ref[...] = _dot(h, w_ref[:, 0:n_slab])
    q_ref[...] = _dot(h, w_ref[:, n_slab:n_slab + dn]).astype(q_ref.dtype)
    k = _dot(h, w_ref[:, n_slab + dn:n_slab + 2 * dn])
    v = _dot(h, w_ref[:, n_slab + 2 * dn:n_slab + 3 * dn])
    k_ref[...] = k.astype(k_ref.dtype)
    v_ref[...] = v.astype(v_ref.dtype)

    @pl.when(pl.program_id(0) < nb_cache)
    def _():
        kc_ref[0] = k.T
        vc_ref[0] = v.T

    gate_ref[...] = _sigmoid(_dot(h, w_ref[:, n_slab + 3 * dn:]))


def _in_proj(x, modp, w_in, l, lay, n_slab, cache_k, cache_v):
    n, d = x.shape
    n_cols = w_in.shape[2]
    dn = D_BRANCH
    n_gate = n_cols - n_slab - 3 * dn
    b_cache, t_cache = lay.passes[0]
    per_seq = t_cache // ROW_BLOCK
    nb_cache = b_cache * per_seq
    row = lambda i: (i, 0)

    def cache_block(i):
        ic = jnp.minimum(i, nb_cache - 1)
        return ic // per_seq, l, 0, ic % per_seq

    cache_spec = pl.BlockSpec((1, None, dn, ROW_BLOCK), cache_block)
    any_spec = pl.BlockSpec(memory_space=pl.ANY)
    return pl.pallas_call(
        functools.partial(_inproj_kernel, n_slab=n_slab, nb_cache=nb_cache),
        grid=(n // ROW_BLOCK,),
        in_specs=[pl.BlockSpec((ROW_BLOCK, d), row),
                  lay.mod_spec(d),
                  _layer_spec(w_in, l),
                  any_spec, any_spec],
        out_specs=[pl.BlockSpec((ROW_BLOCK, n_slab), row),
                   pl.BlockSpec((ROW_BLOCK, dn), row), pl.BlockSpec((ROW_BLOCK, dn), row),
                   pl.BlockSpec((ROW_BLOCK, dn), row),
                   cache_spec, cache_spec,
                   pl.BlockSpec((ROW_BLOCK, n_gate), row)],
        out_shape=[jax.ShapeDtypeStruct((n, n_slab), F32),
                   jax.ShapeDtypeStruct((n, dn), BF16), jax.ShapeDtypeStruct((n, dn), BF16),
                   jax.ShapeDtypeStruct((n, dn), BF16),
                   jax.ShapeDtypeStruct(cache_k.shape, F32), jax.ShapeDtypeStruct(cache_v.shape, F32),
                   jax.ShapeDtypeStruct((n, n_gate), F32)],
        input_output_aliases={3: 4, 4: 5},
        name="in_proj",
        compiler_params=_params("arbitrary"),
    )(x, modp, w_in, cache_k, cache_v)


def _prep_kernel(cur_ref, prev_ref, next_ref, ts_ref, w0_ref, w2_ref, a0_ref, a2_ref, g2_ref,
                 kkw_ref, ka_ref, u_ref, bd_ref,
                 r_o, kk_o, lwf_o, lwb_o, bf_o, bb_o, ktf_o, ktb_o, v_o, bonus_o, g_o,
                 *, lay, lora_w, lora_a):
    i = pl.program_id(0)
    tb, n_slab = cur_ref.shape
    dr = D_BRANCH
    cur = cur_ref[...]
    first = lay.is_first(i)
    last = lay.is_last(i)
    prow = jnp.where(first, 0.0, prev_ref[7:8, :])
    nrow = jnp.where(last, 0.0, next_ref[0:1, :])
    rid = lax.broadcasted_iota(jnp.int32, (tb, n_slab), 0)
    xp = jnp.where(rid == 0, prow, pltpu.roll(cur, 1, 0))
    xn = jnp.where(rid == tb - 1, nrow, pltpu.roll(cur, tb - 1, 0))
    s = ts_ref[0:1, :] * xp + ts_ref[1:2, :] * cur + ts_ref[2:3, :] * xn

    r = s[:, 0:dr]
    k = s[:, dr:2 * dr]
    v = s[:, 2 * dr:3 * dr]
    o = 3 * dr
    wd = s[:, o:o + 2 * lora_w]
    ad = s[:, o + 2 * lora_w:o + 2 * lora_w + 2 * lora_a]
    gd = s[:, o + 2 * lora_w + 2 * lora_a:]

    wl = w0_ref[...] + _dot(jnp.tanh(wd).astype(BF16), w2_ref[...])
    logw = -float(np.exp(-0.5)) * _sigmoid(wl)
    a = _sigmoid(a0_ref[...] + _dot(ad.astype(BF16), a2_ref[...]))
    g = _dot(_sigmoid(gd).astype(BF16), g2_ref[...])

    bd = bd_ref[...]
    kk = k * kkw_ref[...]
    kk = kk * lax.rsqrt(_head_sum(kk * kk, bd) + 1e-12)
    ka = ka_ref[...]
    a_f = a[:, :dr]
    a_b = a[:, dr:]
    kt_f = k * (1.0 + (a_f - 1.0) * ka)
    kt_b = k * (1.0 + (a_b - 1.0) * ka)
    bonus = _head_sum(r * (0.5 * (kt_f + kt_b)) * u_ref[...], bd) * v

    r_o[...] = r
    kk_o[...] = kk
    lwf_o[...] = logw[:, :dr]
    lwb_o[...] = logw[:, dr:]
    bf_o[...] = kk * a_f
    bb_o[...] = kk * a_b
    ktf_o[...] = kt_f
    ktb_o[...] = kt_b
    v_o[...] = v
    bonus_o[...] = bonus
    g_o[...] = g


def _rwkv_prep(slab, lw, lay):
    n, n_slab = slab.shape
    dr = D_BRANCH
    sub = ROW_BLOCK // 8
    lora_w = lw["w2"].shape[0] // 2
    lora_a = lw["a2"].shape[0] // 2
    row = lambda i: (i, 0)
    consts = [lw["ts"], lw["w0"], lw["w2"], lw["a0"], lw["a2"], lw["g2"], lw["kkw"], lw["ka"], lw["u"], lw["bd"]]
    outs = pl.pallas_call(
        functools.partial(_prep_kernel, lay=lay, lora_w=lora_w, lora_a=lora_a),
        grid=(n // ROW_BLOCK,),
        in_specs=[pl.BlockSpec((ROW_BLOCK, n_slab), row),
                  pl.BlockSpec((8, n_slab), lambda i: (jnp.maximum(i * sub - 1, 0), 0)),
                  pl.BlockSpec((8, n_slab), lambda i: (jnp.minimum((i + 1) * sub, n // 8 - 1), 0))]
                 + [_const_spec(c.shape) for c in consts],
        out_specs=[pl.BlockSpec((ROW_BLOCK, dr), row)] * 11,
        out_shape=[jax.ShapeDtypeStruct((n, dr), F32)] * 11,
        name="rwkv_prep",
        compiler_params=_params("parallel"),
    )(slab, slab, slab, *consts)
    return outs


def _tri_masks(rev):
    c = SCAN_CHUNK
    ti = lax.broadcasted_iota(jnp.int32, (c, 2 * c), 0)
    ii = lax.broadcasted_iota(jnp.int32, (c, 2 * c), 1) % c
    if rev:
        ti, ii = c - 1 - ti, c - 1 - ii
    one = lambda m: jnp.where(m, 1.0, 0.0)
    levels = []
    s = 1
    while s < c:
        same = (ti // (2 * s)) == (ii // (2 * s))
        levels.append(one(same & ((ti % (2 * s)) >= s) & ((ii % (2 * s)) < s)))
        s *= 2
    return one(ii <= ti), one(ii < ti), levels


def _scan_kernel(fwd_tab, bwd_tab, seq_tab, first_tab, last_tab,
                 z0_ref, rf, kkf, lwf, bf, ktf, vf, rb, kkb, lwb, bb, ktb, vb, sbuf_ref, yf_o, yb_o, sfin_o, z_scr,
                 *, n_state_out):
    del sbuf_ref
    g_step = pl.program_id(0)
    c = SCAN_CHUNK
    lanes = V7X_LANES
    npair = N_HEADS // 2
    assert c == HEAD_DIM and 2 * HEAD_DIM == lanes

    row_blk = lax.broadcasted_iota(jnp.int32, (lanes, lanes), 0) // HEAD_DIM
    col_blk = lax.broadcasted_iota(jnp.int32, (lanes, lanes), 1) // HEAD_DIM
    bd_mask = jnp.where(row_blk == col_blk, 1.0, 0.0)
    eye_full = jnp.where(lax.broadcasted_iota(jnp.int32, (lanes, lanes), 0)
                         == lax.broadcasted_iota(jnp.int32, (lanes, lanes), 1), 1.0, 0.0)
    lane_c = lax.broadcasted_iota(jnp.int32, (c, lanes), 1)
    lo = jnp.where(lane_c < HEAD_DIM, 1.0, 0.0)
    hi = 1.0 - lo
    eye_pair = jnp.where(lax.broadcasted_iota(jnp.int32, (c, lanes), 0) == lane_c % HEAD_DIM, 1.0, 0.0)

    def bd(x):
        return jnp.concatenate([x * lo, x * hi], axis=0)

    @pl.when(first_tab[g_step] == 1)
    def _():
        for d in range(2):
            for p in range(npair):
                za = jnp.concatenate([z0_ref[0, d, 2 * p], z0_ref[0, d, 2 * p + 1]], axis=1)
                z_scr[d * npair + p] = bd(za)

    refs = ((rf, kkf, lwf, bf, ktf, vf, yf_o), (rb, kkb, lwb, bb, ktb, vb, yb_o))
    masks = (_tri_masks(False), _tri_masks(True))
    cum_lhs = [m[0][:, 0:c].astype(BF16) for m in masks]
    last_row = (c - 1, 0)
    nck = rf.shape[0] // c
    units = [(d, p, ck) for d in range(2) for p in range(npair) for ck in range(nck)]

    def tile(d, p, ck):
        row0 = (nck - 1 - ck if d else ck) * c
        return slice(row0, row0 + c), slice(p * lanes, (p + 1) * lanes)

    def per(fn):
        return [fn(i, d, tile(d, p, ck)) for i, (d, p, ck) in enumerate(units)]

    lw = per(lambda i, d, sl: refs[d][2][sl])
    cum = per(lambda i, d, sl: _mm_mask_lhs(cum_lhs[d], lw[i]))
    tot = per(lambda i, d, sl: jnp.broadcast_to(cum[i][last_row[d]:last_row[d] + 1], (c, lanes)))
    r_s = per(lambda i, d, sl: refs[d][0][sl] * jnp.exp(cum[i]))
    k_s = per(lambda i, d, sl: refs[d][1][sl] * jnp.exp(cum[i] - lw[i]))
    e_neg = per(lambda i, d, sl: jnp.exp(-cum[i]))
    e_end = per(lambda i, d, sl: jnp.exp(tot[i] - cum[i]))
    b = per(lambda i, d, sl: refs[d][3][sl])
    kt = per(lambda i, d, sl: refs[d][4][sl])
    v = per(lambda i, d, sl: refs[d][5][sl])
    v_bd = per(lambda i, d, sl: bd(v[i]))
    kr = per(lambda i, d, sl: jnp.concatenate([k_s[i], r_s[i]], axis=0))
    ab = per(lambda i, d, sl: _mm1(kr[i], bd(b[i] * e_neg[i]), _dot_nt))
    ak = per(lambda i, d, sl: _mm1(kr[i], bd(kt[i] * e_neg[i]), _dot_nt))
    a_b = per(lambda i, d, sl: ab[i][0:c] * masks[d][1])
    b_b = per(lambda i, d, sl: ab[i][c:] * masks[d][0])
    a_kt = per(lambda i, d, sl: ak[i][0:c] * masks[d][1])
    b_kt = per(lambda i, d, sl: ak[i][c:] * masks[d][0])
    dm = per(lambda i, d, sl: eye_pair - a_b[i] * masks[d][2][0])
    for lvl in range(1, len(masks[0][2])):
        zd = per(lambda i, d, sl: _mm1(a_b[i] * masks[d][2][lvl], bd(dm[i])))
        dm = per(lambda i, d, sl: dm[i] - _mm1(dm[i], bd(zd[i])))
    akv = per(lambda i, d, sl: _mm1(jnp.concatenate([a_kt[i], b_kt[i]], axis=0), v_bd[i]))
    w1 = [x[0:c] for x in akv]
    g = per(lambda i, d, sl: _mm1(dm[i], jnp.concatenate([bd(k_s[i]), bd(w1[i])], axis=1)))
    bg = per(lambda i, d, sl: _mm1(b_b[i], jnp.concatenate([bd(g[i][:, 0:lanes]), bd(g[i][:, lanes:])], axis=1)))
    q = per(lambda i, d, sl: r_s[i] - bg[i][:, 0:lanes])
    y0 = per(lambda i, d, sl: akv[i][c:] - bg[i][:, lanes:])
    cg = per(lambda i, d, sl: _mm1(b[i] * e_end[i], g[i], _dot_tn))
    phi = per(lambda i, d, sl: eye_full * jnp.exp(jnp.concatenate([tot[i], tot[i]], axis=0))
              - cg[i][:, 0:lanes] * bd_mask)
    psi = per(lambda i, d, sl: (_mm1(kt[i] * e_end[i], v[i], _dot_tn) - cg[i][:, lanes:]) * bd_mask)
    z = [z_scr[s] for s in range(2 * npair)]
    for ck in range(nck):
        sel = [(i, d, p) for i, (d, p, k2) in enumerate(units) if k2 == ck]
        qz = [_mm1(jnp.concatenate([q[i], phi[i]], axis=0), z[d * npair + p]) for i, d, p in sel]
        y = [x[0:c] + y0[i] for x, (i, d, p) in zip(qz, sel)]
        z = [x[c:] + psi[i] for x, (i, d, p) in zip(qz, sel)]
        for yy, (i, d, p) in zip(y, sel):
            refs[d][6][tile(d, p, ck)] = yy
    for s in range(2 * npair):
        z_scr[s] = z[s]

    @pl.when(jnp.logical_and(last_tab[g_step] == 1, seq_tab[g_step] < n_state_out))
    def _():
        for d in range(2):
            for p in range(npair):
                zt = z_scr[d * npair + p].T
                sfin_o[0, d, 2 * p] = zt[0:HEAD_DIM, 0:HEAD_DIM]
                sfin_o[0, d, 2 * p + 1] = zt[HEAD_DIM:, HEAD_DIM:]


def _scan_tables(lay):
    blk = SCAN_CHUNK * SCAN_CHUNKS_PER_STEP
    rows, blk0, seq0 = [], 0, 0
    for b, t in lay.passes:
        nj = t // blk
        for s in range(b):
            for j in range(nj):
                rows.append((blk0 + s * nj + j, blk0 + s * nj + nj - 1 - j, seq0 + s, int(j == 0), int(j == nj - 1)))
        blk0 += b * nj
        seq0 += b
    return [jnp.asarray(col, jnp.int32) for col in zip(*rows)]


def _rwkv_scan(z0, r, kk, lwf, lwb, bf, bb, ktf, ktb, v, lay, new_state):
    blk = SCAN_CHUNK * SCAN_CHUNKS_PER_STEP
    n, dr = r.shape
    n_out = new_state.shape[0]
    assert n_out == lay.passes[0][0]
    tabs = _scan_tables(lay)
    fwd = pl.BlockSpec((blk, dr), lambda g, fw, bw, sq, fi, la: (fw[g], 0))
    bwd = pl.BlockSpec((blk, dr), lambda g, fw, bw, sq, fi, la: (bw[g], 0))
    st_out = pl.BlockSpec((1, None, 2, N_HEADS, HEAD_DIM, HEAD_DIM),
                          lambda g, fw, bw, sq, fi, la: (jnp.minimum(sq[g], n_out - 1), lay.layer, 0, 0, 0, 0))
    st_in = pl.BlockSpec((None, 1, 2, N_HEADS, HEAD_DIM, HEAD_DIM),
                         lambda g, fw, bw, sq, fi, la: (lay.layer, sq[g], 0, 0, 0, 0))
    grid_spec = pltpu.PrefetchScalarGridSpec(
        num_scalar_prefetch=len(tabs),
        grid=(int(tabs[0].shape[0]),),
        in_specs=[st_in] + [fwd] * 6 + [bwd] * 6 + [pl.BlockSpec(memory_space=pl.ANY)],
        out_specs=[fwd, bwd, st_out],
        scratch_shapes=[pltpu.VMEM((N_HEADS, V7X_LANES, V7X_LANES), F32)],
    )
    n_in = len(tabs) + 13
    return pl.pallas_call(
        functools.partial(_scan_kernel, n_state_out=n_out),
        grid_spec=grid_spec,
        out_shape=[jax.ShapeDtypeStruct((n, dr), F32),
                   jax.ShapeDtypeStruct((n, dr), F32),
                   jax.ShapeDtypeStruct(new_state.shape, F32)],
        input_output_aliases={n_in: 2},
        name="rwkv_scan",
        compiler_params=_params("arbitrary"),
    )(*tabs, z0, r, kk, lwf, bf, ktf, v, r, kk, lwb, bb, ktb, v, new_state)


def _softmax_rows(parts):
    m = parts[0].max(axis=-1, keepdims=True)
    for p in parts[1:]:
        m = jnp.maximum(m, p.max(axis=-1, keepdims=True))
    es = [jnp.exp(p - m) for p in parts]
    den = es[0].sum(axis=-1, keepdims=True)
    for e in es[1:]:
        den = den + e.sum(axis=-1, keepdims=True)
    inv = 1.0 / den
    return [e * inv for e in es]


def _ctx_attn_kernel(q_ref, k_ref, v_ref, o_ref):
    scale = HEAD_DIM ** -0.5
    npair = q_ref.shape[1] // V7X_LANES
    tile = lambda ref, p: ref[:, p * V7X_LANES:(p + 1) * V7X_LANES].astype(BF16)
    qb = [tile(q_ref, p) for p in range(npair)]
    k = [tile(k_ref, p) for p in range(npair)]
    v = [tile(v_ref, p) for p in range(npair)]
    first_head = lax.broadcasted_iota(jnp.int32, qb[0].shape, 1) < HEAD_DIM
    head_lanes = (first_head, jnp.logical_not(first_head))
    units = [(p, hh) for p in range(npair) for hh in range(2)]

    def per(fn):
        return [fn(u, p, hh) for u, (p, hh) in enumerate(units)]

    q = per(lambda u, p, hh: jnp.where(head_lanes[hh], qb[p], jnp.zeros_like(qb[p])))
    s = per(lambda u, p, hh: _dot_nt(q[u], k[p]) * scale)
    e = per(lambda u, p, hh: jnp.exp(s[u] - s[u].max(axis=-1, keepdims=True)))
    pr = per(lambda u, p, hh: (e[u] * (1.0 / e[u].sum(axis=-1, keepdims=True))).astype(BF16))
    out = per(lambda u, p, hh: _dot(pr[u], v[p]))
    for p in range(npair):
        o_ref[:, p * V7X_LANES:(p + 1) * V7X_LANES] = jnp.where(first_head, out[2 * p], out[2 * p + 1]).astype(o_ref.dtype)


def _ctx_attention(q, k, v, b_len, t_len):
    n = b_len * t_len
    seq = pl.BlockSpec((t_len, D_BRANCH), lambda b: (b, 0))
    return pl.pallas_call(
        _ctx_attn_kernel,
        grid=(b_len,),
        in_specs=[seq, seq, seq],
        out_specs=seq,
        out_shape=jax.ShapeDtypeStruct((n, D_BRANCH), BF16),
        name="ctx_attn",
        compiler_params=_params("parallel"),
    )(q, k, v)


def _nbr_attn_kernel(q_ref, k_ref, v_ref, kc_ref, vc_ref, bias_ref, o_ref, *, rows, kh):
    scale = HEAD_DIM ** -0.5
    win = kh * GRID_W

    group = 8 if rows % 8 == 0 else 1
    units = [(r, hh) for r in range(group) for hh in range(2)]
    first_head = lax.broadcasted_iota(jnp.int32, (GRID_W, V7X_LANES), 1) < HEAD_DIM
    head_lanes = (first_head, jnp.logical_not(first_head))

    def body(gi, carry):
        def per(fn):
            return [fn(u, r, hh) for u, (r, hh) in enumerate(units)]

        i = [gi * group + r for r in range(group)]
        rs = [jnp.clip(x - kh // 2, 0, rows - kh) for x in i]
        q0 = [pl.multiple_of(x * GRID_W, GRID_W) for x in i]
        k0 = [pl.multiple_of(x * GRID_W, GRID_W) for x in rs]
        qb = [q_ref[pl.ds(x, GRID_W), :].astype(BF16) for x in q0]
        kw = [k_ref[pl.ds(x, win), :].astype(BF16) for x in k0]
        vw = [v_ref[pl.ds(x, win), :].astype(BF16) for x in k0]
        kc = kc_ref[0].astype(BF16)
        vc = vc_ref[0].astype(BF16)
        q = per(lambda u, r, hh: jnp.where(head_lanes[hh], qb[r], jnp.zeros_like(qb[r])))
        off0 = [(KH_MAX - 1) - (i[r] - rs[r]) for r in range(group)]
        bias = per(lambda u, r, hh: jnp.concatenate([bias_ref[hh, off0[r] + 2 * a2] for a2 in range(kh // 2)], axis=1))
        s_loc = per(lambda u, r, hh: _dot_nt(q[u], kw[r]) * scale + bias[u])
        s_ctx = per(lambda u, r, hh: _dot_nt(q[u], kc) * scale)
        m = per(lambda u, r, hh: jnp.maximum(s_loc[u].max(axis=-1, keepdims=True), s_ctx[u].max(axis=-1, keepdims=True)))
        e_loc = per(lambda u, r, hh: jnp.exp(s_loc[u] - m[u]))
        e_ctx = per(lambda u, r, hh: jnp.exp(s_ctx[u] - m[u]))
        inv = per(lambda u, r, hh: 1.0 / (e_loc[u].sum(axis=-1, keepdims=True) + e_ctx[u].sum(axis=-1, keepdims=True)))
        out = per(lambda u, r, hh: _dot((e_loc[u] * inv[u]).astype(BF16), vw[r])
                  + _dot((e_ctx[u] * inv[u]).astype(BF16), vc))
        for r in range(group):
            o_ref[pl.ds(q0[r], GRID_W), :] = jnp.where(first_head, out[2 * r], out[2 * r + 1]).astype(o_ref.dtype)
        return carry

    lax.fori_loop(0, rows // group, body, 0)


def _nbr_attention(q, k, v, k_ctx, v_ctx, bias, l, b_len, t_len, row0):
    n = b_len * t_len
    npair = N_HEADS // 2
    rows = t_len // GRID_W
    kh = min(KH_MAX, rows)
    c_len = k_ctx.shape[1]
    assert row0 % t_len == 0 and kh % 2 == 0
    seq0 = row0 // t_len
    pair = pl.BlockSpec((t_len, V7X_LANES), lambda b, p: (seq0 + b, p))
    return pl.pallas_call(
        functools.partial(_nbr_attn_kernel, rows=rows, kh=kh),
        grid=(b_len, npair),
        in_specs=[pair, pair, pair,
                  pl.BlockSpec((1, c_len, V7X_LANES), lambda b, p: (b, 0, p)),
                  pl.BlockSpec((1, c_len, V7X_LANES), lambda b, p: (b, 0, p)),
                  pl.BlockSpec((None, 2) + bias.shape[2:], lambda b, p: (l, p, 0, 0, 0))],
        out_specs=pl.BlockSpec((t_len, V7X_LANES), lambda b, p: (b, p)),
        out_shape=jax.ShapeDtypeStruct((n, D_BRANCH), BF16),
        name="nbr_attn",
        compiler_params=_params("parallel", "parallel"),
    )(q, k, v, k_ctx, v_ctx, bias)


def _nbr_bias_tables(rpb):
    w = np.arange(GRID_W)
    c = np.arange(GRID_W)
    col_start = np.clip(w - KW // 2, 0, GRID_W - KW)
    inside = (c[None, :] >= col_start[:, None]) & (c[None, :] < col_start[:, None] + KW)
    col_off = c[None, :] - w[:, None] + (KW - 1)
    sel_col = ((np.arange(2 * KW - 1)[:, None, None] == col_off[None]) & inside[None]).astype(np.float32)
    toep = jnp.einsum("lhrj,jwc->lhrwc", rpb, sel_col, precision=HIGHEST)
    toep = jnp.where(inside[None, None, None], toep, NEG_BIG)
    return jnp.concatenate([toep[:, :, :-1], toep[:, :, 1:]], axis=-1)


def _top2_route(logits, n_exp):
    lane = lax.broadcasted_iota(jnp.int32, logits.shape, 1)
    lane_f = lane.astype(F32)
    lg = jnp.where(lane < n_exp, logits, -jnp.inf)
    m1 = lg.max(axis=-1, keepdims=True)
    i1 = jnp.where(lg == m1, lane_f, float(V7X_LANES)).min(axis=-1, keepdims=True)
    lg2 = jnp.where(lane_f == i1, -jnp.inf, lg)
    m2 = lg2.max(axis=-1, keepdims=True)
    i2 = jnp.where(lg2 == m2, lane_f, float(V7X_LANES)).min(axis=-1, keepdims=True)
    e = jnp.exp(m2 - m1)
    g1 = 1.0 / (1.0 + e)
    g2 = e * g1
    out = jnp.where(lane == 0, i1, 0.0)
    out = jnp.where(lane == 1, i2, out)
    out = jnp.where(lane == 2, g1, out)
    return jnp.where(lane == 3, g2, out)


def _mix_kernel(yf_ref, yb_ref, bonus_ref, g_ref, yn0_ref, yn1_ref, gate_ref, x_ref, mod_ref,
                gng_ref, gnb_ref, bd_ref, wur_ref, wun_ref, wo_ref, lng_ref, lnb_ref, *rest,
                alpha, n_exp, nb_first):
    if n_exp:
        rhi_ref, rlo_ref, x1_o, h2_o, route_o = rest
    else:
        x1_o, h2_o = rest
    d = x_ref.shape[1]
    yn = jnp.where(pl.program_id(0) < nb_first, yn0_ref[...], yn1_ref[...])
    bd = bd_ref[...]
    y = yf_ref[...] + yb_ref[...]
    inv_n = 1.0 / HEAD_DIM
    mu = _head_sum(y, bd) * inv_n
    yc = y - mu
    var = _head_sum(yc * yc, bd) * inv_n
    yr = yc * lax.rsqrt(var + GN_EPS) * gng_ref[...] + gnb_ref[...]
    yr = (yr + bonus_ref[...]) * g_ref[...]
    m = (gate_ref[:, 0:d] * _dot(yr.astype(BF16), wur_ref[...])
         + gate_ref[:, d:] * _dot(yn, wun_ref[...]))
    o = _dot(m.astype(BF16), wo_ref[...])
    g1 = mod_ref[0, 2:3, :]
    sh2 = mod_ref[0, 3:4, :]
    sc2 = mod_ref[0, 4:5, :]
    x1 = _layer_norm(alpha * x_ref[...] + g1 * o, lng_ref[...], lnb_ref[...])
    x1_o[...] = x1
    h2 = x1 * (1.0 + sc2) + sh2
    h2_o[...] = h2.astype(BF16)
    if n_exp:
        h_hi = h2.astype(BF16)
        h_lo = (h2 - h_hi.astype(F32)).astype(BF16)
        logits = _dot(h_hi, rhi_ref[...]) + (_dot(h_hi, rlo_ref[...]) + _dot(h_lo, rhi_ref[...]))
        route_o[...] = _top2_route(logits, n_exp)


def _mix_out(yf, yb, bonus, g, yn0, yn1, gates, x, modp, lw, lay, alpha, router):
    n, d = x.shape
    dr = D_BRANCH
    row = lambda i: (i, 0)
    nb_first = yn0.shape[0] // ROW_BLOCK
    yt_spec = pl.BlockSpec((ROW_BLOCK, dr), row)
    consts = [lw["gn_g"], lw["gn_b"], lw["bd"], lw["w_up_r"], lw["w_up_n"], lw["w_o"], lw["ln1_g"], lw["ln1_b"]]
    n_exp = 0
    out_specs = [pl.BlockSpec((ROW_BLOCK, d), row), pl.BlockSpec((ROW_BLOCK, d), row)]
    out_shape = [jax.ShapeDtypeStruct((n, d), F32), jax.ShapeDtypeStruct((n, d), BF16)]
    if router is not None:
        n_exp = router.shape[1]
        r_pad = jnp.pad(router, ((0, 0), (0, V7X_LANES - n_exp)))
        r_hi = r_pad.astype(BF16)
        consts += [r_hi, (r_pad - r_hi.astype(F32)).astype(BF16)]
        out_specs.append(pl.BlockSpec((ROW_BLOCK, V7X_LANES), row))
        out_shape.append(jax.ShapeDtypeStruct((n, V7X_LANES), F32))
    const_specs, const_ops = _param_specs(consts)
    return pl.pallas_call(
        functools.partial(_mix_kernel, alpha=alpha, n_exp=n_exp, nb_first=nb_first),
        grid=(n // ROW_BLOCK,),
        in_specs=[yt_spec, yt_spec,
                  pl.BlockSpec((ROW_BLOCK, dr), row), pl.BlockSpec((ROW_BLOCK, dr), row),
                  pl.BlockSpec((ROW_BLOCK, dr), lambda i: (jnp.minimum(i, nb_first - 1), 0)),
                  pl.BlockSpec((ROW_BLOCK, dr), lambda i: (jnp.maximum(i - nb_first, 0), 0)),
                  pl.BlockSpec((ROW_BLOCK, 2 * d), row),
                  pl.BlockSpec((ROW_BLOCK, d), row),
                  lay.mod_spec(d)]
                 + const_specs,
        out_specs=out_specs,
        out_shape=out_shape,
        name="mix_out",
        compiler_params=_params("parallel"),
    )(yf, yb, bonus, g, yn0, yn1, gates, x, modp, *const_ops)


def _swiglu(h, w1, w3, w2):
    u = _dot(h, w1)
    act = u * _sigmoid(u) * _dot(h, w3)
    return _dot(act.astype(BF16), w2)


def _ffn_kernel(h_ref, x1_ref, mod_ref, w1_ref, w3_ref, w2_ref, lng_ref, lnb_ref, o_ref, *, alpha):
    f = _swiglu(h_ref[...], w1_ref[...], w3_ref[...], w2_ref[...])
    g2 = mod_ref[0, 5:6, :]
    o_ref[...] = _layer_norm(alpha * x1_ref[...] + g2 * f, lng_ref[...], lnb_ref[...])


def _ffn_dense(h2, x1, modp, w1, w3, w2, ln_g, ln_b, lay, alpha):
    n, d = x1.shape
    row = lambda i: (i, 0)
    const_specs, const_ops = _param_specs([w1, w3, w2, ln_g, ln_b])
    return pl.pallas_call(
        functools.partial(_ffn_kernel, alpha=alpha),
        grid=(n // ROW_BLOCK,),
        in_specs=[pl.BlockSpec((ROW_BLOCK, d), row), pl.BlockSpec((ROW_BLOCK, d), row), lay.mod_spec(d)]
                 + const_specs,
        out_specs=pl.BlockSpec((ROW_BLOCK, d), row),
        out_shape=jax.ShapeDtypeStruct((n, d), F32),
        name="ffn_dense",
        compiler_params=_params("parallel"),
    )(h2, x1, modp, *const_ops)


def _moe_kernel(be_ref, nb_ref, x_ref, w1_ref, w3_ref, w2_ref, o_ref):
    i = pl.program_id(0)

    @pl.when(i < nb_ref[0])
    def _():
        o_ref[...] = _swiglu(x_ref[...], w1_ref[0], w3_ref[0], w2_ref[0]).astype(o_ref.dtype)

    @pl.when(i >= nb_ref[0])
    def _():
        o_ref[...] = jnp.zeros(o_ref.shape, o_ref.dtype)


def _moe_experts(block_e, n_used, xb, w1, w3, w2, lm):
    p, d = xb.shape
    dff = w1.shape[3]
    nblk = p // MOE_ROWS
    one = pl.Buffered(1)
    expert = lambda i, be, nb: (lm, be[i], 0, 0)
    grid_spec = pltpu.PrefetchScalarGridSpec(
        num_scalar_prefetch=2,
        grid=(nblk,),
        in_specs=[pl.BlockSpec((MOE_ROWS, d), lambda i, be, nb: (i, 0)),
                  pl.BlockSpec((None, 1, d, dff), expert, pipeline_mode=one),
                  pl.BlockSpec((None, 1, d, dff), expert, pipeline_mode=one),
                  pl.BlockSpec((None, 1, dff, d), expert, pipeline_mode=one)],
        out_specs=pl.BlockSpec((MOE_ROWS, d), lambda i, be, nb: (i, 0)),
    )
    return pl.pallas_call(
        _moe_kernel,
        grid_spec=grid_spec,
        out_shape=jax.ShapeDtypeStruct((p, d), BF16),
        name="moe_experts",
        compiler_params=_params("arbitrary"),
    )(block_e, n_used, xb, w1, w3, w2)


def _combine_kernel(ya_ref, yb_ref, route_ref, x1_ref, mod_ref, lng_ref, lnb_ref, o_ref, *, alpha):
    f = route_ref[:, 2:3] * ya_ref[...].astype(F32) + route_ref[:, 3:4] * yb_ref[...].astype(F32)
    g2 = mod_ref[0, 5:6, :]
    o_ref[...] = _layer_norm(alpha * x1_ref[...] + g2 * f, lng_ref[...], lnb_ref[...])


def _moe_combine(ya, yb, route, x1, modp, ln_g, ln_b, lay, alpha):
    n, d = x1.shape
    row = lambda i: (i, 0)
    return pl.pallas_call(
        functools.partial(_combine_kernel, alpha=alpha),
        grid=(n // ROW_BLOCK,),
        in_specs=[pl.BlockSpec((ROW_BLOCK, d), row), pl.BlockSpec((ROW_BLOCK, d), row),
                  pl.BlockSpec((ROW_BLOCK, V7X_LANES), row), pl.BlockSpec((ROW_BLOCK, d), row),
                  lay.mod_spec(d),
                  _const_spec(ln_g.shape), _const_spec(ln_b.shape)],
        out_specs=pl.BlockSpec((ROW_BLOCK, d), row),
        out_shape=jax.ShapeDtypeStruct((n, d), F32),
        name="moe_combine",
        compiler_params=_params("parallel"),
    )(ya, yb, route, x1, modp, ln_g, ln_b)


def _moe(h2, route, x1, modp, w1, w3, w2, lm, ln_g, ln_b, lay, alpha):
    n, d = x1.shape
    n_exp = w1.shape[1]
    top_e = route[:, 0:2].astype(jnp.int32)
    flat_e = top_e.reshape(-1)
    onehot = (flat_e[:, None] == jnp.arange(n_exp, dtype=jnp.int32)[None, :]).astype(jnp.int32)
    rank = jnp.sum((jnp.cumsum(onehot, axis=0) - onehot) * onehot, axis=1)
    counts = jnp.sum(onehot, axis=0)
    padded = (counts + MOE_ROWS - 1) // MOE_ROWS * MOE_ROWS
    pad_end = jnp.cumsum(padded)
    pad_start = pad_end - padded
    dest = pad_start[flat_e] + rank
    nblk = -(-(2 * n + n_exp * (MOE_ROWS - 1)) // MOE_ROWS)
    p = nblk * MOE_ROWS
    flat_tok = jnp.repeat(jnp.arange(n, dtype=jnp.int32), 2)
    buf_tok = jnp.full((p,), n, jnp.int32).at[dest].set(flat_tok)
    blk_start = jnp.arange(nblk, dtype=jnp.int32) * MOE_ROWS
    block_e = jnp.minimum(jnp.sum((pad_end[None, :] <= blk_start[:, None]).astype(jnp.int32), axis=1), n_exp - 1)
    n_used = (pad_end[-1] // MOE_ROWS).astype(jnp.int32).reshape(1)
    xb = jnp.concatenate([h2, jnp.zeros((1, d), h2.dtype)], axis=0)[buf_tok]
    yb = _moe_experts(block_e, n_used, xb, w1, w3, w2, lm)
    dest2 = dest.reshape(n, 2)
    return _moe_combine(yb[dest2[:, 0]], yb[dest2[:, 1]], route, x1, modp, ln_g, ln_b, lay, alpha)


def _layer(x, modp, lw, l, lay, alpha, z0, ctx_kv, new_skv):
    (bc, tc_len), (bs, ts_len) = lay.passes
    n_ctx = bc * tc_len
    slab, qn, kn, vn, k_new, v_new, gates = _in_proj(x, modp, lw["w_in"], l, lay, lw["n_slab"], *new_skv[1:])
    r, kk, lwf, lwb, bf, bb, ktf, ktb, v, bonus, g = _rwkv_prep(slab, lw, lay)
    yf, yb, s_new = _rwkv_scan(z0, r, kk, lwf, lwb, bf, bb, ktf, ktb, v, lay, new_skv[0])
    yn_ctx = _ctx_attention(qn, kn, vn, bc, tc_len)
    yn_lat = _nbr_attention(qn, kn, vn, ctx_kv[0], ctx_kv[1], lw["bias"], l, bs, ts_len, n_ctx)
    outs = _mix_out(yf, yb, bonus, g, yn_ctx, yn_lat, gates, x, modp, lw, lay, alpha, lw.get("router"))
    if "router" in lw:
        x1, h2, route = outs
        x2 = _moe(h2, route, x1, modp, lw["moe_w1"], lw["moe_w3"], lw["moe_w2"], l // 2, lw["ln2_g"], lw["ln2_b"],
                  lay, alpha)
    else:
        x1, h2 = outs
        x2 = _ffn_dense(h2, x1, modp, (lw["ffn_w1"], l // 2), (lw["ffn_w3"], l // 2), (lw["ffn_w2"], l // 2),
                        lw["ln2_g"], lw["ln2_b"], lay, alpha)
    return x2, s_new, k_new, v_new


def kernel(x_prompt, x_sample, c, state_rwkv, cache_k, cache_v, c_ctx, w_ada, b_ada, w_in, ts_w, rw_w0, rw_w2, rw_a0, rw_a2, rw_g2, rw_kk, rw_ka, rw_u, rw_gn_g, rw_gn_b, na_rpb, w_up_r, w_up_n, w_o, ln1_g, ln1_b, ln2_g, ln2_b, ffn_w1, ffn_w3, ffn_w2, moe_router, moe_w1, moe_w3, moe_w2):
    depth = w_in.shape[0]
    bc, tc_len, d = x_prompt.shape
    bs, ts_len, _ = x_sample.shape
    dr = D_BRANCH
    alpha = float((2 * depth) ** 0.25)
    n_slab = ts_w.shape[2]
    lora_w = rw_w2.shape[2]
    lora_a = rw_a2.shape[2]

    cvec = jnp.zeros((8, d), F32).at[0].set(c_ctx).at[1:1 + bs].set(c)
    mod = _ada_mod(cvec, w_ada, b_ada)

    half = np.kron(np.eye(2, dtype=np.float32), np.ones((HEAD_DIM, HEAD_DIM), np.float32))
    bd = jnp.asarray(half, BF16)
    rows = ts_len // GRID_W

    passes = ((bc, tc_len), (bs, ts_len))
    n_ctx = bc * tc_len
    bias_tables = _nbr_bias_tables(na_rpb)
    x = jnp.concatenate([x_prompt.reshape(n_ctx, d), x_sample.reshape(bs * ts_len, d)], axis=0)
    seq_row = np.concatenate([np.zeros(bc, np.int32), 1 + np.arange(bs, dtype=np.int32)])
    modp = mod[:, seq_row].reshape(depth, bc + bs, 6, d)
    z0_all = jnp.concatenate([jnp.zeros((depth, bc, 2, N_HEADS, HEAD_DIM, HEAD_DIM), F32),
                              jnp.swapaxes(jnp.swapaxes(state_rwkv, 0, 1), -1, -2)], axis=1)
    w_in_b, w_up_r_b, w_up_n_b, w_o_b = (w.astype(BF16) for w in (w_in, w_up_r, w_up_n, w_o))
    ffn_b = [w.astype(BF16) for w in (ffn_w1, ffn_w3, ffn_w2)]
    moe_b = [w.astype(BF16) for w in (moe_w1, moe_w3, moe_w2)]
    new_skv = [jnp.zeros((bc, depth, 2, N_HEADS, HEAD_DIM, HEAD_DIM), F32),
               jnp.zeros((bc, depth, dr, tc_len), F32), jnp.zeros((bc, depth, dr, tc_len), F32)]
    for l in range(depth):
        w2blk = jnp.zeros((2 * lora_w, 2 * dr), F32).at[:lora_w, :dr].set(rw_w2[l, 0]).at[lora_w:, dr:].set(rw_w2[l, 1])
        a2blk = jnp.zeros((2 * lora_a, 2 * dr), F32).at[:lora_a, :dr].set(rw_a2[l, 0]).at[lora_a:, dr:].set(rw_a2[l, 1])
        lw = dict(
            n_slab=n_slab,
            w_in=w_in_b, ts=ts_w[l],
            w0=rw_w0[l].reshape(1, 2 * dr), w2=w2blk.astype(BF16),
            a0=rw_a0[l].reshape(1, 2 * dr), a2=a2blk.astype(BF16),
            g2=rw_g2[l].astype(BF16), kkw=rw_kk[l].reshape(1, dr), ka=rw_ka[l].reshape(1, dr),
            u=rw_u[l].reshape(1, dr), bd=bd,
            gn_g=rw_gn_g[l].reshape(1, dr), gn_b=rw_gn_b[l].reshape(1, dr),
            w_up_r=(w_up_r_b, l), w_up_n=(w_up_n_b, l), w_o=(w_o_b, l),
            ln1_g=ln1_g[l].reshape(1, d), ln1_b=ln1_b[l].reshape(1, d),
            ln2_g=ln2_g[l].reshape(1, d), ln2_b=ln2_b[l].reshape(1, d),
            bias=bias_tables,
        )
        if l % 2 == 0:
            lw.update(ffn_w1=ffn_b[0], ffn_w3=ffn_b[1], ffn_w2=ffn_b[2])
        else:
            lw.update(router=moe_router[l // 2], moe_w1=moe_b[0], moe_w3=moe_b[1], moe_w2=moe_b[2])

        kv = (cache_k[:, l].reshape(bs, -1, dr), cache_v[:, l].reshape(bs, -1, dr))
        x, *new_skv = _layer(x, modp, lw, l, _Layout(passes, l), alpha, z0_all, kv, new_skv)

    cache_out = lambda a: jnp.transpose(a.reshape(bc, depth, N_HEADS, HEAD_DIM, tc_len), (0, 1, 4, 2, 3))
    return (x[:n_ctx].reshape(bc, tc_len, d), x[n_ctx:].reshape(bs, ts_len, d),
            new_skv[0], cache_out(new_skv[1]), cache_out(new_skv[2]))
```

```python
import functools

import numpy as np
import jax
import jax.numpy as jnp
from jax import lax
from jax.experimental import pallas as pl
from jax.experimental.pallas import tpu as pltpu

F32 = jnp.float32
BF16 = jnp.bfloat16
HIGHEST = lax.Precision.HIGHEST

N_HEADS = 8
HEAD_DIM = 64
D_BRANCH = N_HEADS * HEAD_DIM
V7X_LANES = 128
ROW_BLOCK = 256
SCAN_CHUNK = 64
SCAN_CHUNKS_PER_STEP = 2
MOE_ROWS = 256
KH_MAX = 8
KW = 16
GRID_W = 64
LN_EPS = 1e-5
GN_EPS = 64e-5
NEG_BIG = -1e30
VMEM_LIMIT = 56 * 1024 * 1024


def _params(*sem):
    return pltpu.CompilerParams(dimension_semantics=sem, vmem_limit_bytes=VMEM_LIMIT)


def _const_spec(shape):
    nd = len(shape)
    return pl.BlockSpec(shape, lambda *_: (0,) * nd, pipeline_mode=pl.Buffered(1))


def _layer_spec(stack, l):
    nd = stack.ndim - 1
    return pl.BlockSpec((None,) + stack.shape[1:], lambda *_: (l,) + (0,) * nd, pipeline_mode=pl.Buffered(1))


def _param_specs(params):
    specs = [_layer_spec(*p) if isinstance(p, tuple) else _const_spec(p.shape) for p in params]
    return specs, [p[0] if isinstance(p, tuple) else p for p in params]


class _Layout:
    def __init__(self, passes, layer=0):
        self.passes = passes
        self.layer = layer
        self.n_seq = sum(b for b, _ in passes)

    def _per_block(self, fn):
        def at(i):
            out, blk0, seq0 = None, 0, 0
            for b, t in self.passes:
                per = t // ROW_BLOCK
                val = fn(i - blk0, per, seq0)
                out = val if out is None else jnp.where(i >= blk0, val, out)
                blk0 += b * per
                seq0 += b
            return out
        return at

    def seq_of_block(self, i):
        return self._per_block(lambda j, per, s0: s0 + j // per)(i)

    def is_first(self, i):
        return self._per_block(lambda j, per, s0: (j % per) == 0)(i)

    def is_last(self, i):
        return self._per_block(lambda j, per, s0: (j % per) == per - 1)(i)

    def mod_spec(self, d):
        return pl.BlockSpec((None, 1, 6, d), lambda i: (self.layer, self.seq_of_block(i), 0, 0))


def _dot(a, b):
    return jnp.dot(a, b, preferred_element_type=F32)


def _dot_nt(a, b):
    return lax.dot_general(a, b, (((1,), (1,)), ((), ())), preferred_element_type=F32)


def _dot_tn(a, b):
    return lax.dot_general(a, b, (((0,), (0,)), ((), ())), preferred_element_type=F32)


def _mm1(a, b, f=_dot):
    return f(a.astype(BF16), b.astype(BF16))


def _mm_mask_lhs(m, x):
    hi = x.astype(BF16)
    lo = (x - hi.astype(F32)).astype(BF16)
    return _dot(m, hi) + _dot(m, lo)


def _sigmoid(x):
    return 1.0 / (1.0 + jnp.exp(-x))


def _head_sum(x, bd):
    outs = []
    for j in range(x.shape[1] // V7X_LANES):
        xs = x[:, j * V7X_LANES:(j + 1) * V7X_LANES]
        hi = xs.astype(BF16)
        lo = (xs - hi.astype(F32)).astype(BF16)
        outs.append(_dot(hi, bd) + _dot(lo, bd))
    return jnp.concatenate(outs, axis=1)


def _layer_norm(z, g, b):
    mu = jnp.mean(z, axis=-1, keepdims=True)
    zc = z - mu
    var = jnp.mean(zc * zc, axis=-1, keepdims=True)
    return zc * lax.rsqrt(var + LN_EPS) * g + b


def _ada_kernel(c_ref, w_ref, b_ref, o_ref):
    cs = c_ref[...]
    s = cs * _sigmoid(cs)
    o_ref[0] = jnp.dot(s, w_ref[0], preferred_element_type=F32, precision=HIGHEST) + b_ref[0]


def _ada_mod(cvec, w_ada, b_ada):
    depth, d, n6 = w_ada.shape
    tn = n6 // 4
    return pl.pallas_call(
        _ada_kernel,
        grid=(depth, n6 // tn),
        in_specs=[pl.BlockSpec((8, d), lambda l, j: (0, 0)),
                  pl.BlockSpec((1, d, tn), lambda l, j: (l, 0, j)),
                  pl.BlockSpec((1, 1, tn), lambda l, j: (l, 0, j))],
        out_specs=pl.BlockSpec((1, 8, tn), lambda l, j: (l, 0, j)),
        out_shape=jax.ShapeDtypeStruct((depth, 8, n6), F32),
        name="ada_mod",
        compiler_params=_params("parallel", "parallel"),
    )(cvec, w_ada, b_ada.reshape(depth, 1, n6))


def _inproj_kernel(x_ref, mod_ref, w_ref, kbuf_ref, vbuf_ref, slab_ref, q_ref, k_ref, v_ref, kc_ref, vc_ref, gate_ref,
                   *, n_slab, nb_cache):
    del kbuf_ref, vbuf_ref
    sh = mod_ref[0, 0:1, :]
    sc = mod_ref[0, 1:2, :]
    h = (x_ref[...] * (1.0 + sc) + sh).astype(BF16)
    dn = D_BRANCH
    slab_ref[...] = _dot(h, w_ref[:, 0:n_slab])
    q_ref[...] = _dot(h, w_ref[:, n_slab:n_slab + dn]).astype(q_ref.dtype)
    k = _dot(h, w_ref[:, n_slab + dn:n_slab + 2 * dn])
    v = _dot(h, w_ref[:, n_slab + 2 * dn:n_slab + 3 * dn])
    k_ref[...] = k.astype(k_ref.dtype)
    v_ref[...] = v.astype(v_ref.dtype)

    @pl.when(pl.program_id(0) < nb_cache)
    def _():
        kc_ref[0] = k.T
        vc_ref[0] = v.T

    gate_ref[...] = _sigmoid(_dot(h, w_ref[:, n_slab + 3 * dn:]))


def _in_proj(x, modp, w_in, l, lay, n_slab, cache_k, cache_v):
    n, d = x.shape
    n_cols = w_in.shape[2]
    dn = D_BRANCH
    n_gate = n_cols - n_slab - 3 * dn
    b_cache, t_cache = lay.passes[0]
    per_seq = t_cache // ROW_BLOCK
    nb_cache = b_cache * per_seq
    row = lambda i: (i, 0)

    def cache_block(i):
        ic = jnp.minimum(i, nb_cache - 1)
        return ic // per_seq, l, 0, ic % per_seq

    cache_spec = pl.BlockSpec((1, None, dn, ROW_BLOCK), cache_block)
    any_spec = pl.BlockSpec(memory_space=pl.ANY)
    return pl.pallas_call(
        functools.partial(_inproj_kernel, n_slab=n_slab, nb_cache=nb_cache),
        grid=(n // ROW_BLOCK,),
        in_specs=[pl.BlockSpec((ROW_BLOCK, d), row),
                  lay.mod_spec(d),
                  _layer_spec(w_in, l),
                  any_spec, any_spec],
        out_specs=[pl.BlockSpec((ROW_BLOCK, n_slab), row),
                   pl.BlockSpec((ROW_BLOCK, dn), row), pl.BlockSpec((ROW_BLOCK, dn), row),
                   pl.BlockSpec((ROW_BLOCK, dn), row),
                   cache_spec, cache_spec,
                   pl.BlockSpec((ROW_BLOCK, n_gate), row)],
        out_shape=[jax.ShapeDtypeStruct((n, n_slab), F32),
                   jax.ShapeDtypeStruct((n, dn), BF16), jax.ShapeDtypeStruct((n, dn), BF16),
                   jax.ShapeDtypeStruct((n, dn), BF16),
                   jax.ShapeDtypeStruct(cache_k.shape, F32), jax.ShapeDtypeStruct(cache_v.shape, F32),
                   jax.ShapeDtypeStruct((n, n_gate), F32)],
        input_output_aliases={3: 4, 4: 5},
        name="in_proj",
        compiler_params=_params("arbitrary"),
    )(x, modp, w_in, cache_k, cache_v)


def _prep_kernel(cur_ref, prev_ref, next_ref, ts_ref, w0_ref, w2_ref, a0_ref, a2_ref, g2_ref,
                 kkw_ref, ka_ref, u_ref, bd_ref,
                 r_o, kk_o, lwf_o, lwb_o, bf_o, bb_o, ktf_o, ktb_o, v_o, bonus_o, g_o,
                 *, lay, lora_w, lora_a):
    i = pl.program_id(0)
    tb, n_slab = cur_ref.shape
    dr = D_BRANCH
    cur = cur_ref[...]
    first = lay.is_first(i)
    last = lay.is_last(i)
    prow = jnp.where(first, 0.0, prev_ref[7:8, :])
    nrow = jnp.where(last, 0.0, next_ref[0:1, :])
    rid = lax.broadcasted_iota(jnp.int32, (tb, n_slab), 0)
    xp = jnp.where(rid == 0, prow, pltpu.roll(cur, 1, 0))
    xn = jnp.where(rid == tb - 1, nrow, pltpu.roll(cur, tb - 1, 0))
    s = ts_ref[0:1, :] * xp + ts_ref[1:2, :] * cur + ts_ref[2:3, :] * xn

    r = s[:, 0:dr]
    k = s[:, dr:2 * dr]
    v = s[:, 2 * dr:3 * dr]
    o = 3 * dr
    wd = s[:, o:o + 2 * lora_w]
    ad = s[:, o + 2 * lora_w:o + 2 * lora_w + 2 * lora_a]
    gd = s[:, o + 2 * lora_w + 2 * lora_a:]

    wl = w0_ref[...] + _dot(jnp.tanh(wd).astype(BF16), w2_ref[...])
    logw = -float(np.exp(-0.5)) * _sigmoid(wl)
    a = _sigmoid(a0_ref[...] + _dot(ad.astype(BF16), a2_ref[...]))
    g = _dot(_sigmoid(gd).astype(BF16), g2_ref[...])

    bd = bd_ref[...]
    kk = k * kkw_ref[...]
    kk = kk * lax.rsqrt(_head_sum(kk * kk, bd) + 1e-12)
    ka = ka_ref[...]
    a_f = a[:, :dr]
    a_b = a[:, dr:]
    kt_f = k * (1.0 + (a_f - 1.0) * ka)
    kt_b = k * (1.0 + (a_b - 1.0) * ka)
    bonus = _head_sum(r * (0.5 * (kt_f + kt_b)) * u_ref[...], bd) * v

    r_o[...] = r
    kk_o[...] = kk
    lwf_o[...] = logw[:, :dr]
    lwb_o[...] = logw[:, dr:]
    bf_o[...] = kk * a_f
    bb_o[...] = kk * a_b
    ktf_o[...] = kt_f
    ktb_o[...] = kt_b
    v_o[...] = v
    bonus_o[...] = bonus
    g_o[...] = g


def _rwkv_prep(slab, lw, lay):
    n, n_slab = slab.shape
    dr = D_BRANCH
    sub = ROW_BLOCK // 8
    lora_w = lw["w2"].shape[0] // 2
    lora_a = lw["a2"].shape[0] // 2
    row = lambda i: (i, 0)
    consts = [lw["ts"], lw["w0"], lw["w2"], lw["a0"], lw["a2"], lw["g2"], lw["kkw"], lw["ka"], lw["u"], lw["bd"]]
    outs = pl.pallas_call(
        functools.partial(_prep_kernel, lay=lay, lora_w=lora_w, lora_a=lora_a),
        grid=(n // ROW_BLOCK,),
        in_specs=[pl.BlockSpec((ROW_BLOCK, n_slab), row),
                  pl.BlockSpec((8, n_slab), lambda i: (jnp.maximum(i * sub - 1, 0), 0)),
                  pl.BlockSpec((8, n_slab), lambda i: (jnp.minimum((i + 1) * sub, n // 8 - 1), 0))]
                 + [_const_spec(c.shape) for c in consts],
        out_specs=[pl.BlockSpec((ROW_BLOCK, dr), row)] * 11,
        out_shape=[jax.ShapeDtypeStruct((n, dr), F32)] * 11,
        name="rwkv_prep",
        compiler_params=_params("parallel"),
    )(slab, slab, slab, *consts)
    return outs


def _tri_masks(rev):
    c = SCAN_CHUNK
    ti = lax.broadcasted_iota(jnp.int32, (c, 2 * c), 0)
    ii = lax.broadcasted_iota(jnp.int32, (c, 2 * c), 1) % c
    if rev:
        ti, ii = c - 1 - ti, c - 1 - ii
    one = lambda m: jnp.where(m, 1.0, 0.0)
    levels = []
    s = 1
    while s < c:
        same = (ti // (2 * s)) == (ii // (2 * s))
        levels.append(one(same & ((ti % (2 * s)) >= s) & ((ii % (2 * s)) < s)))
        s *= 2
    return one(ii <= ti), one(ii < ti), levels


def _scan_kernel(fwd_tab, bwd_tab, seq_tab, first_tab, last_tab,
                 z0_ref, rf, kkf, lwf, bf, ktf, vf, rb, kkb, lwb, bb, ktb, vb, sbuf_ref, yf_o, yb_o, sfin_o, z_scr,
                 *, n_state_out):
    del sbuf_ref
    g_step = pl.program_id(0)
    c = SCAN_CHUNK
    lanes = V7X_LANES
    npair = N_HEADS // 2
    assert c == HEAD_DIM and 2 * HEAD_DIM == lanes

    row_blk = lax.broadcasted_iota(jnp.int32, (lanes, lanes), 0) // HEAD_DIM
    col_blk = lax.broadcasted_iota(jnp.int32, (lanes, lanes), 1) // HEAD_DIM
    bd_mask = jnp.where(row_blk == col_blk, 1.0, 0.0)
    eye_full = jnp.where(lax.broadcasted_iota(jnp.int32, (lanes, lanes), 0)
                         == lax.broadcasted_iota(jnp.int32, (lanes, lanes), 1), 1.0, 0.0)
    lane_c = lax.broadcasted_iota(jnp.int32, (c, lanes), 1)
    lo = jnp.where(lane_c < HEAD_DIM, 1.0, 0.0)
    hi = 1.0 - lo
    eye_pair = jnp.where(lax.broadcasted_iota(jnp.int32, (c, lanes), 0) == lane_c % HEAD_DIM, 1.0, 0.0)

    def bd(x):
        return jnp.concatenate([x * lo, x * hi], axis=0)

    @pl.when(first_tab[g_step] == 1)
    def _():
        for d in range(2):
            for p in range(npair):
                za = jnp.concatenate([z0_ref[0, d, 2 * p], z0_ref[0, d, 2 * p + 1]], axis=1)
                z_scr[d * npair + p] = bd(za)

    refs = ((rf, kkf, lwf, bf, ktf, vf, yf_o), (rb, kkb, lwb, bb, ktb, vb, yb_o))
    masks = (_tri_masks(False), _tri_masks(True))
    cum_lhs = [m[0][:, 0:c].astype(BF16) for m in masks]
    last_row = (c - 1, 0)
    nck = rf.shape[0] // c
    units = [(d, p, ck) for d in range(2) for p in range(npair) for ck in range(nck)]

    def tile(d, p, ck):
        row0 = (nck - 1 - ck if d else ck) * c
        return slice(row0, row0 + c), slice(p * lanes, (p + 1) * lanes)

    def per(fn):
        return [fn(i, d, tile(d, p, ck)) for i, (d, p, ck) in enumerate(units)]

    lw = per(lambda i, d, sl: refs[d][2][sl])
    cum = per(lambda i, d, sl: _mm_mask_lhs(cum_lhs[d], lw[i]))
    tot = per(lambda i, d, sl: jnp.broadcast_to(cum[i][last_row[d]:last_row[d] + 1], (c, lanes)))
    r_s = per(lambda i, d, sl: refs[d][0][sl] * jnp.exp(cum[i]))
    k_s = per(lambda i, d, sl: refs[d][1][sl] * jnp.exp(cum[i] - lw[i]))
    e_neg = per(lambda i, d, sl: jnp.exp(-cum[i]))
    e_end = per(lambda i, d, sl: jnp.exp(tot[i] - cum[i]))
    b = per(lambda i, d, sl: refs[d][3][sl])
    kt = per(lambda i, d, sl: refs[d][4][sl])
    v = per(lambda i, d, sl: refs[d][5][sl])
    v_bd = per(lambda i, d, sl: bd(v[i]))
    kr = per(lambda i, d, sl: jnp.concatenate([k_s[i], r_s[i]], axis=0))
    ab = per(lambda i, d, sl: _mm1(kr[i], bd(b[i] * e_neg[i]), _dot_nt))
    ak = per(lambda i, d, sl: _mm1(kr[i], bd(kt[i] * e_neg[i]), _dot_nt))
    a_b = per(lambda i, d, sl: ab[i][0:c] * masks[d][1])
    b_b = per(lambda i, d, sl: ab[i][c:] * masks[d][0])
    a_kt = per(lambda i, d, sl: ak[i][0:c] * masks[d][1])
    b_kt = per(lambda i, d, sl: ak[i][c:] * masks[d][0])
    dm = per(lambda i, d, sl: eye_pair - a_b[i] * masks[d][2][0])
    for lvl in range(1, len(masks[0][2])):
        zd = per(lambda i, d, sl: _mm1(a_b[i] * masks[d][2][lvl], bd(dm[i])))
        dm = per(lambda i, d, sl: dm[i] - _mm1(dm[i], bd(zd[i])))
    akv = per(lambda i, d, sl: _mm1(jnp.concatenate([a_kt[i], b_kt[i]], axis=0), v_bd[i]))
    w1 = [x[0:c] for x in akv]
    g = per(lambda i, d, sl: _mm1(dm[i], jnp.concatenate([bd(k_s[i]), bd(w1[i])], axis=1)))
    bg = per(lambda i, d, sl: _mm1(b_b[i], jnp.concatenate([bd(g[i][:, 0:lanes]), bd(g[i][:, lanes:])], axis=1)))
    q = per(lambda i, d, sl: r_s[i] - bg[i][:, 0:lanes])
    y0 = per(lambda i, d, sl: akv[i][c:] - bg[i][:, lanes:])
    cg = per(lambda i, d, sl: _mm1(b[i] * e_end[i], g[i], _dot_tn))
    phi = per(lambda i, d, sl: eye_full * jnp.exp(jnp.concatenate([tot[i], tot[i]], axis=0))
              - cg[i][:, 0:lanes] * bd_mask)
    psi = per(lambda i, d, sl: (_mm1(kt[i] * e_end[i], v[i], _dot_tn) - cg[i][:, lanes:]) * bd_mask)
    z = [z_scr[s] for s in range(2 * npair)]
    for ck in range(nck):
        sel = [(i, d, p) for i, (d, p, k2) in enumerate(units) if k2 == ck]
        qz = [_mm1(jnp.concatenate([q[i], phi[i]], axis=0), z[d * npair + p]) for i, d, p in sel]
        y = [x[0:c] + y0[i] for x, (i, d, p) in zip(qz, sel)]
        z = [x[c:] + psi[i] for x, (i, d, p) in zip(qz, sel)]
        for yy, (i, d, p) in zip(y, sel):
            refs[d][6][tile(d, p, ck)] = yy
    for s in range(2 * npair):
        z_scr[s] = z[s]

    @pl.when(jnp.logical_and(last_tab[g_step] == 1, seq_tab[g_step] < n_state_out))
    def _():
        for d in range(2):
            for p in range(npair):
                zt = z_scr[d * npair + p].T
                sfin_o[0, d, 2 * p] = zt[0:HEAD_DIM, 0:HEAD_DIM]
                sfin_o[0, d, 2 * p + 1] = zt[HEAD_DIM:, HEAD_DIM:]


def _scan_tables(lay):
    blk = SCAN_CHUNK * SCAN_CHUNKS_PER_STEP
    rows, blk0, seq0 = [], 0, 0
    for b, t in lay.passes:
        nj = t // blk
        for s in range(b):
            for j in range(nj):
                rows.append((blk0 + s * nj + j, blk0 + s * nj + nj - 1 - j, seq0 + s, int(j == 0), int(j == nj - 1)))
        blk0 += b * nj
        seq0 += b
    return [jnp.asarray(col, jnp.int32) for col in zip(*rows)]


def _rwkv_scan(z0, r, kk, lwf, lwb, bf, bb, ktf, ktb, v, lay, new_state):
    blk = SCAN_CHUNK * SCAN_CHUNKS_PER_STEP
    n, dr = r.shape
    n_out = new_state.shape[0]
    assert n_out == lay.passes[0][0]
    tabs = _scan_tables(lay)
    fwd = pl.BlockSpec((blk, dr), lambda g, fw, bw, sq, fi, la: (fw[g], 0))
    bwd = pl.BlockSpec((blk, dr), lambda g, fw, bw, sq, fi, la: (bw[g], 0))
    st_out = pl.BlockSpec((1, None, 2, N_HEADS, HEAD_DIM, HEAD_DIM),
                          lambda g, fw, bw, sq, fi, la: (jnp.minimum(sq[g], n_out - 1), lay.layer, 0, 0, 0, 0))
    st_in = pl.BlockSpec((None, 1, 2, N_HEADS, HEAD_DIM, HEAD_DIM),
                         lambda g, fw, bw, sq, fi, la: (lay.layer, sq[g], 0, 0, 0, 0))
    grid_spec = pltpu.PrefetchScalarGridSpec(
        num_scalar_prefetch=len(tabs),
        grid=(int(tabs[0].shape[0]),),
        in_specs=[st_in] + [fwd] * 6 + [bwd] * 6 + [pl.BlockSpec(memory_space=pl.ANY)],
        out_specs=[fwd, bwd, st_out],
        scratch_shapes=[pltpu.VMEM((N_HEADS, V7X_LANES, V7X_LANES), F32)],
    )
    n_in = len(tabs) + 13
    return pl.pallas_call(
        functools.partial(_scan_kernel, n_state_out=n_out),
        grid_spec=grid_spec,
        out_shape=[jax.ShapeDtypeStruct((n, dr), F32),
                   jax.ShapeDtypeStruct((n, dr), F32),
                   jax.ShapeDtypeStruct(new_state.shape, F32)],
        input_output_aliases={n_in: 2},
        name="rwkv_scan",
        compiler_params=_params("arbitrary"),
    )(*tabs, z0, r, kk, lwf, bf, ktf, v, r, kk, lwb, bb, ktb, v, new_state)


def _softmax_rows(parts):
    m = parts[0].max(axis=-1, keepdims=True)
    for p in parts[1:]:
        m = jnp.maximum(m, p.max(axis=-1, keepdims=True))
    es = [jnp.exp(p - m) for p in parts]
    den = es[0].sum(axis=-1, keepdims=True)
    for e in es[1:]:
        den = den + e.sum(axis=-1, keepdims=True)
    inv = 1.0 / den
    return [e * inv for e in es]


def _ctx_attn_kernel(q_ref, k_ref, v_ref, o_ref):
    scale = HEAD_DIM ** -0.5
    npair = q_ref.shape[1] // V7X_LANES
    tile = lambda ref, p: ref[:, p * V7X_LANES:(p + 1) * V7X_LANES].astype(BF16)
    qb = [tile(q_ref, p) for p in range(npair)]
    k = [tile(k_ref, p) for p in range(npair)]
    v = [tile(v_ref, p) for p in range(npair)]
    first_head = lax.broadcasted_iota(jnp.int32, qb[0].shape, 1) < HEAD_DIM
    head_lanes = (first_head, jnp.logical_not(first_head))
    units = [(p, hh) for p in range(npair) for hh in range(2)]

    def per(fn):
        return [fn(u, p, hh) for u, (p, hh) in enumerate(units)]

    q = per(lambda u, p, hh: jnp.where(head_lanes[hh], qb[p], jnp.zeros_like(qb[p])))
    s = per(lambda u, p, hh: _dot_nt(q[u], k[p]) * scale)
    e = per(lambda u, p, hh: jnp.exp(s[u] - s[u].max(axis=-1, keepdims=True)))
    pr = per(lambda u, p, hh: (e[u] * (1.0 / e[u].sum(axis=-1, keepdims=True))).astype(BF16))
    out = per(lambda u, p, hh: _dot(pr[u], v[p]))
    for p in range(npair):
        o_ref[:, p * V7X_LANES:(p + 1) * V7X_LANES] = jnp.where(first_head, out[2 * p], out[2 * p + 1]).astype(o_ref.dtype)


def _ctx_attention(q, k, v, b_len, t_len):
    n = b_len * t_len
    seq = pl.BlockSpec((t_len, D_BRANCH), lambda b: (b, 0))
    return pl.pallas_call(
        _ctx_attn_kernel,
        grid=(b_len,),
        in_specs=[seq, seq, seq],
        out_specs=seq,
        out_shape=jax.ShapeDtypeStruct((n, D_BRANCH), BF16),
        name="ctx_attn",
        compiler_params=_params("parallel"),
    )(q, k, v)


def _nbr_attn_kernel(q_ref, k_ref, v_ref, kc_ref, vc_ref, bias_ref, o_ref, *, rows, kh):
    scale = HEAD_DIM ** -0.5
    win = kh * GRID_W

    group = 8 if rows % 8 == 0 else 1
    units = [(r, hh) for r in range(group) for hh in range(2)]
    first_head = lax.broadcasted_iota(jnp.int32, (GRID_W, V7X_LANES), 1) < HEAD_DIM
    head_lanes = (first_head, jnp.logical_not(first_head))

    def body(gi, carry):
        def per(fn):
            return [fn(u, r, hh) for u, (r, hh) in enumerate(units)]

        i = [gi * group + r for r in range(group)]
        rs = [jnp.clip(x - kh // 2, 0, rows - kh) for x in i]
        q0 = [pl.multiple_of(x * GRID_W, GRID_W) for x in i]
        k0 = [pl.multiple_of(x * GRID_W, GRID_W) for x in rs]
        qb = [q_ref[pl.ds(x, GRID_W), :].astype(BF16) for x in q0]
        kw = [k_ref[pl.ds(x, win), :].astype(BF16) for x in k0]
        vw = [v_ref[pl.ds(x, win), :].astype(BF16) for x in k0]
        kc = kc_ref[0].astype(BF16)
        vc = vc_ref[0].astype(BF16)
        q = per(lambda u, r, hh: jnp.where(head_lanes[hh], qb[r], jnp.zeros_like(qb[r])))
        off0 = [(KH_MAX - 1) - (i[r] - rs[r]) for r in range(group)]
        bias = per(lambda u, r, hh: jnp.concatenate([bias_ref[hh, off0[r] + 2 * a2] for a2 in range(kh // 2)], axis=1))
        s_loc = per(lambda u, r, hh: _dot_nt(q[u], kw[r]) * scale + bias[u])
        s_ctx = per(lambda u, r, hh: _dot_nt(q[u], kc) * scale)
        m = per(lambda u, r, hh: jnp.maximum(s_loc[u].max(axis=-1, keepdims=True), s_ctx[u].max(axis=-1, keepdims=True)))
        e_loc = per(lambda u, r, hh: jnp.exp(s_loc[u] - m[u]))
        e_ctx = per(lambda u, r, hh: jnp.exp(s_ctx[u] - m[u]))
        inv = per(lambda u, r, hh: 1.0 / (e_loc[u].sum(axis=-1, keepdims=True) + e_ctx[u].sum(axis=-1, keepdims=True)))
        out = per(lambda u, r, hh: _dot((e_loc[u] * inv[u]).astype(BF16), vw[r])
                  + _dot((e_ctx[u] * inv[u]).astype(BF16), vc))
        for r in range(group):
            o_ref[pl.ds(q0[r], GRID_W), :] = jnp.where(first_head, out[2 * r], out[2 * r + 1]).astype(o_ref.dtype)
        return carry

    lax.fori_loop(0, rows // group, body, 0)


def _nbr_attention(q, k, v, k_ctx, v_ctx, bias, l, b_len, t_len, row0):
    n = b_len * t_len
    npair = N_HEADS // 2
    rows = t_len // GRID_W
    kh = min(KH_MAX, rows)
    c_len = k_ctx.shape[1]
    assert row0 % t_len == 0 and kh % 2 == 0
    seq0 = row0 // t_len
    pair = pl.BlockSpec((t_len, V7X_LANES), lambda b, p: (seq0 + b, p))
    return pl.pallas_call(
        functools.partial(_nbr_attn_kernel, rows=rows, kh=kh),
        grid=(b_len, npair),
        in_specs=[pair, pair, pair,
                  pl.BlockSpec((1, c_len, V7X_LANES), lambda b, p: (b, 0, p)),
                  pl.BlockSpec((1, c_len, V7X_LANES), lambda b, p: (b, 0, p)),
                  pl.BlockSpec((None, 2) + bias.shape[2:], lambda b, p: (l, p, 0, 0, 0))],
        out_specs=pl.BlockSpec((t_len, V7X_LANES), lambda b, p: (b, p)),
        out_shape=jax.ShapeDtypeStruct((n, D_BRANCH), BF16),
        name="nbr_attn",
        compiler_params=_params("parallel", "parallel"),
    )(q, k, v, k_ctx, v_ctx, bias)


def _nbr_bias_tables(rpb):
    w = np.arange(GRID_W)
    c = np.arange(GRID_W)
    col_start = np.clip(w - KW // 2, 0, GRID_W - KW)
    inside = (c[None, :] >= col_start[:, None]) & (c[None, :] < col_start[:, None] + KW)
    col_off = c[None, :] - w[:, None] + (KW - 1)
    sel_col = ((np.arange(2 * KW - 1)[:, None, None] == col_off[None]) & inside[None]).astype(np.float32)
    toep = jnp.einsum("lhrj,jwc->lhrwc", rpb, sel_col, precision=HIGHEST)
    toep = jnp.where(inside[None, None, None], toep, NEG_BIG)
    return jnp.concatenate([toep[:, :, :-1], toep[:, :, 1:]], axis=-1)


def _top2_route(logits, n_exp):
    lane = lax.broadcasted_iota(jnp.int32, logits.shape, 1)
    lane_f = lane.astype(F32)
    lg = jnp.where(lane < n_exp, logits, -jnp.inf)
    m1 = lg.max(axis=-1, keepdims=True)
    i1 = jnp.where(lg == m1, lane_f, float(V7X_LANES)).min(axis=-1, keepdims=True)
    lg2 = jnp.where(lane_f == i1, -jnp.inf, lg)
    m2 = lg2.max(axis=-1, keepdims=True)
    i2 = jnp.where(lg2 == m2, lane_f, float(V7X_LANES)).min(axis=-1, keepdims=True)
    e = jnp.exp(m2 - m1)
    g1 = 1.0 / (1.0 + e)
    g2 = e * g1
    out = jnp.where(lane == 0, i1, 0.0)
    out = jnp.where(lane == 1, i2, out)
    out = jnp.where(lane == 2, g1, out)
    return jnp.where(lane == 3, g2, out)


def _mix_kernel(yf_ref, yb_ref, bonus_ref, g_ref, yn0_ref, yn1_ref, gate_ref, x_ref, mod_ref,
                gng_ref, gnb_ref, bd_ref, wur_ref, wun_ref, wo_ref, lng_ref, lnb_ref, *rest,
                alpha, n_exp, nb_first):
    if n_exp:
        rhi_ref, rlo_ref, x1_o, h2_o, route_o = rest
    else:
        x1_o, h2_o = rest
    d = x_ref.shape[1]
    yn = jnp.where(pl.program_id(0) < nb_first, yn0_ref[...], yn1_ref[...])
    bd = bd_ref[...]
    y = yf_ref[...] + yb_ref[...]
    inv_n = 1.0 / HEAD_DIM
    mu = _head_sum(y, bd) * inv_n
    yc = y - mu
    var = _head_sum(yc * yc, bd) * inv_n
    yr = yc * lax.rsqrt(var + GN_EPS) * gng_ref[...] + gnb_ref[...]
    yr = (yr + bonus_ref[...]) * g_ref[...]
    m = (gate_ref[:, 0:d] * _dot(yr.astype(BF16), wur_ref[...])
         + gate_ref[:, d:] * _dot(yn, wun_ref[...]))
    o = _dot(m.astype(BF16), wo_ref[...])
    g1 = mod_ref[0, 2:3, :]
    sh2 = mod_ref[0, 3:4, :]
    sc2 = mod_ref[0, 4:5, :]
    x1 = _layer_norm(alpha * x_ref[...] + g1 * o, lng_ref[...], lnb_ref[...])
    x1_o[...] = x1
    h2 = x1 * (1.0 + sc2) + sh2
    h2_o[...] = h2.astype(h2_o.dtype)
    if n_exp:
        h_hi = h2.astype(BF16)
        h_lo = (h2 - h_hi.astype(F32)).astype(BF16)
        logits = _dot(h_hi, rhi_ref[...]) + (_dot(h_hi, rlo_ref[...]) + _dot(h_lo, rhi_ref[...]))
        route_o[...] = _top2_route(logits, n_exp)


def _mix_out(yf, yb, bonus, g, yn0, yn1, gates, x, modp, lw, lay, alpha, router):
    n, d = x.shape
    dr = D_BRANCH
    row = lambda i: (i, 0)
    nb_first = yn0.shape[0] // ROW_BLOCK
    yt_spec = pl.BlockSpec((ROW_BLOCK, dr), row)
    consts = [lw["gn_g"], lw["gn_b"], lw["bd"], lw["w_up_r"], lw["w_up_n"], lw["w_o"], lw["ln1_g"], lw["ln1_b"]]
    n_exp = 0
    out_specs = [pl.BlockSpec((ROW_BLOCK, d), row), pl.BlockSpec((ROW_BLOCK, d), row)]
    out_shape = [jax.ShapeDtypeStruct((n, d), F32), jax.ShapeDtypeStruct((n, d), BF16 if router is None else F32)]
    if router is not None:
        n_exp = router.shape[1]
        r_pad = jnp.pad(router, ((0, 0), (0, V7X_LANES - n_exp)))
        r_hi = r_pad.astype(BF16)
        consts += [r_hi, (r_pad - r_hi.astype(F32)).astype(BF16)]
        out_specs.append(pl.BlockSpec((ROW_BLOCK, V7X_LANES), row))
        out_shape.append(jax.ShapeDtypeStruct((n, V7X_LANES), F32))
    const_specs, const_ops = _param_specs(consts)
    return pl.pallas_call(
        functools.partial(_mix_kernel, alpha=alpha, n_exp=n_exp, nb_first=nb_first),
        grid=(n // ROW_BLOCK,),
        in_specs=[yt_spec, yt_spec,
                  pl.BlockSpec((ROW_BLOCK, dr), row), pl.BlockSpec((ROW_BLOCK, dr), row),
                  pl.BlockSpec((ROW_BLOCK, dr), lambda i: (jnp.minimum(i, nb_first - 1), 0)),
                  pl.BlockSpec((ROW_BLOCK, dr), lambda i: (jnp.maximum(i - nb_first, 0), 0)),
                  pl.BlockSpec((ROW_BLOCK, 2 * d), row),
                  pl.BlockSpec((ROW_BLOCK, d), row),
                  lay.mod_spec(d)]
                 + const_specs,
        out_specs=out_specs,
        out_shape=out_shape,
        name="mix_out",
        compiler_params=_params("parallel"),
    )(yf, yb, bonus, g, yn0, yn1, gates, x, modp, *const_ops)


def _swiglu(h, w1, w3, w2):
    u = _dot(h, w1)
    act = u * _sigmoid(u) * _dot(h, w3)
    return _dot(act.astype(BF16), w2)


def _ffn_kernel(h_ref, x1_ref, mod_ref, w1_ref, w3_ref, w2_ref, lng_ref, lnb_ref, o_ref, *, alpha):
    f = _swiglu(h_ref[...], w1_ref[...], w3_ref[...], w2_ref[...])
    g2 = mod_ref[0, 5:6, :]
    o_ref[...] = _layer_norm(alpha * x1_ref[...] + g2 * f, lng_ref[...], lnb_ref[...])


def _ffn_dense(h2, x1, modp, w1, w3, w2, ln_g, ln_b, lay, alpha):
    n, d = x1.shape
    row = lambda i: (i, 0)
    const_specs, const_ops = _param_specs([w1, w3, w2, ln_g, ln_b])
    return pl.pallas_call(
        functools.partial(_ffn_kernel, alpha=alpha),
        grid=(n // ROW_BLOCK,),
        in_specs=[pl.BlockSpec((ROW_BLOCK, d), row), pl.BlockSpec((ROW_BLOCK, d), row), lay.mod_spec(d)]
                 + const_specs,
        out_specs=pl.BlockSpec((ROW_BLOCK, d), row),
        out_shape=jax.ShapeDtypeStruct((n, d), F32),
        name="ffn_dense",
        compiler_params=_params("parallel"),
    )(h2, x1, modp, *const_ops)


def _moe_kernel(be_ref, nb_ref, x_ref, w1_ref, w3_ref, w2_ref, o_ref):
    i = pl.program_id(0)

    @pl.when(i < nb_ref[0])
    def _():
        o_ref[...] = _swiglu(x_ref[...].astype(BF16), w1_ref[0], w3_ref[0], w2_ref[0])

    @pl.when(i >= nb_ref[0])
    def _():
        o_ref[...] = jnp.zeros(o_ref.shape, F32)


def _moe_experts(block_e, n_used, xb, w1, w3, w2, lm):
    p, d = xb.shape
    dff = w1.shape[3]
    nblk = p // MOE_ROWS
    one = pl.Buffered(1)
    expert = lambda i, be, nb: (lm, be[i], 0, 0)
    grid_spec = pltpu.PrefetchScalarGridSpec(
        num_scalar_prefetch=2,
        grid=(nblk,),
        in_specs=[pl.BlockSpec((MOE_ROWS, d), lambda i, be, nb: (i, 0)),
                  pl.BlockSpec((None, 1, d, dff), expert, pipeline_mode=one),
                  pl.BlockSpec((None, 1, d, dff), expert, pipeline_mode=one),
                  pl.BlockSpec((None, 1, dff, d), expert, pipeline_mode=one)],
        out_specs=pl.BlockSpec((MOE_ROWS, d), lambda i, be, nb: (i, 0)),
    )
    return pl.pallas_call(
        _moe_kernel,
        grid_spec=grid_spec,
        out_shape=jax.ShapeDtypeStruct((p, d), F32),
        name="moe_experts",
        compiler_params=_params("arbitrary"),
    )(block_e, n_used, xb, w1, w3, w2)


def _combine_kernel(ya_ref, yb_ref, route_ref, x1_ref, mod_ref, lng_ref, lnb_ref, o_ref, *, alpha):
    f = route_ref[:, 2:3] * ya_ref[...] + route_ref[:, 3:4] * yb_ref[...]
    g2 = mod_ref[0, 5:6, :]
    o_ref[...] = _layer_norm(alpha * x1_ref[...] + g2 * f, lng_ref[...], lnb_ref[...])


def _moe_combine(ya, yb, route, x1, modp, ln_g, ln_b, lay, alpha):
    n, d = x1.shape
    row = lambda i: (i, 0)
    return pl.pallas_call(
        functools.partial(_combine_kernel, alpha=alpha),
        grid=(n // ROW_BLOCK,),
        in_specs=[pl.BlockSpec((ROW_BLOCK, d), row), pl.BlockSpec((ROW_BLOCK, d), row),
                  pl.BlockSpec((ROW_BLOCK, V7X_LANES), row), pl.BlockSpec((ROW_BLOCK, d), row),
                  lay.mod_spec(d),
                  _const_spec(ln_g.shape), _const_spec(ln_b.shape)],
        out_specs=pl.BlockSpec((ROW_BLOCK, d), row),
        out_shape=jax.ShapeDtypeStruct((n, d), F32),
        name="moe_combine",
        compiler_params=_params("parallel"),
    )(ya, yb, route, x1, modp, ln_g, ln_b)


def _moe(h2, route, x1, modp, w1, w3, w2, lm, ln_g, ln_b, lay, alpha):
    n, d = x1.shape
    n_exp = w1.shape[1]
    top_e = route[:, 0:2].astype(jnp.int32)
    flat_e = top_e.reshape(-1)
    onehot = (flat_e[:, None] == jnp.arange(n_exp, dtype=jnp.int32)[None, :]).astype(jnp.int32)
    rank = jnp.sum((jnp.cumsum(onehot, axis=0) - onehot) * onehot, axis=1)
    counts = jnp.sum(onehot, axis=0)
    padded = (counts + MOE_ROWS - 1) // MOE_ROWS * MOE_ROWS
    pad_end = jnp.cumsum(padded)
    pad_start = pad_end - padded
    dest = pad_start[flat_e] + rank
    nblk = -(-(2 * n + n_exp * (MOE_ROWS - 1)) // MOE_ROWS)
    p = nblk * MOE_ROWS
    flat_tok = jnp.repeat(jnp.arange(n, dtype=jnp.int32), 2)
    buf_tok = jnp.full((p,), n, jnp.int32).at[dest].set(flat_tok)
    blk_start = jnp.arange(nblk, dtype=jnp.int32) * MOE_ROWS
    block_e = jnp.minimum(jnp.sum((pad_end[None, :] <= blk_start[:, None]).astype(jnp.int32), axis=1), n_exp - 1)
    n_used = (pad_end[-1] // MOE_ROWS).astype(jnp.int32).reshape(1)
    xb = jnp.concatenate([h2, jnp.zeros((1, d), h2.dtype)], axis=0)[buf_tok]
    yb = _moe_experts(block_e, n_used, xb, w1, w3, w2, lm)
    dest2 = dest.reshape(n, 2)
    return _moe_combine(yb[dest2[:, 0]], yb[dest2[:, 1]], route, x1, modp, ln_g, ln_b, lay, alpha)


def _layer(x, modp, lw, l, lay, alpha, z0, ctx_kv, new_skv):
    (bc, tc_len), (bs, ts_len) = lay.passes
    n_ctx = bc * tc_len
    slab, qn, kn, vn, k_new, v_new, gates = _in_proj(x, modp, lw["w_in"], l, lay, lw["n_slab"], *new_skv[1:])
    r, kk, lwf, lwb, bf, bb, ktf, ktb, v, bonus, g = _rwkv_prep(slab, lw, lay)
    yf, yb, s_new = _rwkv_scan(z0, r, kk, lwf, lwb, bf, bb, ktf, ktb, v, lay, new_skv[0])
    yn_ctx = _ctx_attention(qn, kn, vn, bc, tc_len)
    yn_lat = _nbr_attention(qn, kn, vn, ctx_kv[0], ctx_kv[1], lw["bias"], l, bs, ts_len, n_ctx)
    outs = _mix_out(yf, yb, bonus, g, yn_ctx, yn_lat, gates, x, modp, lw, lay, alpha, lw.get("router"))
    if "router" in lw:
        x1, h2, route = outs
        x2 = _moe(h2, route, x1, modp, lw["moe_w1"], lw["moe_w3"], lw["moe_w2"], l // 2, lw["ln2_g"], lw["ln2_b"],
                  lay, alpha)
    else:
        x1, h2 = outs
        x2 = _ffn_dense(h2, x1, modp, (lw["ffn_w1"], l // 2), (lw["ffn_w3"], l // 2), (lw["ffn_w2"], l // 2),
                        lw["ln2_g"], lw["ln2_b"], lay, alpha)
    return x2, s_new, k_new, v_new


def kernel(x_prompt, x_sample, c, state_rwkv, cache_k, cache_v, c_ctx, w_ada, b_ada, w_in, ts_w, rw_w0, rw_w2, rw_a0, rw_a2, rw_g2, rw_kk, rw_ka, rw_u, rw_gn_g, rw_gn_b, na_rpb, w_up_r, w_up_n, w_o, ln1_g, ln1_b, ln2_g, ln2_b, ffn_w1, ffn_w3, ffn_w2, moe_router, moe_w1, moe_w3, moe_w2):
    depth = w_in.shape[0]
    bc, tc_len, d = x_prompt.shape
    bs, ts_len, _ = x_sample.shape
    dr = D_BRANCH
    alpha = float((2 * depth) ** 0.25)
    n_slab = ts_w.shape[2]
    lora_w = rw_w2.shape[2]
    lora_a = rw_a2.shape[2]

    cvec = jnp.zeros((8, d), F32).at[0].set(c_ctx).at[1:1 + bs].set(c)
    mod = _ada_mod(cvec, w_ada, b_ada)

    half = np.kron(np.eye(2, dtype=np.float32), np.ones((HEAD_DIM, HEAD_DIM), np.float32))
    bd = jnp.asarray(half, BF16)
    rows = ts_len // GRID_W

    passes = ((bc, tc_len), (bs, ts_len))
    n_ctx = bc * tc_len
    bias_tables = _nbr_bias_tables(na_rpb)
    x = jnp.concatenate([x_prompt.reshape(n_ctx, d), x_sample.reshape(bs * ts_len, d)], axis=0)
    seq_row = np.concatenate([np.zeros(bc, np.int32), 1 + np.arange(bs, dtype=np.int32)])
    modp = mod[:, seq_row].reshape(depth, bc + bs, 6, d)
    z0_all = jnp.concatenate([jnp.zeros((depth, bc, 2, N_HEADS, HEAD_DIM, HEAD_DIM), F32),
                              jnp.swapaxes(jnp.swapaxes(state_rwkv, 0, 1), -1, -2)], axis=1)
    w_in_b, w_up_r_b, w_up_n_b, w_o_b = (w.astype(BF16) for w in (w_in, w_up_r, w_up_n, w_o))
    ffn_b = [w.astype(BF16) for w in (ffn_w1, ffn_w3, ffn_w2)]
    moe_b = [w.astype(BF16) for w in (moe_w1, moe_w3, moe_w2)]
    new_skv = [jnp.zeros((bc, depth, 2, N_HEADS, HEAD_DIM, HEAD_DIM), F32),
               jnp.zeros((bc, depth, dr, tc_len), F32), jnp.zeros((bc, depth, dr, tc_len), F32)]
    for l in range(depth):
        w2blk = jnp.zeros((2 * lora_w, 2 * dr), F32).at[:lora_w, :dr].set(rw_w2[l, 0]).at[lora_w:, dr:].set(rw_w2[l, 1])
        a2blk = jnp.zeros((2 * lora_a, 2 * dr), F32).at[:lora_a, :dr].set(rw_a2[l, 0]).at[lora_a:, dr:].set(rw_a2[l, 1])
        lw = dict(
            n_slab=n_slab,
            w_in=w_in_b, ts=ts_w[l],
            w0=rw_w0[l].reshape(1, 2 * dr), w2=w2blk.astype(BF16),
            a0=rw_a0[l].reshape(1, 2 * dr), a2=a2blk.astype(BF16),
            g2=rw_g2[l].astype(BF16), kkw=rw_kk[l].reshape(1, dr), ka=rw_ka[l].reshape(1, dr),
            u=rw_u[l].reshape(1, dr), bd=bd,
            gn_g=rw_gn_g[l].reshape(1, dr), gn_b=rw_gn_b[l].reshape(1, dr),
            w_up_r=(w_up_r_b, l), w_up_n=(w_up_n_b, l), w_o=(w_o_b, l),
            ln1_g=ln1_g[l].reshape(1, d), ln1_b=ln1_b[l].reshape(1, d),
            ln2_g=ln2_g[l].reshape(1, d), ln2_b=ln2_b[l].reshape(1, d),
            bias=bias_tables,
        )
        if l % 2 == 0:
            lw.update(ffn_w1=ffn_b[0], ffn_w3=ffn_b[1], ffn_w2=ffn_b[2])
        else:
            lw.update(router=moe_router[l // 2], moe_w1=moe_b[0], moe_w3=moe_b[1], moe_w2=moe_b[2])

        kv = (cache_k[:, l].reshape(bs, -1, dr), cache_v[:, l].reshape(bs, -1, dr))
        x, *new_skv = _layer(x, modp, lw, l, _Layout(passes, l), alpha, z0_all, kv, new_skv)

    cache_out = lambda a: jnp.transpose(a.reshape(bc, depth, N_HEADS, HEAD_DIM, tc_len), (0, 1, 4, 2, 3))
    return (x[:n_ctx].reshape(bc, tc_len, d), x[n_ctx:].reshape(bs, ts_len, d),
            new_skv[0], cache_out(new_skv[1]), cache_out(new_skv[2]))
```
